```python
import jax, jax.numpy as jnp
from jax import lax
import numpy as np

D_MODEL = 2048
BATCH = 2
SEQ = 8192
DEPTH = 4

GRID_W = 64
N_Q_HEADS = 16
N_KV_HEADS = 4
HEAD_DIM = 128
Q_BLOCK = 128
ROPE_THETA = 10000.0
ROPE_AXIS_DIM = HEAD_DIM // 2
ROPE_NFREQ = ROPE_AXIS_DIM // 2
POOL_WIDTH = 1024
POOL_WINDOWS = (2, 4, 8, 16)
POOL_GROUP = POOL_WIDTH // len(POOL_WINDOWS)
D_FF = 5632
N_EXPERTS = 8
TOP_K = 2
D_FF_EXPERT = 5632
NORM_EPS = 1e-6

Q_DIM = N_Q_HEADS * HEAD_DIM
KV_DIM = N_KV_HEADS * HEAD_DIM
OFF_Q = 0
OFF_K = OFF_Q + Q_DIM
OFF_V = OFF_K + KV_DIM
OFF_U = OFF_V + KV_DIM
OFF_GA = OFF_U + POOL_WIDTH
OFF_GP = OFF_GA + D_MODEL
IN_COLS = OFF_GP + D_MODEL

N_DENSE = (DEPTH + 1) // 2
N_MOE = DEPTH // 2

kernel_name = "hybrid_gqa_pool_moe_encoder"


def rms_norm(x, g):
    xf = x.astype(jnp.float32)
    y = xf * lax.rsqrt(jnp.mean(xf * xf, axis=-1, keepdims=True) + NORM_EPS)
    return (y * g.astype(jnp.float32)).astype(x.dtype)


def axial_rope_tables(S):
    rows = S // GRID_W
    row = jnp.broadcast_to(jnp.arange(rows)[:, None], (rows, GRID_W)).reshape(S).astype(jnp.float32)
    col = jnp.broadcast_to(jnp.arange(GRID_W)[None, :], (rows, GRID_W)).reshape(S).astype(jnp.float32)
    inv = ROPE_THETA ** (-jnp.arange(ROPE_NFREQ, dtype=jnp.float32) * 2.0 / ROPE_AXIS_DIM)
    ar = row[:, None] * inv
    ac = col[:, None] * inv
    ang = jnp.concatenate([ar, ar, ac, ac], axis=-1)
    return jnp.cos(ang), jnp.sin(ang)


def apply_axial_rope(x, cos, sin):
    xf = x.astype(jnp.float32)
    xs = xf.reshape(x.shape[:-1] + (2, 2, ROPE_NFREQ))
    rot = jnp.stack([-xs[..., 1, :], xs[..., 0, :]], axis=-2).reshape(x.shape)
    return (xf * cos[:, None, :] + rot * sin[:, None, :]).astype(x.dtype)


def gqa_axial_attention(q, k, v, q_g, k_g, cos, sin):
    B, S, _ = q.shape
    G = N_Q_HEADS // N_KV_HEADS
    q = q.reshape(B, S, N_Q_HEADS, HEAD_DIM)
    k = k.reshape(B, S, N_KV_HEADS, HEAD_DIM)
    v = v.reshape(B, S, N_KV_HEADS, HEAD_DIM)
    q = apply_axial_rope(rms_norm(q, q_g), cos, sin)
    k = apply_axial_rope(rms_norm(k, k_g), cos, sin)
    nblk = S // Q_BLOCK
    qb = q.reshape(B, nblk, Q_BLOCK, N_KV_HEADS, G, HEAD_DIM).transpose(1, 0, 3, 4, 2, 5)
    kt = k.transpose(0, 2, 1, 3)
    vt = v.transpose(0, 2, 1, 3)
    scale = HEAD_DIM ** -0.5

    def block(qi):
        s = jnp.einsum('bkgqd,bksd->bkgqs', qi, kt, preferred_element_type=jnp.float32) * scale
        p = jax.nn.softmax(s, axis=-1)
        return jnp.einsum('bkgqs,bksd->bkgqd', p.astype(vt.dtype), vt)

    o = lax.map(block, qb)
    return o.transpose(1, 0, 4, 2, 3, 5).reshape(B, S, N_Q_HEADS * HEAD_DIM)


def multiscale_pool(u, w_mix, scale):
    B, S, P = u.shape
    uf = u.astype(jnp.float32)
    cs = jnp.concatenate([jnp.zeros((B, 1, P), jnp.float32), jnp.cumsum(uf, axis=1)], axis=1)
    t = jnp.arange(S)
    outs = []
    for g, w in enumerate(POOL_WINDOWS):
        sl = slice(g * POOL_GROUP, (g + 1) * POOL_GROUP)
        lo = jnp.clip(t - w // 2, 0, S - 1)
        hi = jnp.clip(t + w // 2 - 1, 0, S - 1)
        csg = cs[..., sl]
        wsum = jnp.take(csg, hi + 1, axis=1) - jnp.take(csg, lo, axis=1)
        cnt = (hi - lo + 1).astype(jnp.float32)[None, :, None]
        d = wsum / cnt - uf[..., sl]
        outs.append(jnp.einsum('bsc,cd->bsd', d.astype(u.dtype), w_mix[g]))
    return jnp.concatenate(outs, axis=-1) * scale


def swiglu(xn, w_gate, w_up, w_down):
    return (jax.nn.silu(xn @ w_gate) * (xn @ w_up)) @ w_down


def moe_swiglu(xn, w_r, b_r, w_gate, w_up, w_down):
    B, S, D = xn.shape
    t = xn.reshape(B * S, D)
    logits = (t @ w_r).astype(jnp.float32) + b_r.astype(jnp.float32)
    top_v, top_i = lax.top_k(logits, TOP_K)
    top_w = jax.nn.softmax(top_v, axis=-1)
    gate = jnp.sum(jax.nn.one_hot(top_i, N_EXPERTS, dtype=jnp.float32) * top_w[..., None], axis=1)
    out = jnp.zeros((B * S, D), jnp.float32)
    for e in range(N_EXPERTS):
        ye = swiglu(t, w_gate[e], w_up[e], w_down[e]).astype(jnp.float32)
        out = out + gate[:, e:e + 1] * ye
    return out.astype(xn.dtype).reshape(B, S, D)


def setup_inputs(seed: int = 0) -> dict:
    key = jax.random.key(seed)
    ks = jax.random.split(key, 21)
    f32 = jnp.float32

    def nrm(k, shape, fan_in):
        return jax.random.normal(k, shape, f32) * (fan_in ** -0.5)

    def gain(k, shape, s=0.02):
        return 1.0 + s * jax.random.normal(k, shape, f32)

    return {
        "x": jax.random.normal(ks[0], (BATCH, SEQ, D_MODEL), f32),
        "norm1_g": gain(ks[1], (DEPTH, D_MODEL)),
        "w_in": nrm(ks[2], (DEPTH, D_MODEL, IN_COLS), D_MODEL),
        "q_norm_g": gain(ks[3], (DEPTH, HEAD_DIM)),
        "k_norm_g": gain(ks[4], (DEPTH, HEAD_DIM)),
        "w_attn_o": nrm(ks[5], (DEPTH, Q_DIM, D_MODEL), Q_DIM),
        "w_pool_mix": nrm(ks[6], (DEPTH, len(POOL_WINDOWS), POOL_GROUP, POOL_GROUP), POOL_GROUP),
        "pool_scale": gain(ks[7], (DEPTH, POOL_WIDTH), 0.1),
        "w_pool_o": nrm(ks[8], (DEPTH, POOL_WIDTH, D_MODEL), POOL_WIDTH),
        "w_out": nrm(ks[9], (DEPTH, D_MODEL, D_MODEL), D_MODEL),
        "norm2_g": gain(ks[10], (DEPTH, D_MODEL)),
        "ffn_w_gate": nrm(ks[11], (N_DENSE, D_MODEL, D_FF), D_MODEL),
        "ffn_w_up": nrm(ks[12], (N_DENSE, D_MODEL, D_FF), D_MODEL),
        "ffn_w_down": nrm(ks[13], (N_DENSE, D_FF, D_MODEL), D_FF),
        "router_w": nrm(ks[14], (N_MOE, D_MODEL, N_EXPERTS), D_MODEL),
        "router_b": 0.01 * jax.random.normal(ks[15], (N_MOE, N_EXPERTS), f32),
        "moe_w_gate": nrm(ks[16], (N_MOE, N_EXPERTS, D_MODEL, D_FF_EXPERT), D_MODEL),
        "moe_w_up": nrm(ks[17], (N_MOE, N_EXPERTS, D_MODEL, D_FF_EXPERT), D_MODEL),
        "moe_w_down": nrm(ks[18], (N_MOE, N_EXPERTS, D_FF_EXPERT, D_MODEL), D_FF_EXPERT),
        "final_g": gain(ks[19], (D_MODEL,)),
    }


def reference(x, norm1_g, w_in, q_norm_g, k_norm_g, w_attn_o, w_pool_mix, pool_scale, w_pool_o,
              w_out, norm2_g, ffn_w_gate, ffn_w_up, ffn_w_down, router_w, router_b,
              moe_w_gate, moe_w_up, moe_w_down, final_g):
    B, S, _ = x.shape
    cos, sin = axial_rope_tables(S)
    h = x
    for l in range(DEPTH):
        xn = rms_norm(h, norm1_g[l])
        proj = xn @ w_in[l]
        q = proj[..., OFF_Q:OFF_K]
        k = proj[..., OFF_K:OFF_V]
        v = proj[..., OFF_V:OFF_U]
        u = proj[..., OFF_U:OFF_GA]
        g_attn = proj[..., OFF_GA:OFF_GP]
        g_pool = proj[..., OFF_GP:IN_COLS]
        attn = gqa_axial_attention(q, k, v, q_norm_g[l], k_norm_g[l], cos, sin) @ w_attn_o[l]
        pool = multiscale_pool(u, w_pool_mix[l], pool_scale[l]) @ w_pool_o[l]
        merged = jax.nn.sigmoid(g_attn) * attn + jax.nn.sigmoid(g_pool) * pool
        h = h + merged @ w_out[l]
        xn2 = rms_norm(h, norm2_g[l])
        if l % 2 == 0:
            i = l // 2
            h = h + swiglu(xn2, ffn_w_gate[i], ffn_w_up[i], ffn_w_down[i])
        else:
            i = l // 2
            h = h + moe_swiglu(xn2, router_w[i], router_b[i], moe_w_gate[i], moe_w_up[i], moe_w_down[i])
    return rms_norm(h, final_g)
```

```python
import functools

import jax
import jax.numpy as jnp
from jax import lax
from jax.experimental import pallas as pl
from jax.experimental.pallas import tpu as pltpu

_GRID_W = 64
_ROPE_THETA = 10000.0
_POOL_WINDOWS = (2, 4, 8, 16)
_TOP_K = 2
_NORM_EPS = 1e-6

_LANES = 128
_SUBLANES = 8
_V7X_VMEM_LIMIT_BYTES = 56 * 1024 * 1024

_F32 = jnp.float32
_BF16 = jnp.bfloat16
_NT_DIMS = (((1,), (1,)), ((), ()))


def _params(*semantics):
    return pltpu.CompilerParams(dimension_semantics=semantics,
                                vmem_limit_bytes=_V7X_VMEM_LIMIT_BYTES)


def _largest_tile(n, cap, quantum):
    t = min(cap, n)
    t -= t % quantum
    while t > quantum and n % t:
        t -= quantum
    assert t >= quantum and n % t == 0, (n, cap, quantum)
    return t


def _rms_scale(x):
    return lax.rsqrt(jnp.mean(x * x, axis=-1, keepdims=True) + _NORM_EPS)


def _in_proj_kernel(h_ref, g_ref, w_ref, cos_ref, sin_ref, qg_ref, kg_ref,
                    q_ref, kv_ref, u_ref, gate_ref, xn_sc,
                    *, nq, nk, nv, nu, hd, qscale):
    j = pl.program_id(1)

    @pl.when(j == 0)
    def _():
        x = h_ref[...]
        xn_sc[...] = (x * _rms_scale(x) * g_ref[...]).astype(_BF16)

    acc = jnp.dot(xn_sc[...], w_ref[...], preferred_element_type=_F32)
    tn = acc.shape[1]

    def norm_rope(gain, scale):
        cos = cos_ref[...]
        sin = sin_ref[...]
        lane = lax.broadcasted_iota(jnp.int32, (1, hd), 1)
        first_half = (lane % (hd // 2)) < (hd // 4)
        outs = []
        for hh in range(tn // hd):
            xh = acc[:, hh * hd:(hh + 1) * hd]
            xh = xh * _rms_scale(xh) * gain
            partner = jnp.where(first_half,
                                pltpu.roll(xh, hd - hd // 4, 1),
                                pltpu.roll(xh, hd // 4, 1))
            outs.append(((xh * cos + partner * sin) * scale).astype(_BF16))
        return jnp.concatenate(outs, axis=1)

    @pl.when(j < nq)
    def _():
        q_ref[...] = norm_rope(qg_ref[...], qscale)

    @pl.when((j >= nq) & (j < nq + nk))
    def _():
        kv_ref[...] = norm_rope(kg_ref[...], 1.0)

    @pl.when((j >= nq + nk) & (j < nq + nk + nv))
    def _():
        kv_ref[...] = acc.astype(_BF16)

    @pl.when((j >= nq + nk + nv) & (j < nq + nk + nv + nu))
    def _():
        u_ref[...] = acc

    @pl.when(j >= nq + nk + nv + nu)
    def _():
        gate_ref[...] = jax.nn.sigmoid(acc).astype(_BF16)


def _in_proj(h, g1, w, cos, sin_signed, qg, kg, dims):
    n, d = h.shape
    q_dim, kv_dim, p_dim, hd, seq = dims
    cols = w.shape[1]
    tm = _largest_tile(seq, 1024, _SUBLANES)
    tn = _largest_tile(_gcd_all(q_dim, kv_dim, p_dim, d), 512, hd)
    nq, nk, nv, nu, ng = q_dim // tn, kv_dim // tn, kv_dim // tn, p_dim // tn, 2 * d // tn
    assert (nq + nk + nv + nu + ng) * tn == cols
    s_tiles = seq // tm

    def clamp(j, lo, cnt):
        return jnp.clip(j - lo, 0, cnt - 1)

    kern = functools.partial(_in_proj_kernel, nq=nq, nk=nk, nv=nv, nu=nu, hd=hd,
                             qscale=float(hd) ** -0.5)
    return pl.pallas_call(
        kern,
        grid=(n // tm, cols // tn),
        in_specs=[
            pl.BlockSpec((tm, d), lambda i, j: (i, 0)),
            pl.BlockSpec((1, d), lambda i, j: (0, 0)),
            pl.BlockSpec((d, tn), lambda i, j: (0, j)),
            pl.BlockSpec((tm, hd), lambda i, j: (i % s_tiles, 0)),
            pl.BlockSpec((tm, hd), lambda i, j: (i % s_tiles, 0)),
            pl.BlockSpec((1, hd), lambda i, j: (0, 0)),
            pl.BlockSpec((1, hd), lambda i, j: (0, 0)),
        ],
        out_specs=[
            pl.BlockSpec((tm, tn), lambda i, j: (i, clamp(j, 0, nq))),
            pl.BlockSpec((tm, tn), lambda i, j: (i, clamp(j, nq, nk + nv))),
            pl.BlockSpec((tm, tn), lambda i, j: (i, clamp(j, nq + nk + nv, nu))),
            pl.BlockSpec((tm, tn), lambda i, j: (i, clamp(j, nq + nk + nv + nu, ng))),
        ],
        out_shape=[
            jax.ShapeDtypeStruct((n, q_dim), _BF16),
            jax.ShapeDtypeStruct((n, 2 * kv_dim), _BF16),
            jax.ShapeDtypeStruct((n, p_dim), _F32),
            jax.ShapeDtypeStruct((n, 2 * d), _BF16),
        ],
        scratch_shapes=[pltpu.VMEM((tm, d), _BF16)],
        compiler_params=_params("parallel", "arbitrary"),
        name="in_proj",
    )(h, g1, w, cos, sin_signed, qg, kg)


def _gcd_all(*vals):
    import math
    g = 0
    for v in vals:
        g = math.gcd(g, v)
    return g


def _attn_kernel(q_ref, k_ref, v_ref, o_ref, qs_sc, m_sc, l_sc, acc_sc, *, groups, hd, tq, tk, seq):
    for g in range(groups):
        qs_sc[g * tq:(g + 1) * tq, :] = q_ref[:, g * hd:(g + 1) * hd]
    m_sc[...] = jnp.full(m_sc.shape, -jnp.inf, _F32)
    l_sc[...] = jnp.zeros(l_sc.shape, _F32)
    acc_sc[...] = jnp.zeros(acc_sc.shape, _F32)

    def body(c, carry):
        off = pl.multiple_of(c * tk, tk)
        k_c = k_ref[pl.ds(off, tk), :]
        v_c = v_ref[pl.ds(off, tk), :]
        s = lax.dot_general(qs_sc[...], k_c, _NT_DIMS, preferred_element_type=_F32)
        m_prev = m_sc[...]
        m_new = jnp.maximum(m_prev, jnp.max(s, axis=-1, keepdims=True))
        alpha = jnp.exp(m_prev - m_new)
        p = jnp.exp(s - m_new)
        l_sc[...] = alpha * l_sc[...] + jnp.sum(p, axis=-1, keepdims=True)
        acc_sc[...] = alpha * acc_sc[...] + jnp.dot(p.astype(_BF16), v_c,
                                                    preferred_element_type=_F32)
        m_sc[...] = m_new
        return carry

    lax.fori_loop(0, seq // tk, body, 0)
    out = acc_sc[...] / l_sc[...]
    for g in range(groups):
        o_ref[:, g * hd:(g + 1) * hd] = out[g * tq:(g + 1) * tq, :].astype(o_ref.dtype)


def _attention(q, kv, dims, batch):
    q_dim, kv_dim, _, hd, seq = dims
    n_kv = kv_dim // hd
    groups = q_dim // kv_dim
    tq = _largest_tile(seq, 256, _SUBLANES)
    tk = _largest_tile(seq, 512, _SUBLANES)
    q3 = q.reshape(batch, seq, q_dim)
    kv3 = kv.reshape(batch, seq, 2 * kv_dim)
    rows = groups * tq
    kern = functools.partial(_attn_kernel, groups=groups, hd=hd, tq=tq, tk=tk, seq=seq)
    o = pl.pallas_call(
        kern,
        grid=(batch, n_kv, seq // tq),
        in_specs=[
            pl.BlockSpec((None, tq, groups * hd), lambda b, k, i: (b, i, k)),
            pl.BlockSpec((None, seq, hd), lambda b, k, i: (b, 0, k)),
            pl.BlockSpec((None, seq, hd), lambda b, k, i: (b, 0, n_kv + k)),
        ],
        out_specs=pl.BlockSpec((None, tq, groups * hd), lambda b, k, i: (b, i, k)),
        out_shape=jax.ShapeDtypeStruct((batch, seq, q_dim), _BF16),
        scratch_shapes=[
            pltpu.VMEM((rows, hd), _BF16),
            pltpu.VMEM((rows, 1), _F32),
            pltpu.VMEM((rows, 1), _F32),
            pltpu.VMEM((rows, hd), _F32),
        ],
        compiler_params=_params("parallel", "parallel", "arbitrary"),
        name="attention",
    )(q3, kv3, kv3)
    return o.reshape(batch * seq, q_dim)


_POOL_HALO = 8


def _pool_kernel(prev_ref, u_ref, next_ref, wmix_ref, scale_ref, o_ref, ext_sc, *, tm, seq, grp):
    i = pl.program_id(0)
    halo = _POOL_HALO
    ext_sc[0:halo, :] = prev_ref[...]
    ext_sc[halo:halo + tm, :] = u_ref[...]
    ext_sc[halo + tm:halo + tm + halo, :] = next_ref[...]
    t = (i % (seq // tm)) * tm + lax.broadcasted_iota(jnp.int32, (tm, 1), 0)
    for g, w in enumerate(_POOL_WINDOWS):
        sl = slice(g * grp, (g + 1) * grp)
        wsum = jnp.zeros((tm, grp), _F32)
        for k in range(-(w // 2), w // 2):
            rows = ext_sc[halo + k:halo + k + tm, sl]
            inside = (t + k >= 0) & (t + k <= seq - 1)
            wsum = wsum + jnp.where(inside, rows, 0.0)
        lo = jnp.maximum(t - w // 2, 0)
        hi = jnp.minimum(t + w // 2 - 1, seq - 1)
        cnt = (hi - lo + 1).astype(_F32)
        dev = wsum / cnt - u_ref[:, sl]
        y = jnp.dot(dev.astype(_BF16), wmix_ref[g], preferred_element_type=_F32)
        o_ref[:, sl] = (y * scale_ref[:, sl]).astype(o_ref.dtype)


def _pool(u, wmix, scale, seq):
    n, p = u.shape
    grp = p // len(_POOL_WINDOWS)
    tm = _largest_tile(seq, 512, _POOL_HALO)
    halo = _POOL_HALO
    hb = tm // halo
    kern = functools.partial(_pool_kernel, tm=tm, seq=seq, grp=grp)
    return pl.pallas_call(
        kern,
        grid=(n // tm,),
        in_specs=[
            pl.BlockSpec((halo, p), lambda i: (jnp.maximum(i * hb - 1, 0), 0)),
            pl.BlockSpec((tm, p), lambda i: (i, 0)),
            pl.BlockSpec((halo, p), lambda i: (jnp.minimum((i + 1) * hb, n // halo - 1), 0)),
            pl.BlockSpec(wmix.shape, lambda i: (0, 0, 0)),
            pl.BlockSpec((1, p), lambda i: (0, 0)),
        ],
        out_specs=pl.BlockSpec((tm, p), lambda i: (i, 0)),
        out_shape=jax.ShapeDtypeStruct((n, p), _BF16),
        scratch_shapes=[pltpu.VMEM((tm + 2 * halo, p), _F32)],
        compiler_params=_params("parallel"),
        name="pool",
    )(u, u, u, wmix, scale)


def _merge_kernel(o_ref, p_ref, wa_ref, wp_ref, ga_ref, gp_ref, out_ref):
    a = jnp.dot(o_ref[...], wa_ref[...], preferred_element_type=_F32)
    p = jnp.dot(p_ref[...], wp_ref[...], preferred_element_type=_F32)
    merged = ga_ref[...].astype(_F32) * a + gp_ref[...].astype(_F32) * p
    out_ref[...] = merged.astype(out_ref.dtype)


def _merge(o, pooled, wa, wp, gates):
    n, q_dim = o.shape
    p_dim = pooled.shape[1]
    d = wa.shape[1]
    tm = _largest_tile(n, 1024, _SUBLANES)
    tn = _largest_tile(d, 1024, _LANES)
    nj = d // tn
    return pl.pallas_call(
        _merge_kernel,
        grid=(n // tm, nj),
        in_specs=[
            pl.BlockSpec((tm, q_dim), lambda i, j: (i, 0)),
            pl.BlockSpec((tm, p_dim), lambda i, j: (i, 0)),
            pl.BlockSpec((q_dim, tn), lambda i, j: (0, j)),
            pl.BlockSpec((p_dim, tn), lambda i, j: (0, j)),
            pl.BlockSpec((tm, tn), lambda i, j: (i, j)),
            pl.BlockSpec((tm, tn), lambda i, j: (i, nj + j)),
        ],
        out_specs=pl.BlockSpec((tm, tn), lambda i, j: (i, j)),
        out_shape=jax.ShapeDtypeStruct((n, d), _BF16),
        compiler_params=_params("parallel", "arbitrary"),
        name="merge",
    )(o, pooled, wa, wp, gates, gates)


def _out_proj_kernel(a_ref, w_ref, res_ref, out_ref):
    out_ref[...] = res_ref[...] + jnp.dot(a_ref[...], w_ref[...], preferred_element_type=_F32)


def _out_proj(merged, w, h):
    n, d = h.shape
    tm = _largest_tile(n, 1024, _SUBLANES)
    tn = _largest_tile(d, 1024, _LANES)
    return pl.pallas_call(
        _out_proj_kernel,
        grid=(n // tm, d // tn),
        in_specs=[
            pl.BlockSpec((tm, merged.shape[1]), lambda i, j: (i, 0)),
            pl.BlockSpec((merged.shape[1], tn), lambda i, j: (0, j)),
            pl.BlockSpec((tm, tn), lambda i, j: (i, j)),
        ],
        out_specs=pl.BlockSpec((tm, tn), lambda i, j: (i, j)),
        out_shape=jax.ShapeDtypeStruct((n, d), _F32),
        compiler_params=_params("parallel", "arbitrary"),
        name="out_proj",
    )(merged, w, h)


def _swiglu_step(xn, wg_ref, wu_ref, wd_ref):
    a = jnp.dot(xn, wg_ref[...], preferred_element_type=_F32)
    b = jnp.dot(xn, wu_ref[...], preferred_element_type=_F32)
    act = (a * jax.nn.sigmoid(a) * b).astype(_BF16)
    return jnp.dot(act, wd_ref[...], preferred_element_type=_F32)


def _ffn_kernel(h_ref, g_ref, wg_ref, wu_ref, wd_ref, out_ref, xn_sc):
    @pl.when(pl.program_id(1) == 0)
    def _():
        x = h_ref[...]
        xn_sc[...] = (x * _rms_scale(x) * g_ref[...]).astype(_BF16)
        out_ref[...] = x

    out_ref[...] += _swiglu_step(xn_sc[...], wg_ref, wu_ref, wd_ref)


def _ffn(h, g2, wg, wu, wd):
    n, d = h.shape
    ff = wg.shape[1]
    tm = _largest_tile(n, 512, _SUBLANES)
    tf = _largest_tile(ff, 512, _LANES)
    return pl.pallas_call(
        _ffn_kernel,
        grid=(n // tm, ff // tf),
        in_specs=[
            pl.BlockSpec((tm, d), lambda i, f: (i, 0)),
            pl.BlockSpec((1, d), lambda i, f: (0, 0)),
            pl.BlockSpec((d, tf), lambda i, f: (0, f)),
            pl.BlockSpec((d, tf), lambda i, f: (0, f)),
            pl.BlockSpec((tf, d), lambda i, f: (f, 0)),
        ],
        out_specs=pl.BlockSpec((tm, d), lambda i, f: (i, 0)),
        out_shape=jax.ShapeDtypeStruct((n, d), _F32),
        scratch_shapes=[pltpu.VMEM((tm, d), _BF16)],
        compiler_params=_params("parallel", "arbitrary"),
        name="ffn_dense",
    )(h, g2, wg, wu, wd)


_R_E1, _R_E2, _R_RANK1, _R_RANK2, _R_W1, _R_W2 = range(6)


def _router_kernel(h_ref, g_ref, wr_ref, br_ref, xp_ref, info_ref, cnt_ref, carry_sc, *, n_exp):
    @pl.when(pl.program_id(0) == 0)
    def _():
        carry_sc[...] = jnp.zeros(carry_sc.shape, _F32)

    x = h_ref[...]
    xn = x * _rms_scale(x) * g_ref[...]
    tm, d = xn.shape

    xb = xn.astype(_BF16).astype(_F32)
    lo = lax.bitcast_convert_type(xb[:, :d // 2], jnp.int32)
    hi = lax.bitcast_convert_type(xb[:, d // 2:], jnp.int32)
    xp_ref[...] = lax.shift_right_logical(lo, 16) | hi

    logits = jnp.dot(xn, wr_ref[...], preferred_element_type=_F32,
                     precision=lax.Precision.HIGHEST) + br_ref[...]
    lane = lax.broadcasted_iota(jnp.int32, logits.shape, 1)
    logits = jnp.where(lane < n_exp, logits, -jnp.inf)
    v1 = jnp.max(logits, axis=-1, keepdims=True)
    i1 = jnp.min(jnp.where(logits == v1, lane, _LANES), axis=-1, keepdims=True)
    pick1 = lane == i1
    rest = jnp.where(pick1, -jnp.inf, logits)
    v2 = jnp.max(rest, axis=-1, keepdims=True)
    i2 = jnp.min(jnp.where(rest == v2, lane, _LANES), axis=-1, keepdims=True)
    pick2 = lane == i2
    e = jnp.exp(v2 - v1)
    w1 = 1.0 / (1.0 + e)
    w2 = e / (1.0 + e)

    cnt = (pick1 | pick2).astype(_F32)
    row = lax.broadcasted_iota(jnp.int32, (tm, tm), 0)
    col = lax.broadcasted_iota(jnp.int32, (tm, tm), 1)
    below = (row > col).astype(_BF16)
    before = jnp.dot(below, cnt.astype(_BF16), preferred_element_type=_F32) + carry_sc[...]
    rank1 = jnp.sum(jnp.where(pick1, before, 0.0), axis=-1, keepdims=True)
    rank2 = jnp.sum(jnp.where(pick2, before, 0.0), axis=-1, keepdims=True)
    carry_sc[...] += jnp.sum(cnt, axis=0, keepdims=True)

    info = jnp.zeros(logits.shape, _F32)
    for slot, val in ((_R_E1, i1.astype(_F32)), (_R_E2, i2.astype(_F32)), (_R_RANK1, rank1),
                      (_R_RANK2, rank2), (_R_W1, w1), (_R_W2, w2)):
        info = jnp.where(lane == slot, val, info)
    info_ref[...] = info
    cnt_ref[...] = jnp.broadcast_to(carry_sc[...], cnt_ref.shape)


def _router(h, g2, wr_pad, br_pad, n_exp):
    n, d = h.shape
    tm = _largest_tile(n, 512, _SUBLANES)
    kern = functools.partial(_router_kernel, n_exp=n_exp)
    return pl.pallas_call(
        kern,
        grid=(n // tm,),
        in_specs=[
            pl.BlockSpec((tm, d), lambda i: (i, 0)),
            pl.BlockSpec((1, d), lambda i: (0, 0)),
            pl.BlockSpec((d, _LANES), lambda i: (0, 0)),
            pl.BlockSpec((1, _LANES), lambda i: (0, 0)),
        ],
        out_specs=[
            pl.BlockSpec((tm, d // 2), lambda i: (i, 0)),
            pl.BlockSpec((tm, _LANES), lambda i: (i, 0)),
            pl.BlockSpec((_SUBLANES, _LANES), lambda i: (0, 0)),
        ],
        out_shape=[
            jax.ShapeDtypeStruct((n, d // 2), jnp.int32),
            jax.ShapeDtypeStruct((n, _LANES), _F32),
            jax.ShapeDtypeStruct((_SUBLANES, _LANES), _F32),
        ],
        scratch_shapes=[pltpu.VMEM((1, _LANES), _F32)],
        compiler_params=_params("arbitrary"),
        name="router",
    )(h, g2, wr_pad, br_pad)


def _row_copy(src_ref, src_row, dst_ref, dst_row, sem):
    return pltpu.make_async_copy(src_ref.at[pl.ds(src_row, 1)], dst_ref.at[pl.ds(dst_row, 1)], sem)


def _dispatch_kernel(pos_ref, xp_ref, zeros_ref, xs_ref, sem, *, tc):
    del zeros_ref
    base = pl.program_id(0) * tc

    def start(t, carry):
        for k in range(_TOP_K):
            _row_copy(xp_ref, base + t, xs_ref, pos_ref[_TOP_K * t + k], sem).start()
        return carry

    lax.fori_loop(0, tc, start, 0)

    def wait(t, carry):
        _row_copy(xp_ref, 0, xs_ref, 0, sem).wait()
        return carry

    lax.fori_loop(0, _TOP_K * tc, wait, 0)


def _dispatch(pos, xp, rows):
    n, half = xp.shape
    tc = _largest_tile(n, 512, _SUBLANES)
    kern = functools.partial(_dispatch_kernel, tc=tc)
    return pl.pallas_call(
        kern,
        grid=(n // tc,),
        in_specs=[
            pl.BlockSpec((_TOP_K * tc,), lambda i: (i,), memory_space=pltpu.SMEM),
            pl.BlockSpec(memory_space=pl.ANY),
            pl.BlockSpec(memory_space=pl.ANY),
        ],
        out_specs=pl.BlockSpec(memory_space=pl.ANY),
        out_shape=jax.ShapeDtypeStruct((rows, half), jnp.int32),
        scratch_shapes=[pltpu.SemaphoreType.DMA(())],
        input_output_aliases={2: 0},
        compiler_params=_params("arbitrary"),
        name="moe_dispatch",
    )(pos, xp, jnp.zeros((rows, half), jnp.int32))


def _moe_ffn_kernel(te_ref, nact_ref, xs_ref, wg_ref, wu_ref, wd_ref, y_ref, xn_sc):
    del te_ref
    i = pl.program_id(0)
    f = pl.program_id(1)
    active = i < nact_ref[0]

    @pl.when(f == 0)
    def _():
        y_ref[...] = jnp.zeros(y_ref.shape, y_ref.dtype)

    @pl.when(active & (f == 0))
    def _():
        packed = xs_ref[...]
        lo = lax.bitcast_convert_type(lax.shift_left(packed, 16), _F32)
        hi = lax.bitcast_convert_type(packed & jnp.int32(-65536), _F32)
        xn_sc[...] = jnp.concatenate([lo, hi], axis=1).astype(_BF16)

    @pl.when(active)
    def _():
        y_ref[...] += _swiglu_step(xn_sc[...], wg_ref, wu_ref, wd_ref)


def _moe_ffn(tile_expert, n_active, xs, wg, wu, wd, tm):
    rows, half = xs.shape
    d = 2 * half
    ff = wg.shape[2]
    tf = _largest_tile(ff, 512, _LANES)
    nf = ff // tf

    def row_idx(i, nact):
        return jnp.minimum(i, nact[0] - 1)

    def f_idx(i, f, nact):
        return jnp.where(i < nact[0], f, nf - 1)

    grid_spec = pltpu.PrefetchScalarGridSpec(
        num_scalar_prefetch=2,
        grid=(rows // tm, nf),
        in_specs=[
            pl.BlockSpec((tm, half), lambda i, f, te, na: (row_idx(i, na), 0)),
            pl.BlockSpec((None, d, tf), lambda i, f, te, na: (te[i], 0, f_idx(i, f, na))),
            pl.BlockSpec((None, d, tf), lambda i, f, te, na: (te[i], 0, f_idx(i, f, na))),
            pl.BlockSpec((None, tf, d), lambda i, f, te, na: (te[i], f_idx(i, f, na), 0)),
        ],
        out_specs=pl.BlockSpec((tm, d), lambda i, f, te, na: (i, 0)),
        scratch_shapes=[pltpu.VMEM((tm, d), _BF16)],
    )
    return pl.pallas_call(
        _moe_ffn_kernel,
        grid_spec=grid_spec,
        out_shape=jax.ShapeDtypeStruct((rows, d), _F32),
        compiler_params=_params("arbitrary", "arbitrary"),
        name="moe_ffn",
    )(tile_expert, n_active, xs, wg, wu, wd)


def _combine_kernel(pos_ref, info_ref, h_ref, y_ref, g_ref, out_ref, buf, sem, *, tc, final_norm):
    def start(t, carry):
        for k in range(_TOP_K):
            pltpu.make_async_copy(y_ref.at[pl.ds(pos_ref[_TOP_K * t + k], 1)],
                                  buf.at[k, pl.ds(t, 1)], sem).start()
        return carry

    lax.fori_loop(0, tc, start, 0)

    def wait(t, carry):
        pltpu.make_async_copy(y_ref.at[pl.ds(0, 1)], buf.at[0, pl.ds(0, 1)], sem).wait()
        return carry

    lax.fori_loop(0, _TOP_K * tc, wait, 0)

    info = info_ref[...]
    w1 = info[:, _R_W1:_R_W1 + 1]
    w2 = info[:, _R_W2:_R_W2 + 1]
    out = h_ref[...] + (w1 * buf[0] + w2 * buf[1])
    if final_norm:
        out = out * _rms_scale(out) * g_ref[...]
    out_ref[...] = out


def _combine(pos, info, h, y, final_g, final_norm):
    n, d = h.shape
    tc = _largest_tile(n, 256, _SUBLANES)
    kern = functools.partial(_combine_kernel, tc=tc, final_norm=final_norm)
    return pl.pallas_call(
        kern,
        grid=(n // tc,),
        in_specs=[
            pl.BlockSpec((_TOP_K * tc,), lambda i: (i,), memory_space=pltpu.SMEM),
            pl.BlockSpec((tc, _LANES), lambda i: (i, 0)),
            pl.BlockSpec((tc, d), lambda i: (i, 0)),
            pl.BlockSpec(memory_space=pl.ANY),
            pl.BlockSpec((1, d), lambda i: (0, 0)),
        ],
        out_specs=pl.BlockSpec((tc, d), lambda i: (i, 0)),
        out_shape=jax.ShapeDtypeStruct((n, d), _F32),
        scratch_shapes=[pltpu.VMEM((_TOP_K, tc, d), _F32), pltpu.SemaphoreType.DMA(())],
        compiler_params=_params("arbitrary"),
        name="moe_combine",
    )(pos, info, h, y, final_g)


def _final_norm_kernel(h_ref, g_ref, out_ref):
    x = h_ref[...]
    out_ref[...] = x * _rms_scale(x) * g_ref[...]


def _final_norm(h, g):
    n, d = h.shape
    tm = _largest_tile(n, 512, _SUBLANES)
    return pl.pallas_call(
        _final_norm_kernel,
        grid=(n // tm,),
        in_specs=[pl.BlockSpec((tm, d), lambda i: (i, 0)), pl.BlockSpec((1, d), lambda i: (0, 0))],
        out_specs=pl.BlockSpec((tm, d), lambda i: (i, 0)),
        out_shape=jax.ShapeDtypeStruct((n, d), _F32),
        compiler_params=_params("parallel"),
        name="final_norm",
    )(h, g)


_MOE_ROW_TILE = 512


def _moe(h, g2, wr, br, wg, wu, wd, final_g, final_norm):
    n, d = h.shape
    n_exp = wr.shape[1]
    tm = _largest_tile(n, _MOE_ROW_TILE, _SUBLANES)
    wr_pad = jnp.zeros((d, _LANES), _F32).at[:, :n_exp].set(wr)
    br_pad = jnp.zeros((1, _LANES), _F32).at[0, :n_exp].set(br)
    xp, info, cnt = _router(h, g2, wr_pad, br_pad, n_exp)

    counts = cnt[0, :n_exp].astype(jnp.int32)
    tiles_per = (counts + tm - 1) // tm
    tile_end = jnp.cumsum(tiles_per)
    starts = (tile_end - tiles_per) * tm
    n_active = tile_end[-1:]
    n_tiles = (_TOP_K * n) // tm + n_exp
    tile_ids = jnp.minimum(jnp.arange(n_tiles, dtype=jnp.int32), n_active - 1)
    tile_expert = jnp.sum((tile_ids[:, None] >= tile_end[None, :]).astype(jnp.int32), axis=1)
    experts = info[:, _R_E1:_R_E2 + 1].astype(jnp.int32)
    ranks = info[:, _R_RANK1:_R_RANK2 + 1].astype(jnp.int32)
    onehot = experts[:, :, None] == jnp.arange(n_exp, dtype=jnp.int32)[None, None, :]
    pos = (jnp.sum(jnp.where(onehot, starts[None, None, :], 0), axis=-1) + ranks).reshape(-1)

    xs = _dispatch(pos, xp, n_tiles * tm)
    y = _moe_ffn(tile_expert, n_active, xs, wg, wu, wd, tm)
    return _combine(pos, info, h, y, final_g, final_norm)


def _rope_tables(seq, hd):
    axis_dim = hd // 2
    nfreq = axis_dim // 2
    rows = seq // _GRID_W
    row = jnp.broadcast_to(jnp.arange(rows)[:, None], (rows, _GRID_W)).reshape(seq).astype(_F32)
    col = jnp.broadcast_to(jnp.arange(_GRID_W)[None, :], (rows, _GRID_W)).reshape(seq).astype(_F32)
    inv = _ROPE_THETA ** (-jnp.arange(nfreq, dtype=_F32) * 2.0 / axis_dim)
    ar = row[:, None] * inv
    ac = col[:, None] * inv
    ang = jnp.concatenate([ar, ar, ac, ac], axis=-1)
    lane = jnp.arange(hd)
    sign = jnp.where((lane % axis_dim) < nfreq, -1.0, 1.0).astype(_F32)
    return jnp.cos(ang), jnp.sin(ang) * sign


def kernel(x, norm1_g, w_in, q_norm_g, k_norm_g, w_attn_o, w_pool_mix, pool_scale, w_pool_o,
           w_out, norm2_g, ffn_w_gate, ffn_w_up, ffn_w_down, router_w, router_b,
           moe_w_gate, moe_w_up, moe_w_down, final_g):
    batch, seq, d = x.shape
    depth = w_in.shape[0]
    hd = q_norm_g.shape[1]
    q_dim = w_attn_o.shape[1]
    p_dim = pool_scale.shape[1]
    kv_dim = (w_in.shape[2] - q_dim - p_dim - 2 * d) // 2
    dims = (q_dim, kv_dim, p_dim, hd, seq)
    assert seq % _GRID_W == 0 and q_dim % kv_dim == 0 and kv_dim % hd == 0

    cos, sin_signed = _rope_tables(seq, hd)
    bf = lambda w: w.astype(_BF16)
    row = lambda v: v.reshape(1, -1)

    h = x.reshape(batch * seq, d)
    fg = row(final_g)
    for l in range(depth):
        q, kv, u, gates = _in_proj(h, row(norm1_g[l]), bf(w_in[l]), cos, sin_signed,
                                   row(q_norm_g[l]), row(k_norm_g[l]), dims)
        o = _attention(q, kv, dims, batch)
        pooled = _pool(u, bf(w_pool_mix[l]), row(pool_scale[l]), seq)
        merged = _merge(o, pooled, bf(w_attn_o[l]), bf(w_pool_o[l]), gates)
        h = _out_proj(merged, bf(w_out[l]), h)
        i = l // 2
        if l % 2 == 0:
            h = _ffn(h, row(norm2_g[l]), bf(ffn_w_gate[i]), bf(ffn_w_up[i]), bf(ffn_w_down[i]))
        else:
            last = l == depth - 1
            h = _moe(h, row(norm2_g[l]), router_w[i], router_b[i], bf(moe_w_gate[i]),
                     bf(moe_w_up[i]), bf(moe_w_down[i]), fg, last)
    if depth % 2 == 1 or depth == 0:
        h = _final_norm(h, fg)
    return h.reshape(batch, seq, d)
```

```python
import functools

import jax
import jax.numpy as jnp
from jax import lax
from jax.experimental import pallas as pl
from jax.experimental.pallas import tpu as pltpu

_GRID_W = 64
_ROPE_THETA = 10000.0
_POOL_WINDOWS = (2, 4, 8, 16)
_TOP_K = 2
_NORM_EPS = 1e-6
_LOG2_E = 1.4426950408889634

_LANES = 128
_SUBLANES = 8
_V7X_VMEM_LIMIT_BYTES = 56 * 1024 * 1024

_F32 = jnp.float32
_BF16 = jnp.bfloat16
_NT_DIMS = (((1,), (1,)), ((), ()))


def _params(*semantics):
    return pltpu.CompilerParams(dimension_semantics=semantics,
                                vmem_limit_bytes=_V7X_VMEM_LIMIT_BYTES)


def _largest_tile(n, cap, quantum):
    t = min(cap, n)
    t -= t % quantum
    while t > quantum and n % t:
        t -= quantum
    assert t >= quantum and n % t == 0, (n, cap, quantum)
    return t


def _rms_scale(x):
    return lax.rsqrt(jnp.mean(x * x, axis=-1, keepdims=True) + _NORM_EPS)


def _in_proj_kernel(h_ref, g_ref, w_ref, cos_ref, sin_ref, qg_ref, kg_ref,
                    q_ref, kv_ref, u_ref, gate_ref, xn_sc,
                    *, nq, nk, nv, nu, hd, qscale):
    j = pl.program_id(1)

    @pl.when(j == 0)
    def _():
        x = h_ref[...]
        xn_sc[...] = (x * _rms_scale(x) * g_ref[...]).astype(_BF16)

    acc = jnp.dot(xn_sc[...], w_ref[...], preferred_element_type=_F32)
    tn = acc.shape[1]

    def norm_rope(gain, scale):
        cos = cos_ref[...]
        sin = sin_ref[...]
        lane = lax.broadcasted_iota(jnp.int32, (1, hd), 1)
        first_half = (lane % (hd // 2)) < (hd // 4)
        outs = []
        for hh in range(tn // hd):
            xh = acc[:, hh * hd:(hh + 1) * hd]
            xh = xh * _rms_scale(xh) * gain
            partner = jnp.where(first_half,
                                pltpu.roll(xh, hd - hd // 4, 1),
                                pltpu.roll(xh, hd // 4, 1))
            outs.append(((xh * cos + partner * sin) * scale).astype(_BF16))
        return jnp.concatenate(outs, axis=1)

    @pl.when(j < nq)
    def _():
        q_ref[...] = norm_rope(qg_ref[...], qscale)

    @pl.when((j >= nq) & (j < nq + nk))
    def _():
        kv_ref[...] = norm_rope(kg_ref[...], 1.0)

    @pl.when((j >= nq + nk) & (j < nq + nk + nv))
    def _():
        kv_ref[...] = acc.astype(_BF16)

    @pl.when((j >= nq + nk + nv) & (j < nq + nk + nv + nu))
    def _():
        u_ref[...] = acc

    @pl.when(j >= nq + nk + nv + nu)
    def _():
        gate_ref[...] = jax.nn.sigmoid(acc).astype(_BF16)


def _in_proj(h, g1, w, layer, cos, sin_signed, qg, kg, dims):
    n, d = h.shape
    q_dim, kv_dim, p_dim, hd, seq = dims
    cols = w.shape[2]
    tm = _largest_tile(seq, 1024, _SUBLANES)
    tn = _largest_tile(_gcd_all(q_dim, kv_dim, p_dim, d), 512, hd)
    nq, nk, nv, nu, ng = q_dim // tn, kv_dim // tn, kv_dim // tn, p_dim // tn, 2 * d // tn
    assert (nq + nk + nv + nu + ng) * tn == cols
    s_tiles = seq // tm

    def clamp(j, lo, cnt):
        return jnp.clip(j - lo, 0, cnt - 1)

    kern = functools.partial(_in_proj_kernel, nq=nq, nk=nk, nv=nv, nu=nu, hd=hd,
                             qscale=_LOG2_E * float(hd) ** -0.5)
    return pl.pallas_call(
        kern,
        grid=(n // tm, cols // tn),
        in_specs=[
            pl.BlockSpec((tm, d), lambda i, j: (i, 0)),
            pl.BlockSpec((1, d), lambda i, j: (0, 0)),
            pl.BlockSpec((None, d, tn), lambda i, j: (layer, 0, j)),
            pl.BlockSpec((tm, hd), lambda i, j: (i % s_tiles, 0)),
            pl.BlockSpec((tm, hd), lambda i, j: (i % s_tiles, 0)),
            pl.BlockSpec((1, hd), lambda i, j: (0, 0)),
            pl.BlockSpec((1, hd), lambda i, j: (0, 0)),
        ],
        out_specs=[
            pl.BlockSpec((tm, tn), lambda i, j: (i, clamp(j, 0, nq))),
            pl.BlockSpec((tm, tn), lambda i, j: (i, clamp(j, nq, nk + nv))),
            pl.BlockSpec((tm, tn), lambda i, j: (i, clamp(j, nq + nk + nv, nu))),
            pl.BlockSpec((tm, tn), lambda i, j: (i, clamp(j, nq + nk + nv + nu, ng))),
        ],
        out_shape=[
            jax.ShapeDtypeStruct((n, q_dim), _BF16),
            jax.ShapeDtypeStruct((n, 2 * kv_dim), _BF16),
            jax.ShapeDtypeStruct((n, p_dim), _F32),
            jax.ShapeDtypeStruct((n, 2 * d), _BF16),
        ],
        scratch_shapes=[pltpu.VMEM((tm, d), _BF16)],
        compiler_params=_params("parallel", "arbitrary"),
        name="in_proj",
    )(h, g1, w, cos, sin_signed, qg, kg)


def _gcd_all(*vals):
    import math
    g = 0
    for v in vals:
        g = math.gcd(g, v)
    return g


def _attn_kernel(q_ref, k_ref, v_ref, o_ref, qs_sc, vt_sc, s_sc, acc_sc, *, groups, hd, tq, tk, seq):
    n_chunks = seq // tk
    cols = groups * tq

    @pl.when(pl.program_id(2) == 0)
    def _():
        for c in range(n_chunks):
            vt_sc[c] = v_ref[c * tk:(c + 1) * tk, :].T

    for g in range(groups):
        qs_sc[g * tq:(g + 1) * tq, :] = q_ref[:, g * hd:(g + 1) * hd]
    acc_sc[...] = jnp.zeros(acc_sc.shape, _F32)

    def scores(c, slot):
        k_c = k_ref[pl.ds(pl.multiple_of(c * tk, tk), tk), :]
        s_sc[slot] = lax.dot_general(k_c, qs_sc[...], _NT_DIMS, preferred_element_type=_F32)

    def softmax_pv(c, slot, m_prev, l_prev):
        s = s_sc[slot]
        m_new = jnp.maximum(m_prev, jnp.max(s, axis=0, keepdims=True))
        alpha = jnp.exp2(m_prev - m_new)
        p = jnp.exp2(s - m_new)
        l_new = alpha * l_prev + jnp.sum(p, axis=0, keepdims=True)
        acc_sc[...] = acc_sc[...] * alpha + jnp.dot(vt_sc[c], p.astype(_BF16),
                                                    preferred_element_type=_F32)
        return m_new, l_new

    def pair(j, carry):
        m, l = carry
        c = 2 * j
        scores(c + 1, 1)
        m, l = softmax_pv(c, 0, m, l)
        scores(c + 2, 0)
        return softmax_pv(c + 1, 1, m, l)

    scores(0, 0)
    m0 = jnp.full((1, cols), -jnp.inf, _F32)
    l0 = jnp.zeros((1, cols), _F32)
    m, l = lax.fori_loop(0, n_chunks // 2 - 1, pair, (m0, l0))
    scores(n_chunks - 1, 1)
    m, l = softmax_pv(n_chunks - 2, 0, m, l)
    m, l = softmax_pv(n_chunks - 1, 1, m, l)

    out_t = acc_sc[...] / l
    for g in range(groups):
        o_ref[:, g * hd:(g + 1) * hd] = out_t[:, g * tq:(g + 1) * tq].T.astype(o_ref.dtype)


def _attention(q, kv, dims, batch):
    q_dim, kv_dim, _, hd, seq = dims
    n_kv = kv_dim // hd
    groups = q_dim // kv_dim
    tq = _largest_tile(seq, 256, _LANES)
    tk = _largest_tile(seq // 2, 512, _LANES)
    assert (seq // tk) % 2 == 0
    q3 = q.reshape(batch, seq, q_dim)
    kv3 = kv.reshape(batch, seq, 2 * kv_dim)
    cols = groups * tq
    kern = functools.partial(_attn_kernel, groups=groups, hd=hd, tq=tq, tk=tk, seq=seq)
    o = pl.pallas_call(
        kern,
        grid=(batch, n_kv, seq // tq),
        in_specs=[
            pl.BlockSpec((None, tq, groups * hd), lambda b, k, i: (b, i, k)),
            pl.BlockSpec((None, seq, hd), lambda b, k, i: (b, 0, k)),
            pl.BlockSpec((None, seq, hd), lambda b, k, i: (b, 0, n_kv + k)),
        ],
        out_specs=pl.BlockSpec((None, tq, groups * hd), lambda b, k, i: (b, i, k)),
        out_shape=jax.ShapeDtypeStruct((batch, seq, q_dim), _BF16),
        scratch_shapes=[
            pltpu.VMEM((cols, hd), _BF16),
            pltpu.VMEM((seq // tk, hd, tk), _BF16),
            pltpu.VMEM((2, tk, cols), _F32),
            pltpu.VMEM((hd, cols), _F32),
        ],
        compiler_params=_params("parallel", "parallel", "arbitrary"),
        name="attention",
    )(q3, kv3, kv3)
    return o.reshape(batch * seq, q_dim)


_POOL_HALO = 8


def _pool_kernel(prev_ref, u_ref, next_ref, wmix_ref, scale_ref, o_ref, ext_sc, *, tm, seq, grp):
    i = pl.program_id(0)
    halo = _POOL_HALO
    ext_sc[0:halo, :] = prev_ref[...]
    ext_sc[halo:halo + tm, :] = u_ref[...]
    ext_sc[halo + tm:halo + tm + halo, :] = next_ref[...]
    t = (i % (seq // tm)) * tm + lax.broadcasted_iota(jnp.int32, (tm, 1), 0)
    for g, w in enumerate(_POOL_WINDOWS):
        sl = slice(g * grp, (g + 1) * grp)
        wsum = jnp.zeros((tm, grp), _F32)
        for k in range(-(w // 2), w // 2):
            rows = ext_sc[halo + k:halo + k + tm, sl]
            inside = (t + k >= 0) & (t + k <= seq - 1)
            wsum = wsum + jnp.where(inside, rows, 0.0)
        lo = jnp.maximum(t - w // 2, 0)
        hi = jnp.minimum(t + w // 2 - 1, seq - 1)
        cnt = (hi - lo + 1).astype(_F32)
        dev = wsum / cnt - u_ref[:, sl]
        y = jnp.dot(dev.astype(_BF16), wmix_ref[g], preferred_element_type=_F32)
        o_ref[:, sl] = (y * scale_ref[:, sl]).astype(o_ref.dtype)


def _pool(u, wmix, scale, seq):
    n, p = u.shape
    grp = p // len(_POOL_WINDOWS)
    tm = _largest_tile(seq, 512, _POOL_HALO)
    halo = _POOL_HALO
    hb = tm // halo
    kern = functools.partial(_pool_kernel, tm=tm, seq=seq, grp=grp)
    return pl.pallas_call(
        kern,
        grid=(n // tm,),
        in_specs=[
            pl.BlockSpec((halo, p), lambda i: (jnp.maximum(i * hb - 1, 0), 0)),
            pl.BlockSpec((tm, p), lambda i: (i, 0)),
            pl.BlockSpec((halo, p), lambda i: (jnp.minimum((i + 1) * hb, n // halo - 1), 0)),
            pl.BlockSpec(wmix.shape, lambda i: (0, 0, 0)),
            pl.BlockSpec((1, p), lambda i: (0, 0)),
        ],
        out_specs=pl.BlockSpec((tm, p), lambda i: (i, 0)),
        out_shape=jax.ShapeDtypeStruct((n, p), _BF16),
        scratch_shapes=[pltpu.VMEM((tm + 2 * halo, p), _F32)],
        compiler_params=_params("parallel"),
        name="pool",
    )(u, u, u, wmix, scale)


def _merge_kernel(o_ref, p_ref, wa_ref, wp_ref, ga_ref, gp_ref, out_ref):
    a = jnp.dot(o_ref[...], wa_ref[...], preferred_element_type=_F32)
    p = jnp.dot(p_ref[...], wp_ref[...], preferred_element_type=_F32)
    merged = ga_ref[...].astype(_F32) * a + gp_ref[...].astype(_F32) * p
    out_ref[...] = merged.astype(out_ref.dtype)


def _merge(o, pooled, wa, wp, layer, gates):
    n, q_dim = o.shape
    p_dim = pooled.shape[1]
    d = wa.shape[2]
    tm = _largest_tile(n, 1024, _SUBLANES)
    tn = _largest_tile(d, 1024, _LANES)
    nj = d // tn
    return pl.pallas_call(
        _merge_kernel,
        grid=(n // tm, nj),
        in_specs=[
            pl.BlockSpec((tm, q_dim), lambda i, j: (i, 0)),
            pl.BlockSpec((tm, p_dim), lambda i, j: (i, 0)),
            pl.BlockSpec((None, q_dim, tn), lambda i, j: (layer, 0, j)),
            pl.BlockSpec((None, p_dim, tn), lambda i, j: (layer, 0, j)),
            pl.BlockSpec((tm, tn), lambda i, j: (i, j)),
            pl.BlockSpec((tm, tn), lambda i, j: (i, nj + j)),
        ],
        out_specs=pl.BlockSpec((tm, tn), lambda i, j: (i, j)),
        out_shape=jax.ShapeDtypeStruct((n, d), _BF16),
        compiler_params=_params("parallel", "arbitrary"),
        name="merge",
    )(o, pooled, wa, wp, gates, gates)


def _out_proj_kernel(a_ref, w_ref, res_ref, out_ref):
    out_ref[...] = res_ref[...] + jnp.dot(a_ref[...], w_ref[...], preferred_element_type=_F32)


def _out_proj(merged, w, layer, h):
    n, d = h.shape
    tm = _largest_tile(n, 1024, _SUBLANES)
    tn = _largest_tile(d, 1024, _LANES)
    return pl.pallas_call(
        _out_proj_kernel,
        grid=(n // tm, d // tn),
        in_specs=[
            pl.BlockSpec((tm, merged.shape[1]), lambda i, j: (i, 0)),
            pl.BlockSpec((None, merged.shape[1], tn), lambda i, j: (layer, 0, j)),
            pl.BlockSpec((tm, tn), lambda i, j: (i, j)),
        ],
        out_specs=pl.BlockSpec((tm, tn), lambda i, j: (i, j)),
        out_shape=jax.ShapeDtypeStruct((n, d), _F32),
        compiler_params=_params("parallel", "arbitrary"),
        name="out_proj",
    )(merged, w, h)


def _swiglu_step(xn, wg_ref, wu_ref, wd_ref):
    a = jnp.dot(xn, wg_ref[...], preferred_element_type=_F32)
    b = jnp.dot(xn, wu_ref[...], preferred_element_type=_F32)
    act = (a * jax.nn.sigmoid(a) * b).astype(_BF16)
    return jnp.dot(act, wd_ref[...], preferred_element_type=_F32)


def _ffn_kernel(h_ref, g_ref, wg_ref, wu_ref, wd_ref, out_ref, xn_sc):
    @pl.when(pl.program_id(1) == 0)
    def _():
        x = h_ref[...]
        xn_sc[...] = (x * _rms_scale(x) * g_ref[...]).astype(_BF16)
        out_ref[...] = x

    out_ref[...] += _swiglu_step(xn_sc[...], wg_ref, wu_ref, wd_ref)


def _ffn(h, g2, wg, wu, wd, layer):
    n, d = h.shape
    ff = wg.shape[2]
    tm = _largest_tile(n, 512, _SUBLANES)
    tf = _largest_tile(ff, 512, _LANES)
    return pl.pallas_call(
        _ffn_kernel,
        grid=(n // tm, ff // tf),
        in_specs=[
            pl.BlockSpec((tm, d), lambda i, f: (i, 0)),
            pl.BlockSpec((1, d), lambda i, f: (0, 0)),
            pl.BlockSpec((None, d, tf), lambda i, f: (layer, 0, f)),
            pl.BlockSpec((None, d, tf), lambda i, f: (layer, 0, f)),
            pl.BlockSpec((None, tf, d), lambda i, f: (layer, f, 0)),
        ],
        out_specs=pl.BlockSpec((tm, d), lambda i, f: (i, 0)),
        out_shape=jax.ShapeDtypeStruct((n, d), _F32),
        scratch_shapes=[pltpu.VMEM((tm, d), _BF16)],
        compiler_params=_params("parallel", "arbitrary"),
        name="ffn_dense",
    )(h, g2, wg, wu, wd)


_R_E1, _R_E2, _R_RANK1, _R_RANK2, _R_W1, _R_W2 = range(6)


def _router_kernel(h_ref, g_ref, wr_ref, br_ref, xp_ref, info_ref, cnt_ref, carry_sc, *, n_exp):
    @pl.when(pl.program_id(0) == 0)
    def _():
        carry_sc[...] = jnp.zeros(carry_sc.shape, _F32)

    x = h_ref[...]
    xn = x * _rms_scale(x) * g_ref[...]
    tm, d = xn.shape

    xb = xn.astype(_BF16).astype(_F32)
    lo = lax.bitcast_convert_type(xb[:, :d // 2], jnp.int32)
    hi = lax.bitcast_convert_type(xb[:, d // 2:], jnp.int32)
    xp_ref[...] = lax.shift_right_logical(lo, 16) | hi

    logits = jnp.dot(xn, wr_ref[...], preferred_element_type=_F32,
                     precision=lax.Precision.HIGHEST) + br_ref[...]
    lane = lax.broadcasted_iota(jnp.int32, logits.shape, 1)
    logits = jnp.where(lane < n_exp, logits, -jnp.inf)
    v1 = jnp.max(logits, axis=-1, keepdims=True)
    i1 = jnp.min(jnp.where(logits == v1, lane, _LANES), axis=-1, keepdims=True)
    pick1 = lane == i1
    rest = jnp.where(pick1, -jnp.inf, logits)
    v2 = jnp.max(rest, axis=-1, keepdims=True)
    i2 = jnp.min(jnp.where(rest == v2, lane, _LANES), axis=-1, keepdims=True)
    pick2 = lane == i2
    e = jnp.exp(v2 - v1)
    w1 = 1.0 / (1.0 + e)
    w2 = e / (1.0 + e)

    cnt = (pick1 | pick2).astype(_F32)
    row = lax.broadcasted_iota(jnp.int32, (tm, tm), 0)
    col = lax.broadcasted_iota(jnp.int32, (tm, tm), 1)
    below = (row > col).astype(_BF16)
    before = jnp.dot(below, cnt.astype(_BF16), preferred_element_type=_F32) + carry_sc[...]
    rank1 = jnp.sum(jnp.where(pick1, before, 0.0), axis=-1, keepdims=True)
    rank2 = jnp.sum(jnp.where(pick2, before, 0.0), axis=-1, keepdims=True)
    carry_sc[...] += jnp.sum(cnt, axis=0, keepdims=True)

    info = jnp.zeros(logits.shape, _F32)
    for slot, val in ((_R_E1, i1.astype(_F32)), (_R_E2, i2.astype(_F32)), (_R_RANK1, rank1),
                      (_R_RANK2, rank2), (_R_W1, w1), (_R_W2, w2)):
        info = jnp.where(lane == slot, val, info)
    info_ref[...] = info
    cnt_ref[...] = jnp.broadcast_to(carry_sc[...], cnt_ref.shape)


def _router(h, g2, wr_pad, br_pad, n_exp):
    n, d = h.shape
    tm = _largest_tile(n, 512, _SUBLANES)
    kern = functools.partial(_router_kernel, n_exp=n_exp)
    return pl.pallas_call(
        kern,
        grid=(n // tm,),
        in_specs=[
            pl.BlockSpec((tm, d), lambda i: (i, 0)),
            pl.BlockSpec((1, d), lambda i: (0, 0)),
            pl.BlockSpec((d, _LANES), lambda i: (0, 0)),
            pl.BlockSpec((1, _LANES), lambda i: (0, 0)),
        ],
        out_specs=[
            pl.BlockSpec((tm, d // 2), lambda i: (i, 0)),
            pl.BlockSpec((tm, _LANES), lambda i: (i, 0)),
            pl.BlockSpec((_SUBLANES, _LANES), lambda i: (0, 0)),
        ],
        out_shape=[
            jax.ShapeDtypeStruct((n, d // 2), jnp.int32),
            jax.ShapeDtypeStruct((n, _LANES), _F32),
            jax.ShapeDtypeStruct((_SUBLANES, _LANES), _F32),
        ],
        scratch_shapes=[pltpu.VMEM((1, _LANES), _F32)],
        compiler_params=_params("arbitrary"),
        name="router",
    )(h, g2, wr_pad, br_pad)


def _row_copy(src_ref, src_row, dst_ref, dst_row, sem):
    return pltpu.make_async_copy(src_ref.at[pl.ds(src_row, 1)], dst_ref.at[pl.ds(dst_row, 1)], sem)


def _dispatch_kernel(pos_ref, xp_ref, zeros_ref, xs_ref, sem, *, tc):
    del zeros_ref

    def start(t, carry):
        for k in range(_TOP_K):
            _row_copy(xp_ref, t, xs_ref, pos_ref[_TOP_K * t + k], sem).start()
        return carry

    lax.fori_loop(0, tc, start, 0)

    def wait(t, carry):
        _row_copy(xp_ref, 0, xs_ref, 0, sem).wait()
        return carry

    lax.fori_loop(0, _TOP_K * tc, wait, 0)


def _dispatch(pos, xp, rows):
    n, half = xp.shape
    tc = _largest_tile(n, 512, _SUBLANES)
    kern = functools.partial(_dispatch_kernel, tc=tc)
    return pl.pallas_call(
        kern,
        grid=(n // tc,),
        in_specs=[
            pl.BlockSpec((_TOP_K * tc,), lambda i: (i,), memory_space=pltpu.SMEM),
            pl.BlockSpec((tc, half), lambda i: (i, 0)),
            pl.BlockSpec(memory_space=pl.ANY),
        ],
        out_specs=pl.BlockSpec(memory_space=pl.ANY),
        out_shape=jax.ShapeDtypeStruct((rows, half), jnp.int32),
        scratch_shapes=[pltpu.SemaphoreType.DMA(())],
        input_output_aliases={2: 0},
        compiler_params=_params("arbitrary"),
        name="moe_dispatch",
    )(pos, xp, jnp.zeros((rows, half), jnp.int32))


def _moe_ffn_kernel(te_ref, nact_ref, xs_ref, wg_ref, wu_ref, wd_ref, y_ref, xn_sc):
    del te_ref
    i = pl.program_id(0)
    f = pl.program_id(1)
    active = i < nact_ref[0]

    @pl.when(f == 0)
    def _():
        y_ref[...] = jnp.zeros(y_ref.shape, y_ref.dtype)

    @pl.when(active & (f == 0))
    def _():
        packed = xs_ref[...]
        lo = lax.bitcast_convert_type(lax.shift_left(packed, 16), _F32)
        hi = lax.bitcast_convert_type(packed & jnp.int32(-65536), _F32)
        xn_sc[...] = jnp.concatenate([lo, hi], axis=1).astype(_BF16)

    @pl.when(active)
    def _():
        y_ref[...] += _swiglu_step(xn_sc[...], wg_ref, wu_ref, wd_ref)


def _moe_ffn(tile_expert, n_active, xs, wg, wu, wd, tm):
    rows, half = xs.shape
    d = 2 * half
    ff = wg.shape[2]
    tf = _largest_tile(ff, 512, _LANES)
    nf = ff // tf

    def row_idx(i, nact):
        return jnp.minimum(i, nact[0] - 1)

    def f_idx(i, f, nact):
        return jnp.where(i < nact[0], f, nf - 1)

    grid_spec = pltpu.PrefetchScalarGridSpec(
        num_scalar_prefetch=2,
        grid=(rows // tm, nf),
        in_specs=[
            pl.BlockSpec((tm, half), lambda i, f, te, na: (row_idx(i, na), 0)),
            pl.BlockSpec((None, d, tf), lambda i, f, te, na: (te[i], 0, f_idx(i, f, na))),
            pl.BlockSpec((None, d, tf), lambda i, f, te, na: (te[i], 0, f_idx(i, f, na))),
            pl.BlockSpec((None, tf, d), lambda i, f, te, na: (te[i], f_idx(i, f, na), 0)),
        ],
        out_specs=pl.BlockSpec((tm, d), lambda i, f, te, na: (i, 0)),
        scratch_shapes=[pltpu.VMEM((tm, d), _BF16)],
    )
    return pl.pallas_call(
        _moe_ffn_kernel,
        grid_spec=grid_spec,
        out_shape=jax.ShapeDtypeStruct((rows, d), _F32),
        compiler_params=_params("arbitrary", "arbitrary"),
        name="moe_ffn",
    )(tile_expert, n_active, xs, wg, wu, wd)


def _combine_kernel(pos_ref, info_ref, h_ref, y_ref, g_ref, out_ref, buf, sem, *, tc, final_norm):
    def start(t, carry):
        for k in range(_TOP_K):
            pltpu.make_async_copy(y_ref.at[pl.ds(pos_ref[_TOP_K * t + k], 1)],
                                  buf.at[k, pl.ds(t, 1)], sem).start()
        return carry

    lax.fori_loop(0, tc, start, 0)

    def wait(t, carry):
        pltpu.make_async_copy(y_ref.at[pl.ds(0, 1)], buf.at[0, pl.ds(0, 1)], sem).wait()
        return carry

    lax.fori_loop(0, _TOP_K * tc, wait, 0)

    info = info_ref[...]
    w1 = info[:, _R_W1:_R_W1 + 1]
    w2 = info[:, _R_W2:_R_W2 + 1]
    out = h_ref[...] + (w1 * buf[0] + w2 * buf[1])
    if final_norm:
        out = out * _rms_scale(out) * g_ref[...]
    out_ref[...] = out


def _combine(pos, info, h, y, final_g, final_norm):
    n, d = h.shape
    tc = _largest_tile(n, 256, _SUBLANES)
    kern = functools.partial(_combine_kernel, tc=tc, final_norm=final_norm)
    return pl.pallas_call(
        kern,
        grid=(n // tc,),
        in_specs=[
            pl.BlockSpec((_TOP_K * tc,), lambda i: (i,), memory_space=pltpu.SMEM),
            pl.BlockSpec((tc, _LANES), lambda i: (i, 0)),
            pl.BlockSpec((tc, d), lambda i: (i, 0)),
            pl.BlockSpec(memory_space=pl.ANY),
            pl.BlockSpec((1, d), lambda i: (0, 0)),
        ],
        out_specs=pl.BlockSpec((tc, d), lambda i: (i, 0)),
        out_shape=jax.ShapeDtypeStruct((n, d), _F32),
        scratch_shapes=[pltpu.VMEM((_TOP_K, tc, d), _F32), pltpu.SemaphoreType.DMA(())],
        compiler_params=_params("arbitrary"),
        name="moe_combine",
    )(pos, info, h, y, final_g)


def _final_norm_kernel(h_ref, g_ref, out_ref):
    x = h_ref[...]
    out_ref[...] = x * _rms_scale(x) * g_ref[...]


def _final_norm(h, g):
    n, d = h.shape
    tm = _largest_tile(n, 512, _SUBLANES)
    return pl.pallas_call(
        _final_norm_kernel,
        grid=(n // tm,),
        in_specs=[pl.BlockSpec((tm, d), lambda i: (i, 0)), pl.BlockSpec((1, d), lambda i: (0, 0))],
        out_specs=pl.BlockSpec((tm, d), lambda i: (i, 0)),
        out_shape=jax.ShapeDtypeStruct((n, d), _F32),
        compiler_params=_params("parallel"),
        name="final_norm",
    )(h, g)


_MOE_ROW_TILE = 512


def _moe(h, g2, wr, br, wg, wu, wd, layer, final_g, final_norm):
    n, d = h.shape
    n_exp = wr.shape[1]
    tm = _largest_tile(n, _MOE_ROW_TILE, _SUBLANES)
    wr_pad = jnp.zeros((d, _LANES), _F32).at[:, :n_exp].set(wr)
    br_pad = jnp.zeros((1, _LANES), _F32).at[0, :n_exp].set(br)
    xp, info, cnt = _router(h, g2, wr_pad, br_pad, n_exp)

    counts = cnt[0, :n_exp].astype(jnp.int32)
    tiles_per = (counts + tm - 1) // tm
    tile_end = jnp.cumsum(tiles_per)
    starts = (tile_end - tiles_per) * tm
    n_active = tile_end[-1:]
    n_tiles = (_TOP_K * n) // tm + n_exp
    tile_ids = jnp.minimum(jnp.arange(n_tiles, dtype=jnp.int32), n_active - 1)
    tile_expert = jnp.sum((tile_ids[:, None] >= tile_end[None, :]).astype(jnp.int32), axis=1)
    tile_expert = tile_expert + layer * n_exp
    experts = info[:, _R_E1:_R_E2 + 1].astype(jnp.int32)
    ranks = info[:, _R_RANK1:_R_RANK2 + 1].astype(jnp.int32)
    onehot = experts[:, :, None] == jnp.arange(n_exp, dtype=jnp.int32)[None, None, :]
    pos = (jnp.sum(jnp.where(onehot, starts[None, None, :], 0), axis=-1) + ranks).reshape(-1)

    xs = _dispatch(pos, xp, n_tiles * tm)
    y = _moe_ffn(tile_expert, n_active, xs, wg, wu, wd, tm)
    return _combine(pos, info, h, y, final_g, final_norm)


def _rope_tables(seq, hd):
    axis_dim = hd // 2
    nfreq = axis_dim // 2
    rows = seq // _GRID_W
    row = jnp.broadcast_to(jnp.arange(rows)[:, None], (rows, _GRID_W)).reshape(seq).astype(_F32)
    col = jnp.broadcast_to(jnp.arange(_GRID_W)[None, :], (rows, _GRID_W)).reshape(seq).astype(_F32)
    inv = _ROPE_THETA ** (-jnp.arange(nfreq, dtype=_F32) * 2.0 / axis_dim)
    ar = row[:, None] * inv
    ac = col[:, None] * inv
    ang = jnp.concatenate([ar, ar, ac, ac], axis=-1)
    lane = jnp.arange(hd)
    sign = jnp.where((lane % axis_dim) < nfreq, -1.0, 1.0).astype(_F32)
    return jnp.cos(ang), jnp.sin(ang) * sign


def kernel(x, norm1_g, w_in, q_norm_g, k_norm_g, w_attn_o, w_pool_mix, pool_scale, w_pool_o,
           w_out, norm2_g, ffn_w_gate, ffn_w_up, ffn_w_down, router_w, router_b,
           moe_w_gate, moe_w_up, moe_w_down, final_g):
    batch, seq, d = x.shape
    depth = w_in.shape[0]
    hd = q_norm_g.shape[1]
    q_dim = w_attn_o.shape[1]
    p_dim = pool_scale.shape[1]
    kv_dim = (w_in.shape[2] - q_dim - p_dim - 2 * d) // 2
    dims = (q_dim, kv_dim, p_dim, hd, seq)
    assert seq % _GRID_W == 0 and q_dim % kv_dim == 0 and kv_dim % hd == 0

    cos, sin_signed = _rope_tables(seq, hd)
    bf = lambda w: w.astype(_BF16)
    row = lambda v: v.reshape(1, -1)
    experts = lambda w: bf(w).reshape((-1,) + w.shape[2:])

    w_in, w_attn_o, w_pool_o, w_out = bf(w_in), bf(w_attn_o), bf(w_pool_o), bf(w_out)
    ffn_w_gate, ffn_w_up, ffn_w_down = bf(ffn_w_gate), bf(ffn_w_up), bf(ffn_w_down)
    moe_w_gate, moe_w_up, moe_w_down = experts(moe_w_gate), experts(moe_w_up), experts(moe_w_down)

    h = x.reshape(batch * seq, d)
    fg = row(final_g)
    for l in range(depth):
        q, kv, u, gates = _in_proj(h, row(norm1_g[l]), w_in, l, cos, sin_signed,
                                   row(q_norm_g[l]), row(k_norm_g[l]), dims)
        o = _attention(q, kv, dims, batch)
        pooled = _pool(u, bf(w_pool_mix[l]), row(pool_scale[l]), seq)
        merged = _merge(o, pooled, w_attn_o, w_pool_o, l, gates)
        h = _out_proj(merged, w_out, l, h)
        i = l // 2
        if l % 2 == 0:
            h = _ffn(h, row(norm2_g[l]), ffn_w_gate, ffn_w_up, ffn_w_down, i)
        else:
            last = l == depth - 1
            h = _moe(h, row(norm2_g[l]), router_w[i], router_b[i], moe_w_gate, moe_w_up,
                     moe_w_down, i, fg, last)
    if depth % 2 == 1 or depth == 0:
        h = _final_norm(h, fg)
    return h.reshape(batch, seq, d)
```

```python
import functools

import jax
import jax.numpy as jnp
from jax import lax
from jax.experimental import pallas as pl
from jax.experimental.pallas import tpu as pltpu

_GRID_W = 64
_ROPE_THETA = 10000.0
_POOL_WINDOWS = (2, 4, 8, 16)
_TOP_K = 2
_NORM_EPS = 1e-6
_LOG2_E = 1.4426950408889634

_LANES = 128
_SUBLANES = 8
_V7X_VMEM_LIMIT_BYTES = 56 * 1024 * 1024

_F32 = jnp.float32
_BF16 = jnp.bfloat16
_NT_DIMS = (((1,), (1,)), ((), ()))


def _params(*semantics):
    return pltpu.CompilerParams(dimension_semantics=semantics,
                                vmem_limit_bytes=_V7X_VMEM_LIMIT_BYTES)


def _largest_tile(n, cap, quantum):
    t = min(cap, n)
    t -= t % quantum
    while t > quantum and n % t:
        t -= quantum
    assert t >= quantum and n % t == 0, (n, cap, quantum)
    return t


def _rms_scale(x):
    return lax.rsqrt(jnp.mean(x * x, axis=-1, keepdims=True) + _NORM_EPS)


def _in_proj_kernel(h_ref, g_ref, w_ref, cos_ref, sin_ref, qg_ref, kg_ref,
                    q_ref, k_ref, vt_ref, u_ref, gate_ref, xn_sc, acc_sc,
                    *, nq, nk, nv, nu, ng, hd, tk, qscale):
    j = pl.program_id(1)
    ncol = nq + nk + nv + nu + ng
    tn = w_ref.shape[1]
    heads = tn // hd

    @pl.when(j == 0)
    def _():
        x = h_ref[...]
        xn_sc[...] = (x * _rms_scale(x) * g_ref[...]).astype(_BF16)

    def matmul():
        return jnp.dot(xn_sc[...], w_ref[...], preferred_element_type=_F32)

    def norm_rope(acc, gain, scale):
        cos = cos_ref[...]
        sin = sin_ref[...]
        lane = lax.broadcasted_iota(jnp.int32, (1, hd), 1)
        first_half = (lane % (hd // 2)) < (hd // 4)
        outs = []
        for hh in range(heads):
            xh = acc[:, hh * hd:(hh + 1) * hd]
            xh = xh * _rms_scale(xh) * gain
            partner = jnp.where(first_half,
                                pltpu.roll(xh, hd - hd // 4, 1),
                                pltpu.roll(xh, hd // 4, 1))
            outs.append(((xh * cos + partner * sin) * scale).astype(_BF16))
        return outs

    def finish_q(acc):
        q_ref[...] = jnp.concatenate(norm_rope(acc, qg_ref[...], qscale), axis=1)

    def finish_k(acc):
        for hh, kh in enumerate(norm_rope(acc, kg_ref[...], 1.0)):
            k_ref[hh] = kh

    def finish_v(acc):
        for hh in range(heads):
            for c in range(acc.shape[0] // tk):
                vt_ref[hh, c] = acc[c * tk:(c + 1) * tk, hh * hd:(hh + 1) * hd].T.astype(_BF16)

    def finish_u(acc):
        u_ref[...] = acc

    def finish_gate(acc):
        gate_ref[...] = (0.5 * jnp.tanh(0.5 * acc) + 0.5).astype(_BF16)

    @pl.when(j == 0)
    def _():
        acc_sc[0] = matmul()

    t = j - 1
    lo = 0
    for cnt, finish in ((nq, finish_q), (nk, finish_k), (nv, finish_v), (nu, finish_u),
                        (ng, finish_gate)):
        @pl.when((t >= lo) & (t < lo + cnt) & (j < ncol))
        def _(finish=finish):
            finish(acc_sc[t % 2])
            acc_sc[j % 2] = matmul()
        lo += cnt

    @pl.when(j == ncol)
    def _():
        finish_gate(acc_sc[t % 2])


def _attn_tiles(seq):
    tq = _largest_tile(seq, 256, _LANES)
    tk = _largest_tile(seq // 2, 512, _LANES)
    assert (seq // tk) % 2 == 0
    return tq, tk


def _in_proj(h, g1, w, layer, cos, sin_signed, qg, kg, dims):
    n, d = h.shape
    q_dim, kv_dim, p_dim, hd, seq = dims
    _, tk = _attn_tiles(seq)
    cols = w.shape[2]
    tm = _largest_tile(seq, 1024, _SUBLANES)
    tn = _largest_tile(_gcd_all(q_dim, kv_dim, p_dim, d), 512, hd)
    nq, nk, nv, nu, ng = q_dim // tn, kv_dim // tn, kv_dim // tn, p_dim // tn, 2 * d // tn
    ncol = nq + nk + nv + nu + ng
    assert ncol * tn == cols and tm % tk == 0
    s_tiles = seq // tm
    heads = tn // hd

    def tile(j, lo, cnt):
        return jnp.clip(j - 1 - lo, 0, cnt - 1)

    kern = functools.partial(_in_proj_kernel, nq=nq, nk=nk, nv=nv, nu=nu, ng=ng, hd=hd, tk=tk,
                             qscale=_LOG2_E * float(hd) ** -0.5)
    return pl.pallas_call(
        kern,
        grid=(n // tm, ncol + 1),
        in_specs=[
            pl.BlockSpec((tm, d), lambda i, j: (i, 0)),
            pl.BlockSpec((1, d), lambda i, j: (0, 0)),
            pl.BlockSpec((None, d, tn), lambda i, j: (layer, 0, jnp.minimum(j, ncol - 1))),
            pl.BlockSpec((tm, hd), lambda i, j: (i % s_tiles, 0)),
            pl.BlockSpec((tm, hd), lambda i, j: (i % s_tiles, 0)),
            pl.BlockSpec((1, hd), lambda i, j: (0, 0)),
            pl.BlockSpec((1, hd), lambda i, j: (0, 0)),
        ],
        out_specs=[
            pl.BlockSpec((tm, tn), lambda i, j: (i, tile(j, 0, nq))),
            pl.BlockSpec((heads, tm, hd), lambda i, j: (tile(j, nq, nk), i, 0)),
            pl.BlockSpec((heads, tm // tk, hd, tk), lambda i, j: (tile(j, nq + nk, nv), i, 0, 0)),
            pl.BlockSpec((tm, tn), lambda i, j: (i, tile(j, nq + nk + nv, nu))),
            pl.BlockSpec((tm, tn), lambda i, j: (i, tile(j, nq + nk + nv + nu, ng))),
        ],
        out_shape=[
            jax.ShapeDtypeStruct((n, q_dim), _BF16),
            jax.ShapeDtypeStruct((kv_dim // hd, n, hd), _BF16),
            jax.ShapeDtypeStruct((kv_dim // hd, n // tk, hd, tk), _BF16),
            jax.ShapeDtypeStruct((n, p_dim), _F32),
            jax.ShapeDtypeStruct((n, 2 * d), _BF16),
        ],
        scratch_shapes=[pltpu.VMEM((tm, d), _BF16), pltpu.VMEM((2, tm, tn), _F32)],
        compiler_params=_params("parallel", "arbitrary"),
        name="in_proj",
    )(h, g1, w, cos, sin_signed, qg, kg)


def _gcd_all(*vals):
    import math
    g = 0
    for v in vals:
        g = math.gcd(g, v)
    return g


def _attn_kernel(q_ref, k_ref, vt_ref, o_ref, qs_sc, s_sc, acc_sc, *, groups, hd, tq, tk, seq):
    n_chunks = seq // tk
    cols = groups * tq

    for g in range(groups):
        qs_sc[g * tq:(g + 1) * tq, :] = q_ref[:, g * hd:(g + 1) * hd]
    acc_sc[...] = jnp.zeros(acc_sc.shape, _F32)

    def scores(c, slot):
        k_c = k_ref[pl.ds(pl.multiple_of(c * tk, tk), tk), :]
        s_sc[slot] = lax.dot_general(k_c, qs_sc[...], _NT_DIMS, preferred_element_type=_F32)

    def softmax_pv(c, slot, m_prev, l_prev):
        s = s_sc[slot]
        m_new = jnp.maximum(m_prev, jnp.max(s, axis=0, keepdims=True))
        alpha = jnp.exp2(m_prev - m_new)
        p = jnp.exp2(s - m_new)
        l_new = alpha * l_prev + jnp.sum(p, axis=0, keepdims=True)
        acc_sc[...] = acc_sc[...] * alpha + jnp.dot(vt_ref[c], p.astype(_BF16),
                                                    preferred_element_type=_F32)
        return m_new, l_new

    def pair(j, carry):
        m, l = carry
        c = 2 * j
        scores(c + 1, 1)
        m, l = softmax_pv(c, 0, m, l)
        scores(c + 2, 0)
        return softmax_pv(c + 1, 1, m, l)

    scores(0, 0)
    m0 = jnp.full((1, cols), -jnp.inf, _F32)
    l0 = jnp.zeros((1, cols), _F32)
    m, l = lax.fori_loop(0, n_chunks // 2 - 1, pair, (m0, l0))
    scores(n_chunks - 1, 1)
    m, l = softmax_pv(n_chunks - 2, 0, m, l)
    m, l = softmax_pv(n_chunks - 1, 1, m, l)

    out_t = acc_sc[...] / l
    for g in range(groups):
        o_ref[:, g * hd:(g + 1) * hd] = out_t[:, g * tq:(g + 1) * tq].T.astype(o_ref.dtype)


def _attention(q, k, vt, dims, batch):
    q_dim, kv_dim, _, hd, seq = dims
    n_kv = kv_dim // hd
    groups = q_dim // kv_dim
    tq, tk = _attn_tiles(seq)
    q3 = q.reshape(batch, seq, q_dim)
    cols = groups * tq
    kern = functools.partial(_attn_kernel, groups=groups, hd=hd, tq=tq, tk=tk, seq=seq)
    o = pl.pallas_call(
        kern,
        grid=(batch, n_kv, seq // tq),
        in_specs=[
            pl.BlockSpec((None, tq, groups * hd), lambda b, kh, i: (b, i, kh)),
            pl.BlockSpec((None, seq, hd), lambda b, kh, i: (kh, b, 0)),
            pl.BlockSpec((None, seq // tk, hd, tk), lambda b, kh, i: (kh, b, 0, 0)),
        ],
        out_specs=pl.BlockSpec((None, tq, groups * hd), lambda b, kh, i: (b, i, kh)),
        out_shape=jax.ShapeDtypeStruct((batch, seq, q_dim), _BF16),
        scratch_shapes=[
            pltpu.VMEM((cols, hd), _BF16),
            pltpu.VMEM((2, tk, cols), _F32),
            pltpu.VMEM((hd, cols), _F32),
        ],
        compiler_params=_params("parallel", "parallel", "parallel"),
        name="attention",
    )(q3, k, vt)
    return o.reshape(batch * seq, q_dim)


_POOL_HALO = 8


def _pool_kernel(prev_ref, u_ref, next_ref, wmix_ref, scale_ref, o_ref, ext_sc, *, tm, seq, grp):
    i = pl.program_id(0)
    halo = _POOL_HALO
    ext_sc[0:halo, :] = prev_ref[...]
    ext_sc[halo:halo + tm, :] = u_ref[...]
    ext_sc[halo + tm:halo + tm + halo, :] = next_ref[...]
    t = (i % (seq // tm)) * tm + lax.broadcasted_iota(jnp.int32, (tm, 1), 0)
    for g, w in enumerate(_POOL_WINDOWS):
        sl = slice(g * grp, (g + 1) * grp)
        wsum = jnp.zeros((tm, grp), _F32)
        for k in range(-(w // 2), w // 2):
            rows = ext_sc[halo + k:halo + k + tm, sl]
            inside = (t + k >= 0) & (t + k <= seq - 1)
            wsum = wsum + jnp.where(inside, rows, 0.0)
        lo = jnp.maximum(t - w // 2, 0)
        hi = jnp.minimum(t + w // 2 - 1, seq - 1)
        cnt = (hi - lo + 1).astype(_F32)
        dev = wsum / cnt - u_ref[:, sl]
        y = jnp.dot(dev.astype(_BF16), wmix_ref[g], preferred_element_type=_F32)
        o_ref[:, sl] = (y * scale_ref[:, sl]).astype(o_ref.dtype)


def _pool(u, wmix, scale, seq):
    n, p = u.shape
    grp = p // len(_POOL_WINDOWS)
    tm = _largest_tile(seq, 512, _POOL_HALO)
    halo = _POOL_HALO
    hb = tm // halo
    kern = functools.partial(_pool_kernel, tm=tm, seq=seq, grp=grp)
    return pl.pallas_call(
        kern,
        grid=(n // tm,),
        in_specs=[
            pl.BlockSpec((halo, p), lambda i: (jnp.maximum(i * hb - 1, 0), 0)),
            pl.BlockSpec((tm, p), lambda i: (i, 0)),
            pl.BlockSpec((halo, p), lambda i: (jnp.minimum((i + 1) * hb, n // halo - 1), 0)),
            pl.BlockSpec(wmix.shape, lambda i: (0, 0, 0)),
            pl.BlockSpec((1, p), lambda i: (0, 0)),
        ],
        out_specs=pl.BlockSpec((tm, p), lambda i: (i, 0)),
        out_shape=jax.ShapeDtypeStruct((n, p), _BF16),
        scratch_shapes=[pltpu.VMEM((tm + 2 * halo, p), _F32)],
        compiler_params=_params("parallel"),
        name="pool",
    )(u, u, u, wmix, scale)


def _merge_kernel(o_ref, p_ref, wa_ref, wp_ref, ga_ref, gp_ref, out_ref):
    a = jnp.dot(o_ref[...], wa_ref[...], preferred_element_type=_F32)
    p = jnp.dot(p_ref[...], wp_ref[...], preferred_element_type=_F32)
    merged = ga_ref[...].astype(_F32) * a + gp_ref[...].astype(_F32) * p
    out_ref[...] = merged.astype(out_ref.dtype)


def _merge(o, pooled, wa, wp, layer, gates):
    n, q_dim = o.shape
    p_dim = pooled.shape[1]
    d = wa.shape[2]
    tm = _largest_tile(n, 1024, _SUBLANES)
    tn = _largest_tile(d, 1024, _LANES)
    nj = d // tn
    return pl.pallas_call(
        _merge_kernel,
        grid=(n // tm, nj),
        in_specs=[
            pl.BlockSpec((tm, q_dim), lambda i, j: (i, 0)),
            pl.BlockSpec((tm, p_dim), lambda i, j: (i, 0)),
            pl.BlockSpec((None, q_dim, tn), lambda i, j: (layer, 0, j)),
            pl.BlockSpec((None, p_dim, tn), lambda i, j: (layer, 0, j)),
            pl.BlockSpec((tm, tn), lambda i, j: (i, j)),
            pl.BlockSpec((tm, tn), lambda i, j: (i, nj + j)),
        ],
        out_specs=pl.BlockSpec((tm, tn), lambda i, j: (i, j)),
        out_shape=jax.ShapeDtypeStruct((n, d), _BF16),
        compiler_params=_params("parallel", "arbitrary"),
        name="merge",
    )(o, pooled, wa, wp, gates, gates)


def _out_proj_kernel(a_ref, w_ref, res_ref, out_ref):
    out_ref[...] = res_ref[...] + jnp.dot(a_ref[...], w_ref[...], preferred_element_type=_F32)


def _out_proj(merged, w, layer, h):
    n, d = h.shape
    tm = _largest_tile(n, 1024, _SUBLANES)
    tn = _largest_tile(d, 1024, _LANES)
    return pl.pallas_call(
        _out_proj_kernel,
        grid=(n // tm, d // tn),
        in_specs=[
            pl.BlockSpec((tm, merged.shape[1]), lambda i, j: (i, 0)),
            pl.BlockSpec((None, merged.shape[1], tn), lambda i, j: (layer, 0, j)),
            pl.BlockSpec((tm, tn), lambda i, j: (i, j)),
        ],
        out_specs=pl.BlockSpec((tm, tn), lambda i, j: (i, j)),
        out_shape=jax.ShapeDtypeStruct((n, d), _F32),
        compiler_params=_params("parallel", "arbitrary"),
        name="out_proj",
    )(merged, w, h)


def _swiglu_step(xn, wg_ref, wu_ref, wd_ref):
    a = jnp.dot(xn, wg_ref[...], preferred_element_type=_F32)
    b = jnp.dot(xn, wu_ref[...], preferred_element_type=_F32)
    act = (a * jax.nn.sigmoid(a) * b).astype(_BF16)
    return jnp.dot(act, wd_ref[...], preferred_element_type=_F32)


def _ffn_kernel(h_ref, g_ref, wg_ref, wu_ref, wd_ref, out_ref, xn_sc):
    @pl.when(pl.program_id(1) == 0)
    def _():
        x = h_ref[...]
        xn_sc[...] = (x * _rms_scale(x) * g_ref[...]).astype(_BF16)
        out_ref[...] = x

    out_ref[...] += _swiglu_step(xn_sc[...], wg_ref, wu_ref, wd_ref)


def _ffn(h, g2, wg, wu, wd, layer):
    n, d = h.shape
    ff = wg.shape[2]
    tm = _largest_tile(n, 512, _SUBLANES)
    tf = _largest_tile(ff, 512, _LANES)
    return pl.pallas_call(
        _ffn_kernel,
        grid=(n // tm, ff // tf),
        in_specs=[
            pl.BlockSpec((tm, d), lambda i, f: (i, 0)),
            pl.BlockSpec((1, d), lambda i, f: (0, 0)),
            pl.BlockSpec((None, d, tf), lambda i, f: (layer, 0, f)),
            pl.BlockSpec((None, d, tf), lambda i, f: (layer, 0, f)),
            pl.BlockSpec((None, tf, d), lambda i, f: (layer, f, 0)),
        ],
        out_specs=pl.BlockSpec((tm, d), lambda i, f: (i, 0)),
        out_shape=jax.ShapeDtypeStruct((n, d), _F32),
        scratch_shapes=[pltpu.VMEM((tm, d), _BF16)],
        compiler_params=_params("parallel", "arbitrary"),
        name="ffn_dense",
    )(h, g2, wg, wu, wd)


_R_E1, _R_E2, _R_RANK1, _R_RANK2, _R_W1, _R_W2 = range(6)


def _router_kernel(h_ref, g_ref, wr_ref, br_ref, xp_ref, info_ref, cnt_ref, carry_sc, *, n_exp):
    @pl.when(pl.program_id(0) == 0)
    def _():
        carry_sc[...] = jnp.zeros(carry_sc.shape, _F32)

    x = h_ref[...]
    xn = x * _rms_scale(x) * g_ref[...]
    tm, d = xn.shape

    xb = xn.astype(_BF16).astype(_F32)
    lo = lax.bitcast_convert_type(xb[:, :d // 2], jnp.int32)
    hi = lax.bitcast_convert_type(xb[:, d // 2:], jnp.int32)
    xp_ref[...] = lax.shift_right_logical(lo, 16) | hi

    logits = jnp.dot(xn, wr_ref[...], preferred_element_type=_F32,
                     precision=lax.Precision.HIGHEST) + br_ref[...]
    lane = lax.broadcasted_iota(jnp.int32, logits.shape, 1)
    logits = jnp.where(lane < n_exp, logits, -jnp.inf)
    v1 = jnp.max(logits, axis=-1, keepdims=True)
    i1 = jnp.min(jnp.where(logits == v1, lane, _LANES), axis=-1, keepdims=True)
    pick1 = lane == i1
    rest = jnp.where(pick1, -jnp.inf, logits)
    v2 = jnp.max(rest, axis=-1, keepdims=True)
    i2 = jnp.min(jnp.where(rest == v2, lane, _LANES), axis=-1, keepdims=True)
    pick2 = lane == i2
    e = jnp.exp(v2 - v1)
    w1 = 1.0 / (1.0 + e)
    w2 = e / (1.0 + e)

    cnt = (pick1 | pick2).astype(_F32)
    row = lax.broadcasted_iota(jnp.int32, (tm, tm), 0)
    col = lax.broadcasted_iota(jnp.int32, (tm, tm), 1)
    below = (row > col).astype(_BF16)
    before = jnp.dot(below, cnt.astype(_BF16), preferred_element_type=_F32) + carry_sc[...]
    rank1 = jnp.sum(jnp.where(pick1, before, 0.0), axis=-1, keepdims=True)
    rank2 = jnp.sum(jnp.where(pick2, before, 0.0), axis=-1, keepdims=True)
    carry_sc[...] += jnp.sum(cnt, axis=0, keepdims=True)

    info = jnp.zeros(logits.shape, _F32)
    for slot, val in ((_R_E1, i1.astype(_F32)), (_R_E2, i2.astype(_F32)), (_R_RANK1, rank1),
                      (_R_RANK2, rank2), (_R_W1, w1), (_R_W2, w2)):
        info = jnp.where(lane == slot, val, info)
    info_ref[...] = info
    cnt_ref[...] = jnp.broadcast_to(carry_sc[...], cnt_ref.shape)


def _router(h, g2, wr_pad, br_pad, n_exp):
    n, d = h.shape
    tm = _largest_tile(n, 512, _SUBLANES)
    kern = functools.partial(_router_kernel, n_exp=n_exp)
    return pl.pallas_call(
        kern,
        grid=(n // tm,),
        in_specs=[
            pl.BlockSpec((tm, d), lambda i: (i, 0)),
            pl.BlockSpec((1, d), lambda i: (0, 0)),
            pl.BlockSpec((d, _LANES), lambda i: (0, 0)),
            pl.BlockSpec((1, _LANES), lambda i: (0, 0)),
        ],
        out_specs=[
            pl.BlockSpec((tm, d // 2), lambda i: (i, 0)),
            pl.BlockSpec((tm, _LANES), lambda i: (i, 0)),
            pl.BlockSpec((_SUBLANES, _LANES), lambda i: (0, 0)),
        ],
        out_shape=[
            jax.ShapeDtypeStruct((n, d // 2), jnp.int32),
            jax.ShapeDtypeStruct((n, _LANES), _F32),
            jax.ShapeDtypeStruct((_SUBLANES, _LANES), _F32),
        ],
        scratch_shapes=[pltpu.VMEM((1, _LANES), _F32)],
        compiler_params=_params("arbitrary"),
        name="router",
    )(h, g2, wr_pad, br_pad)


def _row_copy(src_ref, src_row, dst_ref, dst_row, sem):
    return pltpu.make_async_copy(src_ref.at[pl.ds(src_row, 1)], dst_ref.at[pl.ds(dst_row, 1)], sem)


def _dispatch_kernel(pos_ref, xp_ref, zeros_ref, xs_ref, sem, *, tc):
    del zeros_ref

    def start(t, carry):
        for k in range(_TOP_K):
            _row_copy(xp_ref, t, xs_ref, pos_ref[_TOP_K * t + k], sem).start()
        return carry

    lax.fori_loop(0, tc, start, 0)

    def wait(t, carry):
        _row_copy(xp_ref, 0, xs_ref, 0, sem).wait()
        return carry

    lax.fori_loop(0, _TOP_K * tc, wait, 0)


def _dispatch(pos, xp, rows):
    n, half = xp.shape
    tc = _largest_tile(n, 512, _SUBLANES)
    kern = functools.partial(_dispatch_kernel, tc=tc)
    return pl.pallas_call(
        kern,
        grid=(n // tc,),
        in_specs=[
            pl.BlockSpec((_TOP_K * tc,), lambda i: (i,), memory_space=pltpu.SMEM),
            pl.BlockSpec((tc, half), lambda i: (i, 0)),
            pl.BlockSpec(memory_space=pl.ANY),
        ],
        out_specs=pl.BlockSpec(memory_space=pl.ANY),
        out_shape=jax.ShapeDtypeStruct((rows, half), jnp.int32),
        scratch_shapes=[pltpu.SemaphoreType.DMA(())],
        input_output_aliases={2: 0},
        compiler_params=_params("arbitrary"),
        name="moe_dispatch",
    )(pos, xp, jnp.zeros((rows, half), jnp.int32))


def _moe_ffn_kernel(te_ref, nact_ref, xs_ref, wg_ref, wu_ref, wd_ref, y_ref, xn_sc):
    del te_ref
    i = pl.program_id(0)
    f = pl.program_id(1)
    active = i < nact_ref[0]

    @pl.when(f == 0)
    def _():
        y_ref[...] = jnp.zeros(y_ref.shape, y_ref.dtype)

    @pl.when(active & (f == 0))
    def _():
        packed = xs_ref[...]
        lo = lax.bitcast_convert_type(lax.shift_left(packed, 16), _F32)
        hi = lax.bitcast_convert_type(packed & jnp.int32(-65536), _F32)
        xn_sc[...] = jnp.concatenate([lo, hi], axis=1).astype(_BF16)

    @pl.when(active)
    def _():
        y_ref[...] += _swiglu_step(xn_sc[...], wg_ref, wu_ref, wd_ref)


def _moe_ffn(tile_expert, n_active, xs, wg, wu, wd, tm):
    rows, half = xs.shape
    d = 2 * half
    ff = wg.shape[2]
    tf = _largest_tile(ff, 512, _LANES)
    nf = ff // tf

    def row_idx(i, nact):
        return jnp.maximum(jnp.minimum(i, nact[0] - 1), 0)

    def f_idx(i, f, nact):
        return jnp.where(i < nact[0], f, nf - 1)

    grid_spec = pltpu.PrefetchScalarGridSpec(
        num_scalar_prefetch=2,
        grid=(rows // tm, nf),
        in_specs=[
            pl.BlockSpec((tm, half), lambda i, f, te, na: (row_idx(i, na), 0)),
            pl.BlockSpec((None, d, tf), lambda i, f, te, na: (te[i], 0, f_idx(i, f, na))),
            pl.BlockSpec((None, d, tf), lambda i, f, te, na: (te[i], 0, f_idx(i, f, na))),
            pl.BlockSpec((None, tf, d), lambda i, f, te, na: (te[i], f_idx(i, f, na), 0)),
        ],
        out_specs=pl.BlockSpec((tm, d), lambda i, f, te, na: (i, 0)),
        scratch_shapes=[pltpu.VMEM((tm, d), _BF16)],
    )
    return pl.pallas_call(
        _moe_ffn_kernel,
        grid_spec=grid_spec,
        out_shape=jax.ShapeDtypeStruct((rows, d), _F32),
        compiler_params=_params("arbitrary", "arbitrary"),
        name="moe_ffn",
    )(tile_expert, n_active, xs, wg, wu, wd)


def _combine_kernel(pos_ref, info_ref, h_ref, y_ref, g_ref, out_ref, buf, sem, *, tc, final_norm):
    def start(t, carry):
        for k in range(_TOP_K):
            pltpu.make_async_copy(y_ref.at[pl.ds(pos_ref[_TOP_K * t + k], 1)],
                                  buf.at[k, pl.ds(t, 1)], sem).start()
        return carry

    lax.fori_loop(0, tc, start, 0)

    def wait(t, carry):
        pltpu.make_async_copy(y_ref.at[pl.ds(0, 1)], buf.at[0, pl.ds(0, 1)], sem).wait()
        return carry

    lax.fori_loop(0, _TOP_K * tc, wait, 0)

    info = info_ref[...]
    w1 = info[:, _R_W1:_R_W1 + 1]
    w2 = info[:, _R_W2:_R_W2 + 1]
    out = h_ref[...] + (w1 * buf[0] + w2 * buf[1])
    if final_norm:
        out = out * _rms_scale(out) * g_ref[...]
    out_ref[...] = out


def _combine(pos, info, h, y, final_g, final_norm):
    n, d = h.shape
    tc = _largest_tile(n, 256, _SUBLANES)
    kern = functools.partial(_combine_kernel, tc=tc, final_norm=final_norm)
    return pl.pallas_call(
        kern,
        grid=(n // tc,),
        in_specs=[
            pl.BlockSpec((_TOP_K * tc,), lambda i: (i,), memory_space=pltpu.SMEM),
            pl.BlockSpec((tc, _LANES), lambda i: (i, 0)),
            pl.BlockSpec((tc, d), lambda i: (i, 0)),
            pl.BlockSpec(memory_space=pl.ANY),
            pl.BlockSpec((1, d), lambda i: (0, 0)),
        ],
        out_specs=pl.BlockSpec((tc, d), lambda i: (i, 0)),
        out_shape=jax.ShapeDtypeStruct((n, d), _F32),
        scratch_shapes=[pltpu.VMEM((_TOP_K, tc, d), _F32), pltpu.SemaphoreType.DMA(())],
        compiler_params=_params("arbitrary"),
        name="moe_combine",
    )(pos, info, h, y, final_g)


def _final_norm_kernel(h_ref, g_ref, out_ref):
    x = h_ref[...]
    out_ref[...] = x * _rms_scale(x) * g_ref[...]


def _final_norm(h, g):
    n, d = h.shape
    tm = _largest_tile(n, 512, _SUBLANES)
    return pl.pallas_call(
        _final_norm_kernel,
        grid=(n // tm,),
        in_specs=[pl.BlockSpec((tm, d), lambda i: (i, 0)), pl.BlockSpec((1, d), lambda i: (0, 0))],
        out_specs=pl.BlockSpec((tm, d), lambda i: (i, 0)),
        out_shape=jax.ShapeDtypeStruct((n, d), _F32),
        compiler_params=_params("parallel"),
        name="final_norm",
    )(h, g)


_MOE_ROW_TILE = 512


def _moe(h, g2, wr, br, wg, wu, wd, layer, final_g, final_norm):
    n, d = h.shape
    n_exp = wr.shape[1]
    tm = _largest_tile(n, _MOE_ROW_TILE, _SUBLANES)
    wr_pad = jnp.zeros((d, _LANES), _F32).at[:, :n_exp].set(wr)
    br_pad = jnp.zeros((1, _LANES), _F32).at[0, :n_exp].set(br)
    xp, info, cnt = _router(h, g2, wr_pad, br_pad, n_exp)

    counts = cnt[0, :n_exp].astype(jnp.int32)
    tiles_per = (counts + tm - 1) // tm
    tile_end = jnp.cumsum(tiles_per)
    starts = (tile_end - tiles_per) * tm
    n_active = tile_end[-1:]
    n_tiles = (_TOP_K * n) // tm + n_exp
    tile_ids = jnp.minimum(jnp.arange(n_tiles, dtype=jnp.int32), n_active - 1)
    tile_expert = jnp.sum((tile_ids[:, None] >= tile_end[None, :]).astype(jnp.int32), axis=1)
    tile_expert = tile_expert + layer * n_exp
    experts = info[:, _R_E1:_R_E2 + 1].astype(jnp.int32)
    ranks = info[:, _R_RANK1:_R_RANK2 + 1].astype(jnp.int32)
    onehot = experts[:, :, None] == jnp.arange(n_exp, dtype=jnp.int32)[None, None, :]
    pos = (jnp.sum(jnp.where(onehot, starts[None, None, :], 0), axis=-1) + ranks).reshape(-1)

    xs = _dispatch(pos, xp, n_tiles * tm)
    y = _moe_ffn(tile_expert, n_active, xs, wg, wu, wd, tm)
    return _combine(pos, info, h, y, final_g, final_norm)


def _rope_tables(seq, hd):
    axis_dim = hd // 2
    nfreq = axis_dim // 2
    rows = seq // _GRID_W
    row = jnp.broadcast_to(jnp.arange(rows)[:, None], (rows, _GRID_W)).reshape(seq).astype(_F32)
    col = jnp.broadcast_to(jnp.arange(_GRID_W)[None, :], (rows, _GRID_W)).reshape(seq).astype(_F32)
    inv = _ROPE_THETA ** (-jnp.arange(nfreq, dtype=_F32) * 2.0 / axis_dim)
    ar = row[:, None] * inv
    ac = col[:, None] * inv
    ang = jnp.concatenate([ar, ar, ac, ac], axis=-1)
    lane = jnp.arange(hd)
    sign = jnp.where((lane % axis_dim) < nfreq, -1.0, 1.0).astype(_F32)
    return jnp.cos(ang), jnp.sin(ang) * sign


def kernel(x, norm1_g, w_in, q_norm_g, k_norm_g, w_attn_o, w_pool_mix, pool_scale, w_pool_o,
           w_out, norm2_g, ffn_w_gate, ffn_w_up, ffn_w_down, router_w, router_b,
           moe_w_gate, moe_w_up, moe_w_down, final_g):
    batch, seq, d = x.shape
    depth = w_in.shape[0]
    hd = q_norm_g.shape[1]
    q_dim = w_attn_o.shape[1]
    p_dim = pool_scale.shape[1]
    kv_dim = (w_in.shape[2] - q_dim - p_dim - 2 * d) // 2
    dims = (q_dim, kv_dim, p_dim, hd, seq)
    assert seq % _GRID_W == 0 and q_dim % kv_dim == 0 and kv_dim % hd == 0

    cos, sin_signed = _rope_tables(seq, hd)
    bf = lambda w: w.astype(_BF16)
    row = lambda v: v.reshape(1, -1)
    experts = lambda w: bf(w).reshape((-1,) + w.shape[2:])

    w_in, w_attn_o, w_pool_o, w_out = bf(w_in), bf(w_attn_o), bf(w_pool_o), bf(w_out)
    ffn_w_gate, ffn_w_up, ffn_w_down = bf(ffn_w_gate), bf(ffn_w_up), bf(ffn_w_down)
    moe_w_gate, moe_w_up, moe_w_down = experts(moe_w_gate), experts(moe_w_up), experts(moe_w_down)

    h = x.reshape(batch * seq, d)
    fg = row(final_g)
    for l in range(depth):
        q, k, vt, u, gates = _in_proj(h, row(norm1_g[l]), w_in, l, cos, sin_signed,
                                      row(q_norm_g[l]), row(k_norm_g[l]), dims)
        o = _attention(q, k, vt, dims, batch)
        pooled = _pool(u, bf(w_pool_mix[l]), row(pool_scale[l]), seq)
        merged = _merge(o, pooled, w_attn_o, w_pool_o, l, gates)
        h = _out_proj(merged, w_out, l, h)
        i = l // 2
        if l % 2 == 0:
            h = _ffn(h, row(norm2_g[l]), ffn_w_gate, ffn_w_up, ffn_w_down, i)
        else:
            last = l == depth - 1
            h = _moe(h, row(norm2_g[l]), router_w[i], router_b[i], moe_w_gate, moe_w_up,
                     moe_w_down, i, fg, last)
    if depth % 2 == 1 or depth == 0:
        h = _final_norm(h, fg)
    return h.reshape(batch, seq, d)
```

```python
import functools

import jax
import jax.numpy as jnp
from jax import lax
from jax.experimental import pallas as pl
from jax.experimental.pallas import tpu as pltpu

_GRID_W = 64
_ROPE_THETA = 10000.0
_POOL_WINDOWS = (2, 4, 8, 16)
_TOP_K = 2
_NORM_EPS = 1e-6
_LOG2_E = 1.4426950408889634

_LANES = 128
_SUBLANES = 8
_V7X_VMEM_LIMIT_BYTES = 56 * 1024 * 1024

_F32 = jnp.float32
_BF16 = jnp.bfloat16
_NT_DIMS = (((1,), (1,)), ((), ()))


def _params(*semantics):
    return pltpu.CompilerParams(dimension_semantics=semantics,
                                vmem_limit_bytes=_V7X_VMEM_LIMIT_BYTES)


def _largest_tile(n, cap, quantum):
    t = min(cap, n)
    t -= t % quantum
    while t > quantum and n % t:
        t -= quantum
    assert t >= quantum and n % t == 0, (n, cap, quantum)
    return t


def _rms_scale(x):
    return lax.rsqrt(jnp.mean(x * x, axis=-1, keepdims=True) + _NORM_EPS)


def _in_proj_kernel(h_ref, g_ref, w_ref, cos_ref, sin_ref, qg_ref, kg_ref,
                    q_ref, k_ref, vt_ref, u_ref, gate_ref, xn_sc, acc_sc,
                    *, nq, nk, nv, nu, ng, hd, tk, qscale):
    j = pl.program_id(1)
    ncol = nq + nk + nv + nu + ng
    tn = w_ref.shape[1]
    heads = tn // hd

    @pl.when(j == 0)
    def _():
        x = h_ref[...]
        xn_sc[...] = (x * _rms_scale(x) * g_ref[...]).astype(_BF16)

    def matmul():
        return jnp.dot(xn_sc[...], w_ref[...], preferred_element_type=_F32)

    def norm_rope(acc, gain, scale):
        cos = cos_ref[...]
        sin = sin_ref[...]
        lane = lax.broadcasted_iota(jnp.int32, (1, hd), 1)
        first_half = (lane % (hd // 2)) < (hd // 4)
        outs = []
        for hh in range(heads):
            xh = acc[:, hh * hd:(hh + 1) * hd]
            xh = xh * _rms_scale(xh) * gain
            partner = jnp.where(first_half,
                                pltpu.roll(xh, hd - hd // 4, 1),
                                pltpu.roll(xh, hd // 4, 1))
            outs.append(((xh * cos + partner * sin) * scale).astype(_BF16))
        return outs

    def finish_q(acc):
        q_ref[...] = jnp.concatenate(norm_rope(acc, qg_ref[...], qscale), axis=1)

    def finish_k(acc):
        for hh, kh in enumerate(norm_rope(acc, kg_ref[...], 1.0)):
            k_ref[hh] = kh

    def finish_v(acc):
        for hh in range(heads):
            for c in range(acc.shape[0] // tk):
                vt_ref[hh, c] = acc[c * tk:(c + 1) * tk, hh * hd:(hh + 1) * hd].T.astype(_BF16)

    def finish_u(acc):
        u_ref[...] = acc

    def finish_gate(acc):
        gate_ref[...] = (0.5 * jnp.tanh(0.5 * acc) + 0.5).astype(_BF16)

    @pl.when(j == 0)
    def _():
        acc_sc[0] = matmul()

    t = j - 1
    lo = 0
    for cnt, finish in ((nq, finish_q), (nk, finish_k), (nv, finish_v), (nu, finish_u),
                        (ng, finish_gate)):
        @pl.when((t >= lo) & (t < lo + cnt) & (j < ncol))
        def _(finish=finish):
            finish(acc_sc[t % 2])
            acc_sc[j % 2] = matmul()
        lo += cnt

    @pl.when(j == ncol)
    def _():
        finish_gate(acc_sc[t % 2])


def _attn_tiles(seq):
    tq = _largest_tile(seq, 256, _LANES)
    tk = _largest_tile(seq // 2, 512, _LANES)
    assert (seq // tk) % 2 == 0
    return tq, tk


def _in_proj(h, g1, w, layer, cos, sin_signed, qg, kg, dims):
    n, d = h.shape
    q_dim, kv_dim, p_dim, hd, seq = dims
    _, tk = _attn_tiles(seq)
    cols = w.shape[2]
    tm = _largest_tile(seq, 1024, _SUBLANES)
    tn = _largest_tile(_gcd_all(q_dim, kv_dim, p_dim, d), 512, hd)
    nq, nk, nv, nu, ng = q_dim // tn, kv_dim // tn, kv_dim // tn, p_dim // tn, 2 * d // tn
    ncol = nq + nk + nv + nu + ng
    assert ncol * tn == cols and tm % tk == 0
    s_tiles = seq // tm
    heads = tn // hd

    def tile(j, lo, cnt):
        return jnp.clip(j - 1 - lo, 0, cnt - 1)

    kern = functools.partial(_in_proj_kernel, nq=nq, nk=nk, nv=nv, nu=nu, ng=ng, hd=hd, tk=tk,
                             qscale=_LOG2_E * float(hd) ** -0.5)
    return pl.pallas_call(
        kern,
        grid=(n // tm, ncol + 1),
        in_specs=[
            pl.BlockSpec((tm, d), lambda i, j: (i, 0)),
            pl.BlockSpec((1, d), lambda i, j: (0, 0)),
            pl.BlockSpec((None, d, tn), lambda i, j: (layer, 0, jnp.minimum(j, ncol - 1))),
            pl.BlockSpec((tm, hd), lambda i, j: (i % s_tiles, 0)),
            pl.BlockSpec((tm, hd), lambda i, j: (i % s_tiles, 0)),
            pl.BlockSpec((1, hd), lambda i, j: (0, 0)),
            pl.BlockSpec((1, hd), lambda i, j: (0, 0)),
        ],
        out_specs=[
            pl.BlockSpec((tm, tn), lambda i, j: (i, tile(j, 0, nq))),
            pl.BlockSpec((heads, tm, hd), lambda i, j: (tile(j, nq, nk), i, 0)),
            pl.BlockSpec((heads, tm // tk, hd, tk), lambda i, j: (tile(j, nq + nk, nv), i, 0, 0)),
            pl.BlockSpec((tm, tn), lambda i, j: (i, tile(j, nq + nk + nv, nu))),
            pl.BlockSpec((tm, tn), lambda i, j: (i, tile(j, nq + nk + nv + nu, ng))),
        ],
        out_shape=[
            jax.ShapeDtypeStruct((n, q_dim), _BF16),
            jax.ShapeDtypeStruct((kv_dim // hd, n, hd), _BF16),
            jax.ShapeDtypeStruct((kv_dim // hd, n // tk, hd, tk), _BF16),
            jax.ShapeDtypeStruct((n, p_dim), _F32),
            jax.ShapeDtypeStruct((n, 2 * d), _BF16),
        ],
        scratch_shapes=[pltpu.VMEM((tm, d), _BF16), pltpu.VMEM((2, tm, tn), _F32)],
        compiler_params=_params("parallel", "arbitrary"),
        name="in_proj",
    )(h, g1, w, cos, sin_signed, qg, kg)


def _gcd_all(*vals):
    import math
    g = 0
    for v in vals:
        g = math.gcd(g, v)
    return g


def _attn_kernel(q_ref, k_ref, vt_ref, o_ref, qs_sc, s_sc, cm_sc, acc_sc,
                 *, groups, hd, tq, tk, seq):
    n_chunks = seq // tk
    cols = groups * tq

    for g in range(groups):
        qs_sc[g * tq:(g + 1) * tq, :] = q_ref[:, g * hd:(g + 1) * hd]
    acc_sc[...] = jnp.zeros(acc_sc.shape, _F32)

    def scores(c, slot):
        k_c = k_ref[pl.ds(pl.multiple_of(c * tk, tk), tk), :]
        s = lax.dot_general(k_c, qs_sc[...], _NT_DIMS, preferred_element_type=_F32)
        s_sc[slot] = s
        cm_sc[slot] = jnp.max(s, axis=0, keepdims=True)

    def softmax_values(c, slot, m_prev, l_prev):
        m_new = jnp.maximum(m_prev, cm_sc[slot])
        alpha = jnp.exp2(m_prev - m_new)
        p = jnp.exp2(s_sc[slot] - m_new)
        acc_sc[...] = acc_sc[...] * alpha + jnp.dot(vt_ref[c], p.astype(_BF16),
                                                    preferred_element_type=_F32)
        return m_new, alpha * l_prev + jnp.sum(p, axis=0, keepdims=True)

    def pair(j, carry):
        m, l = carry
        c = 2 * j
        scores(c + 1, 1)
        m, l = softmax_values(c, 0, m, l)
        scores(c + 2, 0)
        return softmax_values(c + 1, 1, m, l)

    m = jnp.full((1, cols), -jnp.inf, _F32)
    l = jnp.zeros((1, cols), _F32)
    scores(0, 0)
    m, l = lax.fori_loop(0, n_chunks // 2 - 1, pair, (m, l))
    scores(n_chunks - 1, 1)
    m, l = softmax_values(n_chunks - 2, 0, m, l)
    m, l = softmax_values(n_chunks - 1, 1, m, l)

    out_t = acc_sc[...] / l
    for g in range(groups):
        o_ref[:, g * hd:(g + 1) * hd] = out_t[:, g * tq:(g + 1) * tq].T.astype(o_ref.dtype)


def _attention(q, k, vt, dims, batch):
    q_dim, kv_dim, _, hd, seq = dims
    n_kv = kv_dim // hd
    groups = q_dim // kv_dim
    tq, tk = _attn_tiles(seq)
    q3 = q.reshape(batch, seq, q_dim)
    cols = groups * tq
    kern = functools.partial(_attn_kernel, groups=groups, hd=hd, tq=tq, tk=tk, seq=seq)
    o = pl.pallas_call(
        kern,
        grid=(batch, n_kv, seq // tq),
        in_specs=[
            pl.BlockSpec((None, tq, groups * hd), lambda b, kh, i: (b, i, kh)),
            pl.BlockSpec((None, seq, hd), lambda b, kh, i: (kh, b, 0)),
            pl.BlockSpec((None, seq // tk, hd, tk), lambda b, kh, i: (kh, b, 0, 0)),
        ],
        out_specs=pl.BlockSpec((None, tq, groups * hd), lambda b, kh, i: (b, i, kh)),
        out_shape=jax.ShapeDtypeStruct((batch, seq, q_dim), _BF16),
        scratch_shapes=[
            pltpu.VMEM((cols, hd), _BF16),
            pltpu.VMEM((2, tk, cols), _F32),
            pltpu.VMEM((2, 1, cols), _F32),
            pltpu.VMEM((hd, cols), _F32),
        ],
        compiler_params=_params("parallel", "parallel", "parallel"),
        name="attention",
    )(q3, k, vt)
    return o.reshape(batch * seq, q_dim)


_POOL_HALO = 8


def _pool_kernel(prev_ref, u_ref, next_ref, wmix_ref, scale_ref, o_ref, ext_sc, *, tm, seq, grp):
    i = pl.program_id(0)
    halo = _POOL_HALO
    ext_sc[0:halo, :] = prev_ref[...]
    ext_sc[halo:halo + tm, :] = u_ref[...]
    ext_sc[halo + tm:halo + tm + halo, :] = next_ref[...]
    t = (i % (seq // tm)) * tm + lax.broadcasted_iota(jnp.int32, (tm, 1), 0)
    for g, w in enumerate(_POOL_WINDOWS):
        sl = slice(g * grp, (g + 1) * grp)
        wsum = jnp.zeros((tm, grp), _F32)
        for k in range(-(w // 2), w // 2):
            rows = ext_sc[halo + k:halo + k + tm, sl]
            inside = (t + k >= 0) & (t + k <= seq - 1)
            wsum = wsum + jnp.where(inside, rows, 0.0)
        lo = jnp.maximum(t - w // 2, 0)
        hi = jnp.minimum(t + w // 2 - 1, seq - 1)
        cnt = (hi - lo + 1).astype(_F32)
        dev = wsum / cnt - u_ref[:, sl]
        y = jnp.dot(dev.astype(_BF16), wmix_ref[g], preferred_element_type=_F32)
        o_ref[:, sl] = (y * scale_ref[:, sl]).astype(o_ref.dtype)


def _pool(u, wmix, scale, seq):
    n, p = u.shape
    grp = p // len(_POOL_WINDOWS)
    tm = _largest_tile(seq, 512, _POOL_HALO)
    halo = _POOL_HALO
    hb = tm // halo
    kern = functools.partial(_pool_kernel, tm=tm, seq=seq, grp=grp)
    return pl.pallas_call(
        kern,
        grid=(n // tm,),
        in_specs=[
            pl.BlockSpec((halo, p), lambda i: (jnp.maximum(i * hb - 1, 0), 0)),
            pl.BlockSpec((tm, p), lambda i: (i, 0)),
            pl.BlockSpec((halo, p), lambda i: (jnp.minimum((i + 1) * hb, n // halo - 1), 0)),
            pl.BlockSpec(wmix.shape, lambda i: (0, 0, 0)),
            pl.BlockSpec((1, p), lambda i: (0, 0)),
        ],
        out_specs=pl.BlockSpec((tm, p), lambda i: (i, 0)),
        out_shape=jax.ShapeDtypeStruct((n, p), _BF16),
        scratch_shapes=[pltpu.VMEM((tm + 2 * halo, p), _F32)],
        compiler_params=_params("parallel"),
        name="pool",
    )(u, u, u, wmix, scale)


def _merge_kernel(o_ref, p_ref, wa_ref, wp_ref, ga_ref, gp_ref, out_ref):
    a = jnp.dot(o_ref[...], wa_ref[...], preferred_element_type=_F32)
    p = jnp.dot(p_ref[...], wp_ref[...], preferred_element_type=_F32)
    merged = ga_ref[...].astype(_F32) * a + gp_ref[...].astype(_F32) * p
    out_ref[...] = merged.astype(out_ref.dtype)


def _merge(o, pooled, wa, wp, layer, gates):
    n, q_dim = o.shape
    p_dim = pooled.shape[1]
    d = wa.shape[2]
    tm = _largest_tile(n, 1024, _SUBLANES)
    tn = _largest_tile(d, 1024, _LANES)
    nj = d // tn
    return pl.pallas_call(
        _merge_kernel,
        grid=(n // tm, nj),
        in_specs=[
            pl.BlockSpec((tm, q_dim), lambda i, j: (i, 0)),
            pl.BlockSpec((tm, p_dim), lambda i, j: (i, 0)),
            pl.BlockSpec((None, q_dim, tn), lambda i, j: (layer, 0, j)),
            pl.BlockSpec((None, p_dim, tn), lambda i, j: (layer, 0, j)),
            pl.BlockSpec((tm, tn), lambda i, j: (i, j)),
            pl.BlockSpec((tm, tn), lambda i, j: (i, nj + j)),
        ],
        out_specs=pl.BlockSpec((tm, tn), lambda i, j: (i, j)),
        out_shape=jax.ShapeDtypeStruct((n, d), _BF16),
        compiler_params=_params("parallel", "arbitrary"),
        name="merge",
    )(o, pooled, wa, wp, gates, gates)


def _out_proj_kernel(a_ref, w_ref, res_ref, out_ref):
    out_ref[...] = res_ref[...] + jnp.dot(a_ref[...], w_ref[...], preferred_element_type=_F32)


def _out_proj(merged, w, layer, h):
    n, d = h.shape
    tm = _largest_tile(n, 1024, _SUBLANES)
    tn = _largest_tile(d, 1024, _LANES)
    return pl.pallas_call(
        _out_proj_kernel,
        grid=(n // tm, d // tn),
        in_specs=[
            pl.BlockSpec((tm, merged.shape[1]), lambda i, j: (i, 0)),
            pl.BlockSpec((None, merged.shape[1], tn), lambda i, j: (layer, 0, j)),
            pl.BlockSpec((tm, tn), lambda i, j: (i, j)),
        ],
        out_specs=pl.BlockSpec((tm, tn), lambda i, j: (i, j)),
        out_shape=jax.ShapeDtypeStruct((n, d), _F32),
        compiler_params=_params("parallel", "arbitrary"),
        name="out_proj",
    )(merged, w, h)


def _swiglu_step(xn, wg_ref, wu_ref, wd_ref):
    a = jnp.dot(xn, wg_ref[...], preferred_element_type=_F32)
    b = jnp.dot(xn, wu_ref[...], preferred_element_type=_F32)
    act = (a * jax.nn.sigmoid(a) * b).astype(_BF16)
    return jnp.dot(act, wd_ref[...], preferred_element_type=_F32)


def _ffn_kernel(h_ref, g_ref, wg_ref, wu_ref, wd_ref, out_ref, xn_sc):
    @pl.when(pl.program_id(1) == 0)
    def _():
        x = h_ref[...]
        xn_sc[...] = (x * _rms_scale(x) * g_ref[...]).astype(_BF16)
        out_ref[...] = x

    out_ref[...] += _swiglu_step(xn_sc[...], wg_ref, wu_ref, wd_ref)


def _ffn(h, g2, wg, wu, wd, layer):
    n, d = h.shape
    ff = wg.shape[2]
    tm = _largest_tile(n, 512, _SUBLANES)
    tf = _largest_tile(ff, 512, _LANES)
    return pl.pallas_call(
        _ffn_kernel,
        grid=(n // tm, ff // tf),
        in_specs=[
            pl.BlockSpec((tm, d), lambda i, f: (i, 0)),
            pl.BlockSpec((1, d), lambda i, f: (0, 0)),
            pl.BlockSpec((None, d, tf), lambda i, f: (layer, 0, f)),
            pl.BlockSpec((None, d, tf), lambda i, f: (layer, 0, f)),
            pl.BlockSpec((None, tf, d), lambda i, f: (layer, f, 0)),
        ],
        out_specs=pl.BlockSpec((tm, d), lambda i, f: (i, 0)),
        out_shape=jax.ShapeDtypeStruct((n, d), _F32),
        scratch_shapes=[pltpu.VMEM((tm, d), _BF16)],
        compiler_params=_params("parallel", "arbitrary"),
        name="ffn_dense",
    )(h, g2, wg, wu, wd)


_R_E1, _R_E2, _R_RANK1, _R_RANK2, _R_W1, _R_W2 = range(6)


def _router_kernel(h_ref, g_ref, wr_ref, br_ref, xp_ref, info_ref, cnt_ref, carry_sc, *, n_exp):
    @pl.when(pl.program_id(0) == 0)
    def _():
        carry_sc[...] = jnp.zeros(carry_sc.shape, _F32)

    x = h_ref[...]
    xn = x * _rms_scale(x) * g_ref[...]
    tm, d = xn.shape

    xb = xn.astype(_BF16).astype(_F32)
    lo = lax.bitcast_convert_type(xb[:, :d // 2], jnp.int32)
    hi = lax.bitcast_convert_type(xb[:, d // 2:], jnp.int32)
    xp_ref[...] = lax.shift_right_logical(lo, 16) | hi

    logits = jnp.dot(xn, wr_ref[...], preferred_element_type=_F32,
                     precision=lax.Precision.HIGHEST) + br_ref[...]
    lane = lax.broadcasted_iota(jnp.int32, logits.shape, 1)
    logits = jnp.where(lane < n_exp, logits, -jnp.inf)
    v1 = jnp.max(logits, axis=-1, keepdims=True)
    i1 = jnp.min(jnp.where(logits == v1, lane, _LANES), axis=-1, keepdims=True)
    pick1 = lane == i1
    rest = jnp.where(pick1, -jnp.inf, logits)
    v2 = jnp.max(rest, axis=-1, keepdims=True)
    i2 = jnp.min(jnp.where(rest == v2, lane, _LANES), axis=-1, keepdims=True)
    pick2 = lane == i2
    e = jnp.exp(v2 - v1)
    w1 = 1.0 / (1.0 + e)
    w2 = e / (1.0 + e)

    cnt = (pick1 | pick2).astype(_F32)
    row = lax.broadcasted_iota(jnp.int32, (tm, tm), 0)
    col = lax.broadcasted_iota(jnp.int32, (tm, tm), 1)
    below = (row > col).astype(_BF16)
    before = jnp.dot(below, cnt.astype(_BF16), preferred_element_type=_F32) + carry_sc[...]
    rank1 = jnp.sum(jnp.where(pick1, before, 0.0), axis=-1, keepdims=True)
    rank2 = jnp.sum(jnp.where(pick2, before, 0.0), axis=-1, keepdims=True)
    carry_sc[...] += jnp.sum(cnt, axis=0, keepdims=True)

    info = jnp.zeros(logits.shape, _F32)
    for slot, val in ((_R_E1, i1.astype(_F32)), (_R_E2, i2.astype(_F32)), (_R_RANK1, rank1),
                      (_R_RANK2, rank2), (_R_W1, w1), (_R_W2, w2)):
        info = jnp.where(lane == slot, val, info)
    info_ref[...] = info
    cnt_ref[...] = jnp.broadcast_to(carry_sc[...], cnt_ref.shape)


def _router(h, g2, wr_pad, br_pad, n_exp):
    n, d = h.shape
    tm = _largest_tile(n, 512, _SUBLANES)
    kern = functools.partial(_router_kernel, n_exp=n_exp)
    return pl.pallas_call(
        kern,
        grid=(n // tm,),
        in_specs=[
            pl.BlockSpec((tm, d), lambda i: (i, 0)),
            pl.BlockSpec((1, d), lambda i: (0, 0)),
            pl.BlockSpec((d, _LANES), lambda i: (0, 0)),
            pl.BlockSpec((1, _LANES), lambda i: (0, 0)),
        ],
        out_specs=[
            pl.BlockSpec((tm, d // 2), lambda i: (i, 0)),
            pl.BlockSpec((tm, _LANES), lambda i: (i, 0)),
            pl.BlockSpec((_SUBLANES, _LANES), lambda i: (0, 0)),
        ],
        out_shape=[
            jax.ShapeDtypeStruct((n, d // 2), jnp.int32),
            jax.ShapeDtypeStruct((n, _LANES), _F32),
            jax.ShapeDtypeStruct((_SUBLANES, _LANES), _F32),
        ],
        scratch_shapes=[pltpu.VMEM((1, _LANES), _F32)],
        compiler_params=_params("arbitrary"),
        name="router",
    )(h, g2, wr_pad, br_pad)


_DMA_ISSUE_UNROLL = 8


def _row_copy(src_ref, src_row, dst_ref, dst_row, sem):
    return pltpu.make_async_copy(src_ref.at[pl.ds(src_row, 1)], dst_ref.at[pl.ds(dst_row, 1)], sem)


def _dispatch_kernel(pos_ref, xp_ref, zeros_ref, xs_ref, sem, *, tc):
    del zeros_ref

    def start(t, carry):
        for k in range(_TOP_K):
            _row_copy(xp_ref, t, xs_ref, pos_ref[_TOP_K * t + k], sem).start()
        return carry

    lax.fori_loop(0, tc, start, 0, unroll=_DMA_ISSUE_UNROLL)
    for _ in range(_TOP_K):
        pltpu.make_async_copy(xp_ref, xs_ref.at[pl.ds(0, tc)], sem).wait()


def _dispatch(pos, xp, rows):
    n, half = xp.shape
    tc = _largest_tile(n, 512, _SUBLANES)
    kern = functools.partial(_dispatch_kernel, tc=tc)
    return pl.pallas_call(
        kern,
        grid=(n // tc,),
        in_specs=[
            pl.BlockSpec((_TOP_K * tc,), lambda i: (i,), memory_space=pltpu.SMEM),
            pl.BlockSpec((tc, half), lambda i: (i, 0)),
            pl.BlockSpec(memory_space=pl.ANY),
        ],
        out_specs=pl.BlockSpec(memory_space=pl.ANY),
        out_shape=jax.ShapeDtypeStruct((rows, half), jnp.int32),
        scratch_shapes=[pltpu.SemaphoreType.DMA(())],
        input_output_aliases={2: 0},
        compiler_params=_params("arbitrary"),
        name="moe_dispatch",
    )(pos, xp, jnp.zeros((rows, half), jnp.int32))


def _moe_ffn_kernel(te_ref, nact_ref, xs_ref, wg_ref, wu_ref, wd_ref, y_ref, xn_sc):
    del te_ref
    i = pl.program_id(0)
    f = pl.program_id(1)
    active = i < nact_ref[0]

    @pl.when(f == 0)
    def _():
        y_ref[...] = jnp.zeros(y_ref.shape, y_ref.dtype)

    @pl.when(active & (f == 0))
    def _():
        packed = xs_ref[...]
        lo = lax.bitcast_convert_type(lax.shift_left(packed, 16), _F32)
        hi = lax.bitcast_convert_type(packed & jnp.int32(-65536), _F32)
        xn_sc[...] = jnp.concatenate([lo, hi], axis=1).astype(_BF16)

    @pl.when(active)
    def _():
        y_ref[...] += _swiglu_step(xn_sc[...], wg_ref, wu_ref, wd_ref)


def _moe_ffn(tile_expert, n_active, xs, wg, wu, wd, tm):
    rows, half = xs.shape
    d = 2 * half
    ff = wg.shape[2]
    tf = _largest_tile(ff, 512, _LANES)
    nf = ff // tf

    def row_idx(i, nact):
        return jnp.maximum(jnp.minimum(i, nact[0] - 1), 0)

    def f_idx(i, f, nact):
        return jnp.where(i < nact[0], f, nf - 1)

    grid_spec = pltpu.PrefetchScalarGridSpec(
        num_scalar_prefetch=2,
        grid=(rows // tm, nf),
        in_specs=[
            pl.BlockSpec((tm, half), lambda i, f, te, na: (row_idx(i, na), 0)),
            pl.BlockSpec((None, d, tf), lambda i, f, te, na: (te[i], 0, f_idx(i, f, na))),
            pl.BlockSpec((None, d, tf), lambda i, f, te, na: (te[i], 0, f_idx(i, f, na))),
            pl.BlockSpec((None, tf, d), lambda i, f, te, na: (te[i], f_idx(i, f, na), 0)),
        ],
        out_specs=pl.BlockSpec((tm, d), lambda i, f, te, na: (i, 0)),
        scratch_shapes=[pltpu.VMEM((tm, d), _BF16)],
    )
    return pl.pallas_call(
        _moe_ffn_kernel,
        grid_spec=grid_spec,
        out_shape=jax.ShapeDtypeStruct((rows, d), _F32),
        compiler_params=_params("arbitrary", "arbitrary"),
        name="moe_ffn",
    )(tile_expert, n_active, xs, wg, wu, wd)


def _combine_kernel(pos_ref, info_ref, h_ref, y_ref, g_ref, out_ref, buf, sem, *, tc, final_norm):
    def start(t, carry):
        for k in range(_TOP_K):
            pltpu.make_async_copy(y_ref.at[pl.ds(pos_ref[_TOP_K * t + k], 1)],
                                  buf.at[k, pl.ds(t, 1)], sem).start()
        return carry

    lax.fori_loop(0, tc, start, 0, unroll=_DMA_ISSUE_UNROLL)
    for k in range(_TOP_K):
        pltpu.make_async_copy(y_ref.at[pl.ds(0, tc)], buf.at[k], sem).wait()

    info = info_ref[...]
    w1 = info[:, _R_W1:_R_W1 + 1]
    w2 = info[:, _R_W2:_R_W2 + 1]
    out = h_ref[...] + (w1 * buf[0] + w2 * buf[1])
    if final_norm:
        out = out * _rms_scale(out) * g_ref[...]
    out_ref[...] = out


def _combine(pos, info, h, y, final_g, final_norm):
    n, d = h.shape
    tc = _largest_tile(n, 256, _SUBLANES)
    kern = functools.partial(_combine_kernel, tc=tc, final_norm=final_norm)
    return pl.pallas_call(
        kern,
        grid=(n // tc,),
        in_specs=[
            pl.BlockSpec((_TOP_K * tc,), lambda i: (i,), memory_space=pltpu.SMEM),
            pl.BlockSpec((tc, _LANES), lambda i: (i, 0)),
            pl.BlockSpec((tc, d), lambda i: (i, 0)),
            pl.BlockSpec(memory_space=pl.ANY),
            pl.BlockSpec((1, d), lambda i: (0, 0)),
        ],
        out_specs=pl.BlockSpec((tc, d), lambda i: (i, 0)),
        out_shape=jax.ShapeDtypeStruct((n, d), _F32),
        scratch_shapes=[pltpu.VMEM((_TOP_K, tc, d), _F32), pltpu.SemaphoreType.DMA(())],
        compiler_params=_params("arbitrary"),
        name="moe_combine",
    )(pos, info, h, y, final_g)


def _final_norm_kernel(h_ref, g_ref, out_ref):
    x = h_ref[...]
    out_ref[...] = x * _rms_scale(x) * g_ref[...]


def _final_norm(h, g):
    n, d = h.shape
    tm = _largest_tile(n, 512, _SUBLANES)
    return pl.pallas_call(
        _final_norm_kernel,
        grid=(n // tm,),
        in_specs=[pl.BlockSpec((tm, d), lambda i: (i, 0)), pl.BlockSpec((1, d), lambda i: (0, 0))],
        out_specs=pl.BlockSpec((tm, d), lambda i: (i, 0)),
        out_shape=jax.ShapeDtypeStruct((n, d), _F32),
        compiler_params=_params("parallel"),
        name="final_norm",
    )(h, g)


_MOE_ROW_TILE = 512


def _moe(h, g2, wr, br, wg, wu, wd, layer, final_g, final_norm):
    n, d = h.shape
    n_exp = wr.shape[1]
    tm = _largest_tile(n, _MOE_ROW_TILE, _SUBLANES)
    wr_pad = jnp.zeros((d, _LANES), _F32).at[:, :n_exp].set(wr)
    br_pad = jnp.zeros((1, _LANES), _F32).at[0, :n_exp].set(br)
    xp, info, cnt = _router(h, g2, wr_pad, br_pad, n_exp)

    counts = cnt[0, :n_exp].astype(jnp.int32)
    tiles_per = (counts + tm - 1) // tm
    tile_end = jnp.cumsum(tiles_per)
    starts = (tile_end - tiles_per) * tm
    n_active = tile_end[-1:]
    n_tiles = (_TOP_K * n) // tm + n_exp
    tile_ids = jnp.minimum(jnp.arange(n_tiles, dtype=jnp.int32), n_active - 1)
    tile_expert = jnp.sum((tile_ids[:, None] >= tile_end[None, :]).astype(jnp.int32), axis=1)
    tile_expert = tile_expert + layer * n_exp
    experts = info[:, _R_E1:_R_E2 + 1].astype(jnp.int32)
    ranks = info[:, _R_RANK1:_R_RANK2 + 1].astype(jnp.int32)
    onehot = experts[:, :, None] == jnp.arange(n_exp, dtype=jnp.int32)[None, None, :]
    pos = (jnp.sum(jnp.where(onehot, starts[None, None, :], 0), axis=-1) + ranks).reshape(-1)

    xs = _dispatch(pos, xp, n_tiles * tm)
    y = _moe_ffn(tile_expert, n_active, xs, wg, wu, wd, tm)
    return _combine(pos, info, h, y, final_g, final_norm)


def _rope_tables(seq, hd):
    axis_dim = hd // 2
    nfreq = axis_dim // 2
    rows = seq // _GRID_W
    row = jnp.broadcast_to(jnp.arange(rows)[:, None], (rows, _GRID_W)).reshape(seq).astype(_F32)
    col = jnp.broadcast_to(jnp.arange(_GRID_W)[None, :], (rows, _GRID_W)).reshape(seq).astype(_F32)
    inv = _ROPE_THETA ** (-jnp.arange(nfreq, dtype=_F32) * 2.0 / axis_dim)
    ar = row[:, None] * inv
    ac = col[:, None] * inv
    ang = jnp.concatenate([ar, ar, ac, ac], axis=-1)
    lane = jnp.arange(hd)
    sign = jnp.where((lane % axis_dim) < nfreq, -1.0, 1.0).astype(_F32)
    return jnp.cos(ang), jnp.sin(ang) * sign


def kernel(x, norm1_g, w_in, q_norm_g, k_norm_g, w_attn_o, w_pool_mix, pool_scale, w_pool_o,
           w_out, norm2_g, ffn_w_gate, ffn_w_up, ffn_w_down, router_w, router_b,
           moe_w_gate, moe_w_up, moe_w_down, final_g):
    batch, seq, d = x.shape
    depth = w_in.shape[0]
    hd = q_norm_g.shape[1]
    q_dim = w_attn_o.shape[1]
    p_dim = pool_scale.shape[1]
    kv_dim = (w_in.shape[2] - q_dim - p_dim - 2 * d) // 2
    dims = (q_dim, kv_dim, p_dim, hd, seq)
    assert seq % _GRID_W == 0 and q_dim % kv_dim == 0 and kv_dim % hd == 0

    cos, sin_signed = _rope_tables(seq, hd)
    bf = lambda w: w.astype(_BF16)
    row = lambda v: v.reshape(1, -1)
    experts = lambda w: bf(w).reshape((-1,) + w.shape[2:])

    w_in, w_attn_o, w_pool_o, w_out = bf(w_in), bf(w_attn_o), bf(w_pool_o), bf(w_out)
    ffn_w_gate, ffn_w_up, ffn_w_down = bf(ffn_w_gate), bf(ffn_w_up), bf(ffn_w_down)
    moe_w_gate, moe_w_up, moe_w_down = experts(moe_w_gate), experts(moe_w_up), experts(moe_w_down)

    h = x.reshape(batch * seq, d)
    fg = row(final_g)
    for l in range(depth):
        q, k, vt, u, gates = _in_proj(h, row(norm1_g[l]), w_in, l, cos, sin_signed,
                                      row(q_norm_g[l]), row(k_norm_g[l]), dims)
        o = _attention(q, k, vt, dims, batch)
        pooled = _pool(u, bf(w_pool_mix[l]), row(pool_scale[l]), seq)
        merged = _merge(o, pooled, w_attn_o, w_pool_o, l, gates)
        h = _out_proj(merged, w_out, l, h)
        i = l // 2
        if l % 2 == 0:
            h = _ffn(h, row(norm2_g[l]), ffn_w_gate, ffn_w_up, ffn_w_down, i)
        else:
            last = l == depth - 1
            h = _moe(h, row(norm2_g[l]), router_w[i], router_b[i], moe_w_gate, moe_w_up,
                     moe_w_down, i, fg, last)
    if depth % 2 == 1 or depth == 0:
        h = _final_norm(h, fg)
    return h.reshape(batch, seq, d)
```

```python
import functools

import jax
import jax.numpy as jnp
from jax import lax
from jax.experimental import pallas as pl
from jax.experimental.pallas import tpu as pltpu

_GRID_W = 64
_ROPE_THETA = 10000.0
_POOL_WINDOWS = (2, 4, 8, 16)
_TOP_K = 2
_NORM_EPS = 1e-6
_LOG2_E = 1.4426950408889634

_LANES = 128
_SUBLANES = 8
_V7X_VMEM_LIMIT_BYTES = 56 * 1024 * 1024

_F32 = jnp.float32
_BF16 = jnp.bfloat16
_NT_DIMS = (((1,), (1,)), ((), ()))


def _params(*semantics):
    return pltpu.CompilerParams(dimension_semantics=semantics,
                                vmem_limit_bytes=_V7X_VMEM_LIMIT_BYTES)


def _largest_tile(n, cap, quantum):
    t = min(cap, n)
    t -= t % quantum
    while t > quantum and n % t:
        t -= quantum
    assert t >= quantum and n % t == 0, (n, cap, quantum)
    return t


def _rms_scale(x):
    return lax.rsqrt(jnp.mean(x * x, axis=-1, keepdims=True) + _NORM_EPS)


def _in_proj_kernel(h_ref, g_ref, w_ref, cos_ref, sin_ref, qg_ref, kg_ref,
                    q_ref, k_ref, vt_ref, u_ref, gate_ref, xn_sc, acc_sc,
                    *, nq, nk, nv, nu, ng, hd, tk, qscale):
    j = pl.program_id(1)
    ncol = nq + nk + nv + nu + ng
    tn = w_ref.shape[1]
    heads = tn // hd

    @pl.when(j == 0)
    def _():
        x = h_ref[...]
        xn_sc[...] = (x * _rms_scale(x) * g_ref[...]).astype(_BF16)

    def matmul():
        return jnp.dot(xn_sc[...], w_ref[...], preferred_element_type=_F32)

    def norm_rope(acc, gain, scale):
        cos = cos_ref[...]
        sin = sin_ref[...]
        lane = lax.broadcasted_iota(jnp.int32, (1, hd), 1)
        first_half = (lane % (hd // 2)) < (hd // 4)
        outs = []
        for hh in range(heads):
            xh = acc[:, hh * hd:(hh + 1) * hd]
            xh = xh * _rms_scale(xh) * gain
            partner = jnp.where(first_half,
                                pltpu.roll(xh, hd - hd // 4, 1),
                                pltpu.roll(xh, hd // 4, 1))
            outs.append(((xh * cos + partner * sin) * scale).astype(_BF16))
        return outs

    def finish_q(acc):
        q_ref[...] = jnp.concatenate(norm_rope(acc, qg_ref[...], qscale), axis=1)

    def finish_k(acc):
        for hh, kh in enumerate(norm_rope(acc, kg_ref[...], 1.0)):
            k_ref[hh] = kh

    def finish_v(acc):
        for hh in range(heads):
            for c in range(acc.shape[0] // tk):
                vt_ref[hh, c] = acc[c * tk:(c + 1) * tk, hh * hd:(hh + 1) * hd].T.astype(_BF16)

    def finish_u(acc):
        u_ref[...] = acc

    def finish_gate(acc):
        gate_ref[...] = (0.5 * jnp.tanh(0.5 * acc) + 0.5).astype(_BF16)

    @pl.when(j == 0)
    def _():
        acc_sc[0] = matmul()

    t = j - 1
    lo = 0
    for cnt, finish in ((nq, finish_q), (nk, finish_k), (nv, finish_v), (nu, finish_u),
                        (ng, finish_gate)):
        @pl.when((t >= lo) & (t < lo + cnt) & (j < ncol))
        def _(finish=finish):
            finish(acc_sc[t % 2])
            acc_sc[j % 2] = matmul()
        lo += cnt

    @pl.when(j == ncol)
    def _():
        finish_gate(acc_sc[t % 2])


def _attn_tiles(seq):
    tq = _largest_tile(seq, 256, _LANES)
    tk = _largest_tile(seq // 2, 512, _LANES)
    assert (seq // tk) % 2 == 0
    return tq, tk


def _in_proj(h, g1, w, layer, cos, sin_signed, qg, kg, dims):
    n, d = h.shape
    q_dim, kv_dim, p_dim, hd, seq = dims
    _, tk = _attn_tiles(seq)
    cols = w.shape[2]
    tm = _largest_tile(seq, 1024, _SUBLANES)
    tn = _largest_tile(_gcd_all(q_dim, kv_dim, p_dim, d), 512, hd)
    nq, nk, nv, nu, ng = q_dim // tn, kv_dim // tn, kv_dim // tn, p_dim // tn, 2 * d // tn
    ncol = nq + nk + nv + nu + ng
    assert ncol * tn == cols and tm % tk == 0
    s_tiles = seq // tm
    heads = tn // hd

    def tile(j, lo, cnt):
        return jnp.clip(j - 1 - lo, 0, cnt - 1)

    kern = functools.partial(_in_proj_kernel, nq=nq, nk=nk, nv=nv, nu=nu, ng=ng, hd=hd, tk=tk,
                             qscale=_LOG2_E * float(hd) ** -0.5)
    return pl.pallas_call(
        kern,
        grid=(n // tm, ncol + 1),
        in_specs=[
            pl.BlockSpec((tm, d), lambda i, j: (i, 0)),
            pl.BlockSpec((1, d), lambda i, j: (0, 0)),
            pl.BlockSpec((None, d, tn), lambda i, j: (layer, 0, jnp.minimum(j, ncol - 1))),
            pl.BlockSpec((tm, hd), lambda i, j: (i % s_tiles, 0)),
            pl.BlockSpec((tm, hd), lambda i, j: (i % s_tiles, 0)),
            pl.BlockSpec((1, hd), lambda i, j: (0, 0)),
            pl.BlockSpec((1, hd), lambda i, j: (0, 0)),
        ],
        out_specs=[
            pl.BlockSpec((tm, tn), lambda i, j: (i, tile(j, 0, nq))),
            pl.BlockSpec((heads, tm, hd), lambda i, j: (tile(j, nq, nk), i, 0)),
            pl.BlockSpec((heads, tm // tk, hd, tk), lambda i, j: (tile(j, nq + nk, nv), i, 0, 0)),
            pl.BlockSpec((tm, tn), lambda i, j: (i, tile(j, nq + nk + nv, nu))),
            pl.BlockSpec((tm, tn), lambda i, j: (i, tile(j, nq + nk + nv + nu, ng))),
        ],
        out_shape=[
            jax.ShapeDtypeStruct((n, q_dim), _BF16),
            jax.ShapeDtypeStruct((kv_dim // hd, n, hd), _BF16),
            jax.ShapeDtypeStruct((kv_dim // hd, n // tk, hd, tk), _BF16),
            jax.ShapeDtypeStruct((n, p_dim), _F32),
            jax.ShapeDtypeStruct((n, 2 * d), _BF16),
        ],
        scratch_shapes=[pltpu.VMEM((tm, d), _BF16), pltpu.VMEM((2, tm, tn), _F32)],
        compiler_params=_params("parallel", "arbitrary"),
        name="in_proj",
    )(h, g1, w, cos, sin_signed, qg, kg)


def _gcd_all(*vals):
    import math
    g = 0
    for v in vals:
        g = math.gcd(g, v)
    return g


def _attn_kernel(q_ref, k_ref, vt_ref, o_ref, qs_sc, s_sc, cm_sc, acc_sc,
                 *, groups, hd, tq, tk, seq):
    n_chunks = seq // tk
    cols = groups * tq

    for g in range(groups):
        qs_sc[g * tq:(g + 1) * tq, :] = q_ref[:, g * hd:(g + 1) * hd]
    acc_sc[...] = jnp.zeros(acc_sc.shape, _F32)

    def scores(c, slot):
        k_c = k_ref[pl.ds(pl.multiple_of(c * tk, tk), tk), :]
        s = lax.dot_general(k_c, qs_sc[...], _NT_DIMS, preferred_element_type=_F32)
        s_sc[slot] = s
        cm_sc[slot] = jnp.max(s, axis=0, keepdims=True)

    def softmax_values(c, slot, m_prev, l_prev):
        m_new = jnp.maximum(m_prev, cm_sc[slot])
        alpha = jnp.exp2(m_prev - m_new)
        p = jnp.exp2(s_sc[slot] - m_new)
        acc_sc[...] = acc_sc[...] * alpha + jnp.dot(vt_ref[c], p.astype(_BF16),
                                                    preferred_element_type=_F32)
        return m_new, alpha * l_prev + jnp.sum(p, axis=0, keepdims=True)

    def pair(j, carry):
        m, l = carry
        c = 2 * j
        scores(c + 1, 1)
        m, l = softmax_values(c, 0, m, l)
        scores(c + 2, 0)
        return softmax_values(c + 1, 1, m, l)

    m = jnp.full((1, cols), -jnp.inf, _F32)
    l = jnp.zeros((1, cols), _F32)
    scores(0, 0)
    m, l = lax.fori_loop(0, n_chunks // 2 - 1, pair, (m, l))
    scores(n_chunks - 1, 1)
    m, l = softmax_values(n_chunks - 2, 0, m, l)
    m, l = softmax_values(n_chunks - 1, 1, m, l)

    out_t = acc_sc[...] / l
    for g in range(groups):
        o_ref[:, g * hd:(g + 1) * hd] = out_t[:, g * tq:(g + 1) * tq].T.astype(o_ref.dtype)


def _attention(q, k, vt, dims, batch):
    q_dim, kv_dim, _, hd, seq = dims
    n_kv = kv_dim // hd
    groups = q_dim // kv_dim
    tq, tk = _attn_tiles(seq)
    q3 = q.reshape(batch, seq, q_dim)
    cols = groups * tq
    kern = functools.partial(_attn_kernel, groups=groups, hd=hd, tq=tq, tk=tk, seq=seq)
    o = pl.pallas_call(
        kern,
        grid=(batch, n_kv, seq // tq),
        in_specs=[
            pl.BlockSpec((None, tq, groups * hd), lambda b, kh, i: (b, i, kh)),
            pl.BlockSpec((None, seq, hd), lambda b, kh, i: (kh, b, 0)),
            pl.BlockSpec((None, seq // tk, hd, tk), lambda b, kh, i: (kh, b, 0, 0)),
        ],
        out_specs=pl.BlockSpec((None, tq, groups * hd), lambda b, kh, i: (b, i, kh)),
        out_shape=jax.ShapeDtypeStruct((batch, seq, q_dim), _BF16),
        scratch_shapes=[
            pltpu.VMEM((cols, hd), _BF16),
            pltpu.VMEM((2, tk, cols), _F32),
            pltpu.VMEM((2, 1, cols), _F32),
            pltpu.VMEM((hd, cols), _F32),
        ],
        compiler_params=_params("parallel", "parallel", "parallel"),
        name="attention",
    )(q3, k, vt)
    return o.reshape(batch * seq, q_dim)


_POOL_HALO = 8


def _pool_kernel(prev_ref, u_ref, next_ref, wmix_ref, scale_ref, o_ref, ext_sc, *, tm, seq, grp):
    i = pl.program_id(0)
    halo = _POOL_HALO
    ext_sc[0:halo, :] = prev_ref[...]
    ext_sc[halo:halo + tm, :] = u_ref[...]
    ext_sc[halo + tm:halo + tm + halo, :] = next_ref[...]
    t = (i % (seq // tm)) * tm + lax.broadcasted_iota(jnp.int32, (tm, 1), 0)
    for g, w in enumerate(_POOL_WINDOWS):
        sl = slice(g * grp, (g + 1) * grp)
        wsum = jnp.zeros((tm, grp), _F32)
        for k in range(-(w // 2), w // 2):
            rows = ext_sc[halo + k:halo + k + tm, sl]
            inside = (t + k >= 0) & (t + k <= seq - 1)
            wsum = wsum + jnp.where(inside, rows, 0.0)
        lo = jnp.maximum(t - w // 2, 0)
        hi = jnp.minimum(t + w // 2 - 1, seq - 1)
        cnt = (hi - lo + 1).astype(_F32)
        dev = wsum / cnt - u_ref[:, sl]
        y = jnp.dot(dev.astype(_BF16), wmix_ref[g], preferred_element_type=_F32)
        o_ref[:, sl] = (y * scale_ref[:, sl]).astype(o_ref.dtype)


def _pool(u, wmix, scale, seq):
    n, p = u.shape
    grp = p // len(_POOL_WINDOWS)
    tm = _largest_tile(seq, 512, _POOL_HALO)
    halo = _POOL_HALO
    hb = tm // halo
    kern = functools.partial(_pool_kernel, tm=tm, seq=seq, grp=grp)
    return pl.pallas_call(
        kern,
        grid=(n // tm,),
        in_specs=[
            pl.BlockSpec((halo, p), lambda i: (jnp.maximum(i * hb - 1, 0), 0)),
            pl.BlockSpec((tm, p), lambda i: (i, 0)),
            pl.BlockSpec((halo, p), lambda i: (jnp.minimum((i + 1) * hb, n // halo - 1), 0)),
            pl.BlockSpec(wmix.shape, lambda i: (0, 0, 0)),
            pl.BlockSpec((1, p), lambda i: (0, 0)),
        ],
        out_specs=pl.BlockSpec((tm, p), lambda i: (i, 0)),
        out_shape=jax.ShapeDtypeStruct((n, p), _BF16),
        scratch_shapes=[pltpu.VMEM((tm + 2 * halo, p), _F32)],
        compiler_params=_params("parallel"),
        name="pool",
    )(u, u, u, wmix, scale)


def _merge_kernel(o_ref, p_ref, wa_ref, wp_ref, ga_ref, gp_ref, out_ref):
    a = jnp.dot(o_ref[...], wa_ref[...], preferred_element_type=_F32)
    p = jnp.dot(p_ref[...], wp_ref[...], preferred_element_type=_F32)
    merged = ga_ref[...].astype(_F32) * a + gp_ref[...].astype(_F32) * p
    out_ref[...] = merged.astype(out_ref.dtype)


def _merge(o, pooled, wa, wp, layer, gates):
    n, q_dim = o.shape
    p_dim = pooled.shape[1]
    d = wa.shape[2]
    tm = _largest_tile(n, 1024, _SUBLANES)
    tn = _largest_tile(d, 1024, _LANES)
    nj = d // tn
    return pl.pallas_call(
        _merge_kernel,
        grid=(n // tm, nj),
        in_specs=[
            pl.BlockSpec((tm, q_dim), lambda i, j: (i, 0)),
            pl.BlockSpec((tm, p_dim), lambda i, j: (i, 0)),
            pl.BlockSpec((None, q_dim, tn), lambda i, j: (layer, 0, j)),
            pl.BlockSpec((None, p_dim, tn), lambda i, j: (layer, 0, j)),
            pl.BlockSpec((tm, tn), lambda i, j: (i, j)),
            pl.BlockSpec((tm, tn), lambda i, j: (i, nj + j)),
        ],
        out_specs=pl.BlockSpec((tm, tn), lambda i, j: (i, j)),
        out_shape=jax.ShapeDtypeStruct((n, d), _BF16),
        compiler_params=_params("parallel", "arbitrary"),
        name="merge",
    )(o, pooled, wa, wp, gates, gates)


def _out_proj_kernel(a_ref, w_ref, res_ref, out_ref):
    out_ref[...] = res_ref[...] + jnp.dot(a_ref[...], w_ref[...], preferred_element_type=_F32)


def _out_proj(merged, w, layer, h):
    n, d = h.shape
    tm = _largest_tile(n, 1024, _SUBLANES)
    tn = _largest_tile(d, 1024, _LANES)
    return pl.pallas_call(
        _out_proj_kernel,
        grid=(n // tm, d // tn),
        in_specs=[
            pl.BlockSpec((tm, merged.shape[1]), lambda i, j: (i, 0)),
            pl.BlockSpec((None, merged.shape[1], tn), lambda i, j: (layer, 0, j)),
            pl.BlockSpec((tm, tn), lambda i, j: (i, j)),
        ],
        out_specs=pl.BlockSpec((tm, tn), lambda i, j: (i, j)),
        out_shape=jax.ShapeDtypeStruct((n, d), _F32),
        compiler_params=_params("parallel", "arbitrary"),
        name="out_proj",
    )(merged, w, h)


def _swiglu_step(xn, wg_ref, wu_ref, wd_ref):
    a = jnp.dot(xn, wg_ref[...], preferred_element_type=_F32)
    b = jnp.dot(xn, wu_ref[...], preferred_element_type=_F32)
    act = (a * jax.nn.sigmoid(a) * b).astype(_BF16)
    return jnp.dot(act, wd_ref[...], preferred_element_type=_F32)


def _ffn_kernel(h_ref, g_ref, wg_ref, wu_ref, wd_ref, out_ref, xn_sc):
    @pl.when(pl.program_id(1) == 0)
    def _():
        x = h_ref[...]
        xn_sc[...] = (x * _rms_scale(x) * g_ref[...]).astype(_BF16)
        out_ref[...] = x

    out_ref[...] += _swiglu_step(xn_sc[...], wg_ref, wu_ref, wd_ref)


def _ffn(h, g2, wg, wu, wd, layer):
    n, d = h.shape
    ff = wg.shape[2]
    tm = _largest_tile(n, 512, _SUBLANES)
    tf = _largest_tile(ff, 512, _LANES)
    return pl.pallas_call(
        _ffn_kernel,
        grid=(n // tm, ff // tf),
        in_specs=[
            pl.BlockSpec((tm, d), lambda i, f: (i, 0)),
            pl.BlockSpec((1, d), lambda i, f: (0, 0)),
            pl.BlockSpec((None, d, tf), lambda i, f: (layer, 0, f)),
            pl.BlockSpec((None, d, tf), lambda i, f: (layer, 0, f)),
            pl.BlockSpec((None, tf, d), lambda i, f: (layer, f, 0)),
        ],
        out_specs=pl.BlockSpec((tm, d), lambda i, f: (i, 0)),
        out_shape=jax.ShapeDtypeStruct((n, d), _F32),
        scratch_shapes=[pltpu.VMEM((tm, d), _BF16)],
        compiler_params=_params("parallel", "arbitrary"),
        name="ffn_dense",
    )(h, g2, wg, wu, wd)


_R_E1, _R_E2, _R_RANK1, _R_RANK2, _R_W1, _R_W2 = range(6)


def _router_kernel(h_ref, g_ref, wr_ref, br_ref, xp_ref, info_ref, cnt_ref, carry_sc, *, n_exp):
    @pl.when(pl.program_id(0) == 0)
    def _():
        carry_sc[...] = jnp.zeros(carry_sc.shape, _F32)

    x = h_ref[...]
    xn = x * _rms_scale(x) * g_ref[...]
    tm, d = xn.shape

    xb = xn.astype(_BF16).astype(_F32)
    lo = lax.bitcast_convert_type(xb[:, :d // 2], jnp.int32)
    hi = lax.bitcast_convert_type(xb[:, d // 2:], jnp.int32)
    xp_ref[...] = lax.shift_right_logical(lo, 16) | hi

    logits = jnp.dot(xn, wr_ref[...], preferred_element_type=_F32,
                     precision=lax.Precision.HIGHEST) + br_ref[...]
    lane = lax.broadcasted_iota(jnp.int32, logits.shape, 1)
    logits = jnp.where(lane < n_exp, logits, -jnp.inf)
    v1 = jnp.max(logits, axis=-1, keepdims=True)
    i1 = jnp.min(jnp.where(logits == v1, lane, _LANES), axis=-1, keepdims=True)
    pick1 = lane == i1
    rest = jnp.where(pick1, -jnp.inf, logits)
    v2 = jnp.max(rest, axis=-1, keepdims=True)
    i2 = jnp.min(jnp.where(rest == v2, lane, _LANES), axis=-1, keepdims=True)
    pick2 = lane == i2
    e = jnp.exp(v2 - v1)
    w1 = 1.0 / (1.0 + e)
    w2 = e / (1.0 + e)

    cnt = (pick1 | pick2).astype(_F32)
    row = lax.broadcasted_iota(jnp.int32, (tm, tm), 0)
    col = lax.broadcasted_iota(jnp.int32, (tm, tm), 1)
    below = (row > col).astype(_BF16)
    before = jnp.dot(below, cnt.astype(_BF16), preferred_element_type=_F32) + carry_sc[...]
    rank1 = jnp.sum(jnp.where(pick1, before, 0.0), axis=-1, keepdims=True)
    rank2 = jnp.sum(jnp.where(pick2, before, 0.0), axis=-1, keepdims=True)
    carry_sc[...] += jnp.sum(cnt, axis=0, keepdims=True)

    info = jnp.zeros(logits.shape, _F32)
    for slot, val in ((_R_E1, i1.astype(_F32)), (_R_E2, i2.astype(_F32)), (_R_RANK1, rank1),
                      (_R_RANK2, rank2), (_R_W1, w1), (_R_W2, w2)):
        info = jnp.where(lane == slot, val, info)
    info_ref[...] = info
    cnt_ref[...] = jnp.broadcast_to(carry_sc[...], cnt_ref.shape)


def _router(h, g2, wr_pad, br_pad, n_exp):
    n, d = h.shape
    tm = _largest_tile(n, 512, _SUBLANES)
    kern = functools.partial(_router_kernel, n_exp=n_exp)
    return pl.pallas_call(
        kern,
        grid=(n // tm,),
        in_specs=[
            pl.BlockSpec((tm, d), lambda i: (i, 0)),
            pl.BlockSpec((1, d), lambda i: (0, 0)),
            pl.BlockSpec((d, _LANES), lambda i: (0, 0)),
            pl.BlockSpec((1, _LANES), lambda i: (0, 0)),
        ],
        out_specs=[
            pl.BlockSpec((tm, d // 2), lambda i: (i, 0)),
            pl.BlockSpec((tm, _LANES), lambda i: (i, 0)),
            pl.BlockSpec((_SUBLANES, _LANES), lambda i: (0, 0)),
        ],
        out_shape=[
            jax.ShapeDtypeStruct((n, d // 2), jnp.int32),
            jax.ShapeDtypeStruct((n, _LANES), _F32),
            jax.ShapeDtypeStruct((_SUBLANES, _LANES), _F32),
        ],
        scratch_shapes=[pltpu.VMEM((1, _LANES), _F32)],
        compiler_params=_params("arbitrary"),
        name="router",
    )(h, g2, wr_pad, br_pad)


_DMA_ISSUE_UNROLL = 8


def _row_copy(src_ref, src_row, dst_ref, dst_row, sem):
    return pltpu.make_async_copy(src_ref.at[pl.ds(src_row, 1)], dst_ref.at[pl.ds(dst_row, 1)], sem)


def _dispatch_kernel(pos_ref, xp_ref, zeros_ref, xs_ref, sem, *, tc):
    del zeros_ref

    def start(t, carry):
        for k in range(_TOP_K):
            _row_copy(xp_ref, t, xs_ref, pos_ref[_TOP_K * t + k], sem).start()
        return carry

    lax.fori_loop(0, tc, start, 0, unroll=_DMA_ISSUE_UNROLL)
    for _ in range(_TOP_K):
        pltpu.make_async_copy(xp_ref, xs_ref.at[pl.ds(0, tc)], sem).wait()


def _dispatch(pos, xp, rows):
    n, half = xp.shape
    tc = _largest_tile(n, 512, _SUBLANES)
    kern = functools.partial(_dispatch_kernel, tc=tc)
    return pl.pallas_call(
        kern,
        grid=(n // tc,),
        in_specs=[
            pl.BlockSpec((_TOP_K * tc,), lambda i: (i,), memory_space=pltpu.SMEM),
            pl.BlockSpec((tc, half), lambda i: (i, 0)),
            pl.BlockSpec(memory_space=pl.ANY),
        ],
        out_specs=pl.BlockSpec(memory_space=pl.ANY),
        out_shape=jax.ShapeDtypeStruct((rows, half), jnp.int32),
        scratch_shapes=[pltpu.SemaphoreType.DMA(())],
        input_output_aliases={2: 0},
        compiler_params=_params("arbitrary"),
        name="moe_dispatch",
    )(pos, xp, jnp.zeros((rows, half), jnp.int32))


def _expert_changed(te_ref, i):
    return (i == 0) | (te_ref[i] != te_ref[jnp.maximum(i - 1, 0)])


def _moe_up_kernel(te_ref, nact_ref, xs_ref, wg_ref, wu_ref, act_ref, wg_sc, wu_sc):
    i = pl.program_id(1)
    active = i < nact_ref[0]

    @pl.when(active & _expert_changed(te_ref, i))
    def _():
        wg_sc[...] = wg_ref[...].astype(_BF16)
        wu_sc[...] = wu_ref[...].astype(_BF16)

    @pl.when(active)
    def _():
        packed = xs_ref[...]
        lo = lax.bitcast_convert_type(lax.shift_left(packed, 16), _F32)
        hi = lax.bitcast_convert_type(packed & jnp.int32(-65536), _F32)
        x = jnp.concatenate([lo, hi], axis=1).astype(_BF16)
        a = jnp.dot(x, wg_sc[...], preferred_element_type=_F32)
        b = jnp.dot(x, wu_sc[...], preferred_element_type=_F32)
        act_ref[...] = (a * jax.nn.sigmoid(a) * b).astype(act_ref.dtype)

    @pl.when(jnp.logical_not(active))
    def _():
        act_ref[...] = jnp.zeros(act_ref.shape, act_ref.dtype)


def _moe_down_kernel(te_ref, nact_ref, act_ref, wd_ref, y_ref, wd_sc):
    i = pl.program_id(1)
    active = i < nact_ref[0]

    @pl.when(active & _expert_changed(te_ref, i))
    def _():
        wd_sc[...] = wd_ref[...].astype(_BF16)

    @pl.when(active)
    def _():
        y_ref[...] = jnp.dot(act_ref[...], wd_sc[...], preferred_element_type=_F32)

    @pl.when(jnp.logical_not(active))
    def _():
        y_ref[...] = jnp.zeros(y_ref.shape, y_ref.dtype)


def _moe_ffn(tile_expert, n_active, xs, wg, wu, wd, tm):
    rows, half = xs.shape
    d = 2 * half
    ff = wg.shape[2]
    tf = _largest_tile(ff, 512, _LANES)
    tn = _largest_tile(d, 512, _LANES)
    n_tiles = rows // tm

    def row_idx(i, nact):
        return jnp.maximum(jnp.minimum(i, nact[0] - 1), 0)

    up_spec = pltpu.PrefetchScalarGridSpec(
        num_scalar_prefetch=2,
        grid=(ff // tf, n_tiles),
        in_specs=[
            pl.BlockSpec((tm, half), lambda f, i, te, na: (row_idx(i, na), 0)),
            pl.BlockSpec((None, d, tf), lambda f, i, te, na: (te[i], 0, f)),
            pl.BlockSpec((None, d, tf), lambda f, i, te, na: (te[i], 0, f)),
        ],
        out_specs=pl.BlockSpec((tm, tf), lambda f, i, te, na: (i, f)),
        scratch_shapes=[pltpu.VMEM((d, tf), _BF16), pltpu.VMEM((d, tf), _BF16)],
    )
    act = pl.pallas_call(
        _moe_up_kernel,
        grid_spec=up_spec,
        out_shape=jax.ShapeDtypeStruct((rows, ff), _BF16),
        compiler_params=_params("arbitrary", "arbitrary"),
        name="moe_up",
    )(tile_expert, n_active, xs, wg, wu)

    down_spec = pltpu.PrefetchScalarGridSpec(
        num_scalar_prefetch=2,
        grid=(d // tn, n_tiles),
        in_specs=[
            pl.BlockSpec((tm, ff), lambda n, i, te, na: (row_idx(i, na), 0)),
            pl.BlockSpec((None, ff, tn), lambda n, i, te, na: (te[i], 0, n)),
        ],
        out_specs=pl.BlockSpec((tm, tn), lambda n, i, te, na: (i, n)),
        scratch_shapes=[pltpu.VMEM((ff, tn), _BF16)],
    )
    return pl.pallas_call(
        _moe_down_kernel,
        grid_spec=down_spec,
        out_shape=jax.ShapeDtypeStruct((rows, d), _F32),
        compiler_params=_params("arbitrary", "arbitrary"),
        name="moe_down",
    )(tile_expert, n_active, act, wd)


def _combine_kernel(pos_ref, info_ref, h_ref, y_ref, g_ref, out_ref, buf, sem, *, tc, final_norm):
    def start(t, carry):
        for k in range(_TOP_K):
            pltpu.make_async_copy(y_ref.at[pl.ds(pos_ref[_TOP_K * t + k], 1)],
                                  buf.at[k, pl.ds(t, 1)], sem).start()
        return carry

    lax.fori_loop(0, tc, start, 0, unroll=_DMA_ISSUE_UNROLL)
    for k in range(_TOP_K):
        pltpu.make_async_copy(y_ref.at[pl.ds(0, tc)], buf.at[k], sem).wait()

    info = info_ref[...]
    w1 = info[:, _R_W1:_R_W1 + 1]
    w2 = info[:, _R_W2:_R_W2 + 1]
    out = h_ref[...] + (w1 * buf[0] + w2 * buf[1])
    if final_norm:
        out = out * _rms_scale(out) * g_ref[...]
    out_ref[...] = out


def _combine(pos, info, h, y, final_g, final_norm):
    n, d = h.shape
    tc = _largest_tile(n, 256, _SUBLANES)
    kern = functools.partial(_combine_kernel, tc=tc, final_norm=final_norm)
    return pl.pallas_call(
        kern,
        grid=(n // tc,),
        in_specs=[
            pl.BlockSpec((_TOP_K * tc,), lambda i: (i,), memory_space=pltpu.SMEM),
            pl.BlockSpec((tc, _LANES), lambda i: (i, 0)),
            pl.BlockSpec((tc, d), lambda i: (i, 0)),
            pl.BlockSpec(memory_space=pl.ANY),
            pl.BlockSpec((1, d), lambda i: (0, 0)),
        ],
        out_specs=pl.BlockSpec((tc, d), lambda i: (i, 0)),
        out_shape=jax.ShapeDtypeStruct((n, d), _F32),
        scratch_shapes=[pltpu.VMEM((_TOP_K, tc, d), _F32), pltpu.SemaphoreType.DMA(())],
        compiler_params=_params("arbitrary"),
        name="moe_combine",
    )(pos, info, h, y, final_g)


def _final_norm_kernel(h_ref, g_ref, out_ref):
    x = h_ref[...]
    out_ref[...] = x * _rms_scale(x) * g_ref[...]


def _final_norm(h, g):
    n, d = h.shape
    tm = _largest_tile(n, 512, _SUBLANES)
    return pl.pallas_call(
        _final_norm_kernel,
        grid=(n // tm,),
        in_specs=[pl.BlockSpec((tm, d), lambda i: (i, 0)), pl.BlockSpec((1, d), lambda i: (0, 0))],
        out_specs=pl.BlockSpec((tm, d), lambda i: (i, 0)),
        out_shape=jax.ShapeDtypeStruct((n, d), _F32),
        compiler_params=_params("parallel"),
        name="final_norm",
    )(h, g)


_MOE_ROW_TILE = 512


def _moe(h, g2, wr, br, wg, wu, wd, layer, final_g, final_norm):
    n, d = h.shape
    n_exp = wr.shape[1]
    tm = _largest_tile(n, _MOE_ROW_TILE, _SUBLANES)
    wr_pad = jnp.zeros((d, _LANES), _F32).at[:, :n_exp].set(wr)
    br_pad = jnp.zeros((1, _LANES), _F32).at[0, :n_exp].set(br)
    xp, info, cnt = _router(h, g2, wr_pad, br_pad, n_exp)

    counts = cnt[0, :n_exp].astype(jnp.int32)
    tiles_per = (counts + tm - 1) // tm
    tile_end = jnp.cumsum(tiles_per)
    starts = (tile_end - tiles_per) * tm
    n_active = tile_end[-1:]
    n_tiles = (_TOP_K * n) // tm + n_exp
    tile_ids = jnp.minimum(jnp.arange(n_tiles, dtype=jnp.int32), n_active - 1)
    tile_expert = jnp.sum((tile_ids[:, None] >= tile_end[None, :]).astype(jnp.int32), axis=1)
    tile_expert = tile_expert + layer * n_exp
    experts = info[:, _R_E1:_R_E2 + 1].astype(jnp.int32)
    ranks = info[:, _R_RANK1:_R_RANK2 + 1].astype(jnp.int32)
    onehot = experts[:, :, None] == jnp.arange(n_exp, dtype=jnp.int32)[None, None, :]
    pos = (jnp.sum(jnp.where(onehot, starts[None, None, :], 0), axis=-1) + ranks).reshape(-1)

    xs = _dispatch(pos, xp, n_tiles * tm)
    y = _moe_ffn(tile_expert, n_active, xs, wg, wu, wd, tm)
    return _combine(pos, info, h, y, final_g, final_norm)


def _rope_tables(seq, hd):
    axis_dim = hd // 2
    nfreq = axis_dim // 2
    rows = seq // _GRID_W
    row = jnp.broadcast_to(jnp.arange(rows)[:, None], (rows, _GRID_W)).reshape(seq).astype(_F32)
    col = jnp.broadcast_to(jnp.arange(_GRID_W)[None, :], (rows, _GRID_W)).reshape(seq).astype(_F32)
    inv = _ROPE_THETA ** (-jnp.arange(nfreq, dtype=_F32) * 2.0 / axis_dim)
    ar = row[:, None] * inv
    ac = col[:, None] * inv
    ang = jnp.concatenate([ar, ar, ac, ac], axis=-1)
    lane = jnp.arange(hd)
    sign = jnp.where((lane % axis_dim) < nfreq, -1.0, 1.0).astype(_F32)
    return jnp.cos(ang), jnp.sin(ang) * sign


def kernel(x, norm1_g, w_in, q_norm_g, k_norm_g, w_attn_o, w_pool_mix, pool_scale, w_pool_o,
           w_out, norm2_g, ffn_w_gate, ffn_w_up, ffn_w_down, router_w, router_b,
           moe_w_gate, moe_w_up, moe_w_down, final_g):
    batch, seq, d = x.shape
    depth = w_in.shape[0]
    hd = q_norm_g.shape[1]
    q_dim = w_attn_o.shape[1]
    p_dim = pool_scale.shape[1]
    kv_dim = (w_in.shape[2] - q_dim - p_dim - 2 * d) // 2
    dims = (q_dim, kv_dim, p_dim, hd, seq)
    assert seq % _GRID_W == 0 and q_dim % kv_dim == 0 and kv_dim % hd == 0

    cos, sin_signed = _rope_tables(seq, hd)
    bf = lambda w: w.astype(_BF16)
    row = lambda v: v.reshape(1, -1)
    experts = lambda w: w.reshape((-1,) + w.shape[2:])

    w_in, w_attn_o, w_pool_o, w_out = bf(w_in), bf(w_attn_o), bf(w_pool_o), bf(w_out)
    ffn_w_gate, ffn_w_up, ffn_w_down = bf(ffn_w_gate), bf(ffn_w_up), bf(ffn_w_down)
    moe_w_gate, moe_w_up, moe_w_down = experts(moe_w_gate), experts(moe_w_up), experts(moe_w_down)

    h = x.reshape(batch * seq, d)
    fg = row(final_g)
    for l in range(depth):
        q, k, vt, u, gates = _in_proj(h, row(norm1_g[l]), w_in, l, cos, sin_signed,
                                      row(q_norm_g[l]), row(k_norm_g[l]), dims)
        o = _attention(q, k, vt, dims, batch)
        pooled = _pool(u, bf(w_pool_mix[l]), row(pool_scale[l]), seq)
        merged = _merge(o, pooled, w_attn_o, w_pool_o, l, gates)
        h = _out_proj(merged, w_out, l, h)
        i = l // 2
        if l % 2 == 0:
            h = _ffn(h, row(norm2_g[l]), ffn_w_gate, ffn_w_up, ffn_w_down, i)
        else:
            last = l == depth - 1
            h = _moe(h, row(norm2_g[l]), router_w[i], router_b[i], moe_w_gate, moe_w_up,
                     moe_w_down, i, fg, last)
    if depth % 2 == 1 or depth == 0:
        h = _final_norm(h, fg)
    return h.reshape(batch, seq, d)
```

```python
import functools

import jax
import jax.numpy as jnp
from jax import lax
from jax.experimental import pallas as pl
from jax.experimental.pallas import tpu as pltpu

_GRID_W = 64
_ROPE_THETA = 10000.0
_POOL_WINDOWS = (2, 4, 8, 16)
_TOP_K = 2
_NORM_EPS = 1e-6
_LOG2_E = 1.4426950408889634

_LANES = 128
_SUBLANES = 8
_SUM_ROWS = 16
_V7X_VMEM_LIMIT_BYTES = 56 * 1024 * 1024

_F32 = jnp.float32
_BF16 = jnp.bfloat16
_NT_DIMS = (((1,), (1,)), ((), ()))


def _params(*semantics):
    return pltpu.CompilerParams(dimension_semantics=semantics,
                                vmem_limit_bytes=_V7X_VMEM_LIMIT_BYTES)


def _largest_tile(n, cap, quantum):
    t = min(cap, n)
    t -= t % quantum
    while t > quantum and n % t:
        t -= quantum
    assert t >= quantum and n % t == 0, (n, cap, quantum)
    return t


def _rms_scale(x):
    return lax.rsqrt(jnp.mean(x * x, axis=-1, keepdims=True) + _NORM_EPS)


def _in_proj_kernel(h_ref, g_ref, w_ref, cos_ref, sin_ref, qg_ref, kg_ref,
                    q_ref, k_ref, vt_ref, u_ref, gate_ref, xn_sc, acc_sc,
                    *, nq, nk, nv, nu, ng, hd, tk, qscale):
    j = pl.program_id(1)
    ncol = nq + nk + nv + nu + ng
    tn = w_ref.shape[1]
    heads = tn // hd

    @pl.when(j == 0)
    def _():
        x = h_ref[...]
        xn_sc[...] = (x * _rms_scale(x) * g_ref[...]).astype(_BF16)

    def matmul():
        return jnp.dot(xn_sc[...], w_ref[...], preferred_element_type=_F32)

    def norm_rope(acc, gain, scale):
        cos = cos_ref[...]
        sin = sin_ref[...]
        lane = lax.broadcasted_iota(jnp.int32, (1, hd), 1)
        first_half = (lane % (hd // 2)) < (hd // 4)
        outs = []
        for hh in range(heads):
            xh = acc[:, hh * hd:(hh + 1) * hd]
            xh = xh * _rms_scale(xh) * gain
            partner = jnp.where(first_half,
                                pltpu.roll(xh, hd - hd // 4, 1),
                                pltpu.roll(xh, hd // 4, 1))
            outs.append(((xh * cos + partner * sin) * scale).astype(_BF16))
        return outs

    def finish_q(acc):
        q_ref[...] = jnp.concatenate(norm_rope(acc, qg_ref[...], qscale), axis=1)

    def finish_k(acc):
        for hh, kh in enumerate(norm_rope(acc, kg_ref[...], 1.0)):
            k_ref[hh] = kh

    def finish_v(acc):
        sum_rows = (lax.broadcasted_iota(jnp.int32, (_SUM_ROWS, tk), 0) == 0).astype(_BF16)
        for hh in range(heads):
            for c in range(acc.shape[0] // tk):
                vt = acc[c * tk:(c + 1) * tk, hh * hd:(hh + 1) * hd].T.astype(_BF16)
                vt_ref[hh, c, 0:hd, :] = vt
                vt_ref[hh, c, hd:hd + _SUM_ROWS, :] = sum_rows

    def finish_u(acc):
        u_ref[...] = acc

    def finish_gate(acc):
        gate_ref[...] = (0.5 * jnp.tanh(0.5 * acc) + 0.5).astype(_BF16)

    @pl.when(j == 0)
    def _():
        acc_sc[0] = matmul()

    t = j - 1
    lo = 0
    for cnt, finish in ((nq, finish_q), (nk, finish_k), (nv, finish_v), (nu, finish_u),
                        (ng, finish_gate)):
        @pl.when((t >= lo) & (t < lo + cnt) & (j < ncol))
        def _(finish=finish):
            finish(acc_sc[t % 2])
            acc_sc[j % 2] = matmul()
        lo += cnt

    @pl.when(j == ncol)
    def _():
        finish_gate(acc_sc[t % 2])


def _attn_tiles(seq):
    tq = _largest_tile(seq, 256, _LANES)
    tk = _largest_tile(seq // 2, 512, _LANES)
    assert (seq // tk) % 2 == 0
    return tq, tk


def _in_proj(h, g1, w, layer, cos, sin_signed, qg, kg, dims):
    n, d = h.shape
    q_dim, kv_dim, p_dim, hd, seq = dims
    _, tk = _attn_tiles(seq)
    cols = w.shape[2]
    tm = _largest_tile(seq, 1024, _SUBLANES)
    tn = _largest_tile(_gcd_all(q_dim, kv_dim, p_dim, d), 512, hd)
    nq, nk, nv, nu, ng = q_dim // tn, kv_dim // tn, kv_dim // tn, p_dim // tn, 2 * d // tn
    ncol = nq + nk + nv + nu + ng
    assert ncol * tn == cols and tm % tk == 0
    s_tiles = seq // tm
    heads = tn // hd

    def tile(j, lo, cnt):
        return jnp.clip(j - 1 - lo, 0, cnt - 1)

    kern = functools.partial(_in_proj_kernel, nq=nq, nk=nk, nv=nv, nu=nu, ng=ng, hd=hd, tk=tk,
                             qscale=_LOG2_E * float(hd) ** -0.5)
    return pl.pallas_call(
        kern,
        grid=(n // tm, ncol + 1),
        in_specs=[
            pl.BlockSpec((tm, d), lambda i, j: (i, 0)),
            pl.BlockSpec((1, d), lambda i, j: (0, 0)),
            pl.BlockSpec((None, d, tn), lambda i, j: (layer, 0, jnp.minimum(j, ncol - 1))),
            pl.BlockSpec((tm, hd), lambda i, j: (i % s_tiles, 0)),
            pl.BlockSpec((tm, hd), lambda i, j: (i % s_tiles, 0)),
            pl.BlockSpec((1, hd), lambda i, j: (0, 0)),
            pl.BlockSpec((1, hd), lambda i, j: (0, 0)),
        ],
        out_specs=[
            pl.BlockSpec((tm, tn), lambda i, j: (i, tile(j, 0, nq))),
            pl.BlockSpec((heads, tm, hd), lambda i, j: (tile(j, nq, nk), i, 0)),
            pl.BlockSpec((heads, tm // tk, hd + _SUM_ROWS, tk),
                         lambda i, j: (tile(j, nq + nk, nv), i, 0, 0)),
            pl.BlockSpec((tm, tn), lambda i, j: (i, tile(j, nq + nk + nv, nu))),
            pl.BlockSpec((tm, tn), lambda i, j: (i, tile(j, nq + nk + nv + nu, ng))),
        ],
        out_shape=[
            jax.ShapeDtypeStruct((n, q_dim), _BF16),
            jax.ShapeDtypeStruct((kv_dim // hd, n, hd), _BF16),
            jax.ShapeDtypeStruct((kv_dim // hd, n // tk, hd + _SUM_ROWS, tk), _BF16),
            jax.ShapeDtypeStruct((n, p_dim), _F32),
            jax.ShapeDtypeStruct((n, 2 * d), _BF16),
        ],
        scratch_shapes=[pltpu.VMEM((tm, d), _BF16), pltpu.VMEM((2, tm, tn), _F32)],
        compiler_params=_params("parallel", "arbitrary"),
        name="in_proj",
    )(h, g1, w, cos, sin_signed, qg, kg)


def _gcd_all(*vals):
    import math
    g = 0
    for v in vals:
        g = math.gcd(g, v)
    return g


def _attn_kernel(q_ref, k_ref, vt_ref, o_ref, qs_sc, s_sc, cm_sc, acc_sc,
                 *, groups, hd, tq, tk, seq):
    n_chunks = seq // tk
    cols = groups * tq

    for g in range(groups):
        qs_sc[g * tq:(g + 1) * tq, :] = q_ref[:, g * hd:(g + 1) * hd]
    acc_sc[...] = jnp.zeros(acc_sc.shape, _F32)

    def scores(c, slot):
        k_c = k_ref[pl.ds(pl.multiple_of(c * tk, tk), tk), :]
        s = lax.dot_general(k_c, qs_sc[...], _NT_DIMS, preferred_element_type=_F32)
        s_sc[slot] = s
        cm_sc[slot] = jnp.max(s, axis=0, keepdims=True)

    def softmax_values(c, slot, m_prev):
        m_new = jnp.maximum(m_prev, cm_sc[slot])
        alpha = jnp.exp2(m_prev - m_new)
        p = jnp.exp2(s_sc[slot] - m_new)
        acc_sc[...] = acc_sc[...] * alpha + jnp.dot(vt_ref[c], p.astype(_BF16),
                                                    preferred_element_type=_F32)
        return m_new

    def pair(j, m):
        c = 2 * j
        scores(c + 1, 1)
        m = softmax_values(c, 0, m)
        scores(c + 2, 0)
        return softmax_values(c + 1, 1, m)

    m = jnp.full((1, cols), -jnp.inf, _F32)
    scores(0, 0)
    m = lax.fori_loop(0, n_chunks // 2 - 1, pair, m)
    scores(n_chunks - 1, 1)
    m = softmax_values(n_chunks - 2, 0, m)
    m = softmax_values(n_chunks - 1, 1, m)

    out_t = acc_sc[0:hd, :] / acc_sc[hd:hd + 1, :]
    for g in range(groups):
        o_ref[:, g * hd:(g + 1) * hd] = out_t[:, g * tq:(g + 1) * tq].T.astype(o_ref.dtype)


def _attention(q, k, vt, dims, batch):
    q_dim, kv_dim, _, hd, seq = dims
    n_kv = kv_dim // hd
    groups = q_dim // kv_dim
    tq, tk = _attn_tiles(seq)
    q3 = q.reshape(batch, seq, q_dim)
    cols = groups * tq
    kern = functools.partial(_attn_kernel, groups=groups, hd=hd, tq=tq, tk=tk, seq=seq)
    o = pl.pallas_call(
        kern,
        grid=(batch, n_kv, seq // tq),
        in_specs=[
            pl.BlockSpec((None, tq, groups * hd), lambda b, kh, i: (b, i, kh)),
            pl.BlockSpec((None, seq, hd), lambda b, kh, i: (kh, b, 0)),
            pl.BlockSpec((None, seq // tk, hd + _SUM_ROWS, tk), lambda b, kh, i: (kh, b, 0, 0)),
        ],
        out_specs=pl.BlockSpec((None, tq, groups * hd), lambda b, kh, i: (b, i, kh)),
        out_shape=jax.ShapeDtypeStruct((batch, seq, q_dim), _BF16),
        scratch_shapes=[
            pltpu.VMEM((cols, hd), _BF16),
            pltpu.VMEM((2, tk, cols), _F32),
            pltpu.VMEM((2, 1, cols), _F32),
            pltpu.VMEM((hd + _SUM_ROWS, cols), _F32),
        ],
        compiler_params=_params("parallel", "parallel", "parallel"),
        name="attention",
    )(q3, k, vt)
    return o.reshape(batch * seq, q_dim)


_POOL_HALO = 8


def _pool_kernel(prev_ref, u_ref, next_ref, wmix_ref, scale_ref, o_ref, ext_sc, *, tm, seq, grp):
    i = pl.program_id(0)
    halo = _POOL_HALO
    ext_sc[0:halo, :] = prev_ref[...]
    ext_sc[halo:halo + tm, :] = u_ref[...]
    ext_sc[halo + tm:halo + tm + halo, :] = next_ref[...]
    t = (i % (seq // tm)) * tm + lax.broadcasted_iota(jnp.int32, (tm, 1), 0)
    for g, w in enumerate(_POOL_WINDOWS):
        sl = slice(g * grp, (g + 1) * grp)
        wsum = jnp.zeros((tm, grp), _F32)
        for k in range(-(w // 2), w // 2):
            rows = ext_sc[halo + k:halo + k + tm, sl]
            inside = (t + k >= 0) & (t + k <= seq - 1)
            wsum = wsum + jnp.where(inside, rows, 0.0)
        lo = jnp.maximum(t - w // 2, 0)
        hi = jnp.minimum(t + w // 2 - 1, seq - 1)
        cnt = (hi - lo + 1).astype(_F32)
        dev = wsum / cnt - u_ref[:, sl]
        y = jnp.dot(dev.astype(_BF16), wmix_ref[g], preferred_element_type=_F32)
        o_ref[:, sl] = (y * scale_ref[:, sl]).astype(o_ref.dtype)


def _pool(u, wmix, scale, seq):
    n, p = u.shape
    grp = p // len(_POOL_WINDOWS)
    tm = _largest_tile(seq, 512, _POOL_HALO)
    halo = _POOL_HALO
    hb = tm // halo
    kern = functools.partial(_pool_kernel, tm=tm, seq=seq, grp=grp)
    return pl.pallas_call(
        kern,
        grid=(n // tm,),
        in_specs=[
            pl.BlockSpec((halo, p), lambda i: (jnp.maximum(i * hb - 1, 0), 0)),
            pl.BlockSpec((tm, p), lambda i: (i, 0)),
            pl.BlockSpec((halo, p), lambda i: (jnp.minimum((i + 1) * hb, n // halo - 1), 0)),
            pl.BlockSpec(wmix.shape, lambda i: (0, 0, 0)),
            pl.BlockSpec((1, p), lambda i: (0, 0)),
        ],
        out_specs=pl.BlockSpec((tm, p), lambda i: (i, 0)),
        out_shape=jax.ShapeDtypeStruct((n, p), _BF16),
        scratch_shapes=[pltpu.VMEM((tm + 2 * halo, p), _F32)],
        compiler_params=_params("parallel"),
        name="pool",
    )(u, u, u, wmix, scale)


def _merge_kernel(o_ref, p_ref, wa_ref, wp_ref, ga_ref, gp_ref, out_ref):
    a = jnp.dot(o_ref[...], wa_ref[...], preferred_element_type=_F32)
    p = jnp.dot(p_ref[...], wp_ref[...], preferred_element_type=_F32)
    merged = ga_ref[...].astype(_F32) * a + gp_ref[...].astype(_F32) * p
    out_ref[...] = merged.astype(out_ref.dtype)


def _merge(o, pooled, wa, wp, layer, gates):
    n, q_dim = o.shape
    p_dim = pooled.shape[1]
    d = wa.shape[2]
    tm = _largest_tile(n, 1024, _SUBLANES)
    tn = _largest_tile(d, 1024, _LANES)
    nj = d // tn
    return pl.pallas_call(
        _merge_kernel,
        grid=(n // tm, nj),
        in_specs=[
            pl.BlockSpec((tm, q_dim), lambda i, j: (i, 0)),
            pl.BlockSpec((tm, p_dim), lambda i, j: (i, 0)),
            pl.BlockSpec((None, q_dim, tn), lambda i, j: (layer, 0, j)),
            pl.BlockSpec((None, p_dim, tn), lambda i, j: (layer, 0, j)),
            pl.BlockSpec((tm, tn), lambda i, j: (i, j)),
            pl.BlockSpec((tm, tn), lambda i, j: (i, nj + j)),
        ],
        out_specs=pl.BlockSpec((tm, tn), lambda i, j: (i, j)),
        out_shape=jax.ShapeDtypeStruct((n, d), _BF16),
        compiler_params=_params("parallel", "arbitrary"),
        name="merge",
    )(o, pooled, wa, wp, gates, gates)


def _out_proj_kernel(a_ref, w_ref, res_ref, out_ref):
    out_ref[...] = res_ref[...] + jnp.dot(a_ref[...], w_ref[...], preferred_element_type=_F32)


def _out_proj(merged, w, layer, h):
    n, d = h.shape
    tm = _largest_tile(n, 1024, _SUBLANES)
    tn = _largest_tile(d, 1024, _LANES)
    return pl.pallas_call(
        _out_proj_kernel,
        grid=(n // tm, d // tn),
        in_specs=[
            pl.BlockSpec((tm, merged.shape[1]), lambda i, j: (i, 0)),
            pl.BlockSpec((None, merged.shape[1], tn), lambda i, j: (layer, 0, j)),
            pl.BlockSpec((tm, tn), lambda i, j: (i, j)),
        ],
        out_specs=pl.BlockSpec((tm, tn), lambda i, j: (i, j)),
        out_shape=jax.ShapeDtypeStruct((n, d), _F32),
        compiler_params=_params("parallel", "arbitrary"),
        name="out_proj",
    )(merged, w, h)


def _swiglu_step(xn, wg_ref, wu_ref, wd_ref):
    a = jnp.dot(xn, wg_ref[...], preferred_element_type=_F32)
    b = jnp.dot(xn, wu_ref[...], preferred_element_type=_F32)
    act = (a * jax.nn.sigmoid(a) * b).astype(_BF16)
    return jnp.dot(act, wd_ref[...], preferred_element_type=_F32)


def _ffn_kernel(h_ref, g_ref, wg_ref, wu_ref, wd_ref, out_ref, xn_sc):
    @pl.when(pl.program_id(1) == 0)
    def _():
        x = h_ref[...]
        xn_sc[...] = (x * _rms_scale(x) * g_ref[...]).astype(_BF16)
        out_ref[...] = x

    out_ref[...] += _swiglu_step(xn_sc[...], wg_ref, wu_ref, wd_ref)


def _ffn(h, g2, wg, wu, wd, layer):
    n, d = h.shape
    ff = wg.shape[2]
    tm = _largest_tile(n, 512, _SUBLANES)
    tf = _largest_tile(ff, 512, _LANES)
    return pl.pallas_call(
        _ffn_kernel,
        grid=(n // tm, ff // tf),
        in_specs=[
            pl.BlockSpec((tm, d), lambda i, f: (i, 0)),
            pl.BlockSpec((1, d), lambda i, f: (0, 0)),
            pl.BlockSpec((None, d, tf), lambda i, f: (layer, 0, f)),
            pl.BlockSpec((None, d, tf), lambda i, f: (layer, 0, f)),
            pl.BlockSpec((None, tf, d), lambda i, f: (layer, f, 0)),
        ],
        out_specs=pl.BlockSpec((tm, d), lambda i, f: (i, 0)),
        out_shape=jax.ShapeDtypeStruct((n, d), _F32),
        scratch_shapes=[pltpu.VMEM((tm, d), _BF16)],
        compiler_params=_params("parallel", "arbitrary"),
        name="ffn_dense",
    )(h, g2, wg, wu, wd)


_R_E1, _R_E2, _R_RANK1, _R_RANK2, _R_W1, _R_W2 = range(6)


def _router_kernel(h_ref, g_ref, wr_ref, br_ref, xp_ref, info_ref, cnt_ref, carry_sc, *, n_exp):
    @pl.when(pl.program_id(0) == 0)
    def _():
        carry_sc[...] = jnp.zeros(carry_sc.shape, _F32)

    x = h_ref[...]
    xn = x * _rms_scale(x) * g_ref[...]
    tm, d = xn.shape

    xb = xn.astype(_BF16).astype(_F32)
    lo = lax.bitcast_convert_type(xb[:, :d // 2], jnp.int32)
    hi = lax.bitcast_convert_type(xb[:, d // 2:], jnp.int32)
    xp_ref[...] = lax.shift_right_logical(lo, 16) | hi

    logits = jnp.dot(xn, wr_ref[...], preferred_element_type=_F32,
                     precision=lax.Precision.HIGHEST) + br_ref[...]
    lane = lax.broadcasted_iota(jnp.int32, logits.shape, 1)
    logits = jnp.where(lane < n_exp, logits, -jnp.inf)
    v1 = jnp.max(logits, axis=-1, keepdims=True)
    i1 = jnp.min(jnp.where(logits == v1, lane, _LANES), axis=-1, keepdims=True)
    pick1 = lane == i1
    rest = jnp.where(pick1, -jnp.inf, logits)
    v2 = jnp.max(rest, axis=-1, keepdims=True)
    i2 = jnp.min(jnp.where(rest == v2, lane, _LANES), axis=-1, keepdims=True)
    pick2 = lane == i2
    e = jnp.exp(v2 - v1)
    w1 = 1.0 / (1.0 + e)
    w2 = e / (1.0 + e)

    cnt = (pick1 | pick2).astype(_F32)
    row = lax.broadcasted_iota(jnp.int32, (tm, tm), 0)
    col = lax.broadcasted_iota(jnp.int32, (tm, tm), 1)
    below = (row > col).astype(_BF16)
    before = jnp.dot(below, cnt.astype(_BF16), preferred_element_type=_F32) + carry_sc[...]
    rank1 = jnp.sum(jnp.where(pick1, before, 0.0), axis=-1, keepdims=True)
    rank2 = jnp.sum(jnp.where(pick2, before, 0.0), axis=-1, keepdims=True)
    carry_sc[...] += jnp.sum(cnt, axis=0, keepdims=True)

    info = jnp.zeros(logits.shape, _F32)
    for slot, val in ((_R_E1, i1.astype(_F32)), (_R_E2, i2.astype(_F32)), (_R_RANK1, rank1),
                      (_R_RANK2, rank2), (_R_W1, w1), (_R_W2, w2)):
        info = jnp.where(lane == slot, val, info)
    info_ref[...] = info
    cnt_ref[...] = jnp.broadcast_to(carry_sc[...], cnt_ref.shape)


def _router(h, g2, wr_pad, br_pad, n_exp):
    n, d = h.shape
    tm = _largest_tile(n, 512, _SUBLANES)
    kern = functools.partial(_router_kernel, n_exp=n_exp)
    return pl.pallas_call(
        kern,
        grid=(n // tm,),
        in_specs=[
            pl.BlockSpec((tm, d), lambda i: (i, 0)),
            pl.BlockSpec((1, d), lambda i: (0, 0)),
            pl.BlockSpec((d, _LANES), lambda i: (0, 0)),
            pl.BlockSpec((1, _LANES), lambda i: (0, 0)),
        ],
        out_specs=[
            pl.BlockSpec((tm, d // 2), lambda i: (i, 0)),
            pl.BlockSpec((tm, _LANES), lambda i: (i, 0)),
            pl.BlockSpec((_SUBLANES, _LANES), lambda i: (0, 0)),
        ],
        out_shape=[
            jax.ShapeDtypeStruct((n, d // 2), jnp.int32),
            jax.ShapeDtypeStruct((n, _LANES), _F32),
            jax.ShapeDtypeStruct((_SUBLANES, _LANES), _F32),
        ],
        scratch_shapes=[pltpu.VMEM((1, _LANES), _F32)],
        compiler_params=_params("arbitrary"),
        name="router",
    )(h, g2, wr_pad, br_pad)


_DMA_ISSUE_UNROLL = 8


def _row_copy(src_ref, src_row, dst_ref, dst_row, sem):
    return pltpu.make_async_copy(src_ref.at[pl.ds(src_row, 1)], dst_ref.at[pl.ds(dst_row, 1)], sem)


def _dispatch_kernel(pos_ref, xp_ref, zeros_ref, xs_ref, sem, *, tc):
    del zeros_ref

    def start(t, carry):
        for k in range(_TOP_K):
            _row_copy(xp_ref, t, xs_ref, pos_ref[_TOP_K * t + k], sem).start()
        return carry

    lax.fori_loop(0, tc, start, 0, unroll=_DMA_ISSUE_UNROLL)
    for _ in range(_TOP_K):
        pltpu.make_async_copy(xp_ref, xs_ref.at[pl.ds(0, tc)], sem).wait()


def _dispatch(pos, xp, rows):
    n, half = xp.shape
    tc = _largest_tile(n, 512, _SUBLANES)
    kern = functools.partial(_dispatch_kernel, tc=tc)
    return pl.pallas_call(
        kern,
        grid=(n // tc,),
        in_specs=[
            pl.BlockSpec((_TOP_K * tc,), lambda i: (i,), memory_space=pltpu.SMEM),
            pl.BlockSpec((tc, half), lambda i: (i, 0)),
            pl.BlockSpec(memory_space=pl.ANY),
        ],
        out_specs=pl.BlockSpec(memory_space=pl.ANY),
        out_shape=jax.ShapeDtypeStruct((rows, half), jnp.int32),
        scratch_shapes=[pltpu.SemaphoreType.DMA(())],
        input_output_aliases={2: 0},
        compiler_params=_params("arbitrary"),
        name="moe_dispatch",
    )(pos, xp, jnp.zeros((rows, half), jnp.int32))


def _expert_changed(te_ref, i):
    return (i == 0) | (te_ref[i] != te_ref[jnp.maximum(i - 1, 0)])


def _moe_up_kernel(te_ref, nact_ref, xs_ref, wg_ref, wu_ref, act_ref, wg_sc, wu_sc):
    i = pl.program_id(1)
    active = i < nact_ref[0]

    @pl.when(active & _expert_changed(te_ref, i))
    def _():
        wg_sc[...] = wg_ref[...].astype(_BF16)
        wu_sc[...] = wu_ref[...].astype(_BF16)

    @pl.when(active)
    def _():
        packed = xs_ref[...]
        lo = lax.bitcast_convert_type(lax.shift_left(packed, 16), _F32)
        hi = lax.bitcast_convert_type(packed & jnp.int32(-65536), _F32)
        x = jnp.concatenate([lo, hi], axis=1).astype(_BF16)
        a = jnp.dot(x, wg_sc[...], preferred_element_type=_F32)
        b = jnp.dot(x, wu_sc[...], preferred_element_type=_F32)
        act_ref[...] = (a * jax.nn.sigmoid(a) * b).astype(act_ref.dtype)

    @pl.when(jnp.logical_not(active))
    def _():
        act_ref[...] = jnp.zeros(act_ref.shape, act_ref.dtype)


def _moe_down_kernel(te_ref, nact_ref, act_ref, wd_ref, y_ref, wd_sc):
    i = pl.program_id(1)
    active = i < nact_ref[0]

    @pl.when(active & _expert_changed(te_ref, i))
    def _():
        wd_sc[...] = wd_ref[...].astype(_BF16)

    @pl.when(active)
    def _():
        y_ref[...] = jnp.dot(act_ref[...], wd_sc[...], preferred_element_type=_F32)

    @pl.when(jnp.logical_not(active))
    def _():
        y_ref[...] = jnp.zeros(y_ref.shape, y_ref.dtype)


def _moe_ffn(tile_expert, n_active, xs, wg, wu, wd, tm):
    rows, half = xs.shape
    d = 2 * half
    ff = wg.shape[2]
    tf = _largest_tile(ff, 512, _LANES)
    tn = _largest_tile(d, 512, _LANES)
    n_tiles = rows // tm

    def row_idx(i, nact):
        return jnp.maximum(jnp.minimum(i, nact[0] - 1), 0)

    up_spec = pltpu.PrefetchScalarGridSpec(
        num_scalar_prefetch=2,
        grid=(ff // tf, n_tiles),
        in_specs=[
            pl.BlockSpec((tm, half), lambda f, i, te, na: (row_idx(i, na), 0)),
            pl.BlockSpec((None, d, tf), lambda f, i, te, na: (te[i], 0, f)),
            pl.BlockSpec((None, d, tf), lambda f, i, te, na: (te[i], 0, f)),
        ],
        out_specs=pl.BlockSpec((tm, tf), lambda f, i, te, na: (i, f)),
        scratch_shapes=[pltpu.VMEM((d, tf), _BF16), pltpu.VMEM((d, tf), _BF16)],
    )
    act = pl.pallas_call(
        _moe_up_kernel,
        grid_spec=up_spec,
        out_shape=jax.ShapeDtypeStruct((rows, ff), _BF16),
        compiler_params=_params("arbitrary", "arbitrary"),
        name="moe_up",
    )(tile_expert, n_active, xs, wg, wu)

    down_spec = pltpu.PrefetchScalarGridSpec(
        num_scalar_prefetch=2,
        grid=(d // tn, n_tiles),
        in_specs=[
            pl.BlockSpec((tm, ff), lambda n, i, te, na: (row_idx(i, na), 0)),
            pl.BlockSpec((None, ff, tn), lambda n, i, te, na: (te[i], 0, n)),
        ],
        out_specs=pl.BlockSpec((tm, tn), lambda n, i, te, na: (i, n)),
        scratch_shapes=[pltpu.VMEM((ff, tn), _BF16)],
    )
    return pl.pallas_call(
        _moe_down_kernel,
        grid_spec=down_spec,
        out_shape=jax.ShapeDtypeStruct((rows, d), _F32),
        compiler_params=_params("arbitrary", "arbitrary"),
        name="moe_down",
    )(tile_expert, n_active, act, wd)


def _combine_kernel(pos_ref, info_ref, h_ref, y_ref, g_ref, out_ref, buf, sem, *, tc, final_norm):
    def start(t, carry):
        for k in range(_TOP_K):
            pltpu.make_async_copy(y_ref.at[pl.ds(pos_ref[_TOP_K * t + k], 1)],
                                  buf.at[k, pl.ds(t, 1)], sem).start()
        return carry

    lax.fori_loop(0, tc, start, 0, unroll=_DMA_ISSUE_UNROLL)
    for k in range(_TOP_K):
        pltpu.make_async_copy(y_ref.at[pl.ds(0, tc)], buf.at[k], sem).wait()

    info = info_ref[...]
    w1 = info[:, _R_W1:_R_W1 + 1]
    w2 = info[:, _R_W2:_R_W2 + 1]
    out = h_ref[...] + (w1 * buf[0] + w2 * buf[1])
    if final_norm:
        out = out * _rms_scale(out) * g_ref[...]
    out_ref[...] = out


def _combine(pos, info, h, y, final_g, final_norm):
    n, d = h.shape
    tc = _largest_tile(n, 256, _SUBLANES)
    kern = functools.partial(_combine_kernel, tc=tc, final_norm=final_norm)
    return pl.pallas_call(
        kern,
        grid=(n // tc,),
        in_specs=[
            pl.BlockSpec((_TOP_K * tc,), lambda i: (i,), memory_space=pltpu.SMEM),
            pl.BlockSpec((tc, _LANES), lambda i: (i, 0)),
            pl.BlockSpec((tc, d), lambda i: (i, 0)),
            pl.BlockSpec(memory_space=pl.ANY),
            pl.BlockSpec((1, d), lambda i: (0, 0)),
        ],
        out_specs=pl.BlockSpec((tc, d), lambda i: (i, 0)),
        out_shape=jax.ShapeDtypeStruct((n, d), _F32),
        scratch_shapes=[pltpu.VMEM((_TOP_K, tc, d), _F32), pltpu.SemaphoreType.DMA(())],
        compiler_params=_params("arbitrary"),
        name="moe_combine",
    )(pos, info, h, y, final_g)


def _final_norm_kernel(h_ref, g_ref, out_ref):
    x = h_ref[...]
    out_ref[...] = x * _rms_scale(x) * g_ref[...]


def _final_norm(h, g):
    n, d = h.shape
    tm = _largest_tile(n, 512, _SUBLANES)
    return pl.pallas_call(
        _final_norm_kernel,
        grid=(n // tm,),
        in_specs=[pl.BlockSpec((tm, d), lambda i: (i, 0)), pl.BlockSpec((1, d), lambda i: (0, 0))],
        out_specs=pl.BlockSpec((tm, d), lambda i: (i, 0)),
        out_shape=jax.ShapeDtypeStruct((n, d), _F32),
        compiler_params=_params("parallel"),
        name="final_norm",
    )(h, g)


_MOE_ROW_TILE = 512


def _moe(h, g2, wr, br, wg, wu, wd, layer, final_g, final_norm):
    n, d = h.shape
    n_exp = wr.shape[1]
    tm = _largest_tile(n, _MOE_ROW_TILE, _SUBLANES)
    wr_pad = jnp.zeros((d, _LANES), _F32).at[:, :n_exp].set(wr)
    br_pad = jnp.zeros((1, _LANES), _F32).at[0, :n_exp].set(br)
    xp, info, cnt = _router(h, g2, wr_pad, br_pad, n_exp)

    counts = cnt[0, :n_exp].astype(jnp.int32)
    tiles_per = (counts + tm - 1) // tm
    tile_end = jnp.cumsum(tiles_per)
    starts = (tile_end - tiles_per) * tm
    n_active = tile_end[-1:]
    n_tiles = (_TOP_K * n) // tm + n_exp
    tile_ids = jnp.minimum(jnp.arange(n_tiles, dtype=jnp.int32), n_active - 1)
    tile_expert = jnp.sum((tile_ids[:, None] >= tile_end[None, :]).astype(jnp.int32), axis=1)
    tile_expert = tile_expert + layer * n_exp
    experts = info[:, _R_E1:_R_E2 + 1].astype(jnp.int32)
    ranks = info[:, _R_RANK1:_R_RANK2 + 1].astype(jnp.int32)
    onehot = experts[:, :, None] == jnp.arange(n_exp, dtype=jnp.int32)[None, None, :]
    pos = (jnp.sum(jnp.where(onehot, starts[None, None, :], 0), axis=-1) + ranks).reshape(-1)

    xs = _dispatch(pos, xp, n_tiles * tm)
    y = _moe_ffn(tile_expert, n_active, xs, wg, wu, wd, tm)
    return _combine(pos, info, h, y, final_g, final_norm)


def _rope_tables(seq, hd):
    axis_dim = hd // 2
    nfreq = axis_dim // 2
    rows = seq // _GRID_W
    row = jnp.broadcast_to(jnp.arange(rows)[:, None], (rows, _GRID_W)).reshape(seq).astype(_F32)
    col = jnp.broadcast_to(jnp.arange(_GRID_W)[None, :], (rows, _GRID_W)).reshape(seq).astype(_F32)
    inv = _ROPE_THETA ** (-jnp.arange(nfreq, dtype=_F32) * 2.0 / axis_dim)
    ar = row[:, None] * inv
    ac = col[:, None] * inv
    ang = jnp.concatenate([ar, ar, ac, ac], axis=-1)
    lane = jnp.arange(hd)
    sign = jnp.where((lane % axis_dim) < nfreq, -1.0, 1.0).astype(_F32)
    return jnp.cos(ang), jnp.sin(ang) * sign


def kernel(x, norm1_g, w_in, q_norm_g, k_norm_g, w_attn_o, w_pool_mix, pool_scale, w_pool_o,
           w_out, norm2_g, ffn_w_gate, ffn_w_up, ffn_w_down, router_w, router_b,
           moe_w_gate, moe_w_up, moe_w_down, final_g):
    batch, seq, d = x.shape
    depth = w_in.shape[0]
    hd = q_norm_g.shape[1]
    q_dim = w_attn_o.shape[1]
    p_dim = pool_scale.shape[1]
    kv_dim = (w_in.shape[2] - q_dim - p_dim - 2 * d) // 2
    dims = (q_dim, kv_dim, p_dim, hd, seq)
    assert seq % _GRID_W == 0 and q_dim % kv_dim == 0 and kv_dim % hd == 0

    cos, sin_signed = _rope_tables(seq, hd)
    bf = lambda w: w.astype(_BF16)
    row = lambda v: v.reshape(1, -1)
    experts = lambda w: w.reshape((-1,) + w.shape[2:])

    w_in, w_attn_o, w_pool_o, w_out = bf(w_in), bf(w_attn_o), bf(w_pool_o), bf(w_out)
    ffn_w_gate, ffn_w_up, ffn_w_down = bf(ffn_w_gate), bf(ffn_w_up), bf(ffn_w_down)
    moe_w_gate, moe_w_up, moe_w_down = experts(moe_w_gate), experts(moe_w_up), experts(moe_w_down)

    h = x.reshape(batch * seq, d)
    fg = row(final_g)
    for l in range(depth):
        q, k, vt, u, gates = _in_proj(h, row(norm1_g[l]), w_in, l, cos, sin_signed,
                                      row(q_norm_g[l]), row(k_norm_g[l]), dims)
        o = _attention(q, k, vt, dims, batch)
        pooled = _pool(u, bf(w_pool_mix[l]), row(pool_scale[l]), seq)
        merged = _merge(o, pooled, w_attn_o, w_pool_o, l, gates)
        h = _out_proj(merged, w_out, l, h)
        i = l // 2
        if l % 2 == 0:
            h = _ffn(h, row(norm2_g[l]), ffn_w_gate, ffn_w_up, ffn_w_down, i)
        else:
            last = l == depth - 1
            h = _moe(h, row(norm2_g[l]), router_w[i], router_b[i], moe_w_gate, moe_w_up,
                     moe_w_down, i, fg, last)
    if depth % 2 == 1 or depth == 0:
        h = _final_norm(h, fg)
    return h.reshape(batch, seq, d)
```

```python
import functools

import jax
import jax.numpy as jnp
from jax import lax
from jax.experimental import pallas as pl
from jax.experimental.pallas import tpu as pltpu

_GRID_W = 64
_ROPE_THETA = 10000.0
_POOL_WINDOWS = (2, 4, 8, 16)
_TOP_K = 2
_NORM_EPS = 1e-6
_LOG2_E = 1.4426950408889634

_LANES = 128
_SUBLANES = 8
_SUM_ROWS = 16
_V7X_VMEM_LIMIT_BYTES = 56 * 1024 * 1024

_F32 = jnp.float32
_BF16 = jnp.bfloat16
_NT_DIMS = (((1,), (1,)), ((), ()))


def _params(*semantics):
    return pltpu.CompilerParams(dimension_semantics=semantics,
                                vmem_limit_bytes=_V7X_VMEM_LIMIT_BYTES)


def _largest_tile(n, cap, quantum):
    t = min(cap, n)
    t -= t % quantum
    while t > quantum and n % t:
        t -= quantum
    assert t >= quantum and n % t == 0, (n, cap, quantum)
    return t


def _rms_scale(x):
    return lax.rsqrt(jnp.mean(x * x, axis=-1, keepdims=True) + _NORM_EPS)


_ROW_PARTS = 4


def _in_proj_kernel(h_ref, g_ref, w_ref, cos_ref, sin_ref, qg_ref, kg_ref,
                    q_ref, k_ref, vt_ref, u_ref, gate_ref, xn_sc, acc_sc,
                    *, nq, nk, nv, nu, ng, hd, tk, qscale):
    j = pl.program_id(1)
    ncol = nq + nk + nv + nu + ng
    tn = w_ref.shape[1]
    heads = tn // hd

    tm = xn_sc.shape[0]

    def matmul_rows(slot, rows):
        acc_sc[slot, rows, :] = jnp.dot(xn_sc[rows, :], w_ref[...], preferred_element_type=_F32)

    def norm_rope(acc, rows, gain, scale):
        cos = cos_ref[rows, :]
        sin = sin_ref[rows, :]
        lane = lax.broadcasted_iota(jnp.int32, (1, hd), 1)
        first_half = (lane % (hd // 2)) < (hd // 4)
        outs = []
        for hh in range(heads):
            xh = acc[:, hh * hd:(hh + 1) * hd]
            xh = xh * _rms_scale(xh) * gain
            partner = jnp.where(first_half,
                                pltpu.roll(xh, hd - hd // 4, 1),
                                pltpu.roll(xh, hd // 4, 1))
            outs.append(((xh * cos + partner * sin) * scale).astype(_BF16))
        return outs

    def finish_q(acc, rows):
        q_ref[rows, :] = jnp.concatenate(norm_rope(acc, rows, qg_ref[...], qscale), axis=1)

    def finish_k(acc, rows):
        for hh, kh in enumerate(norm_rope(acc, rows, kg_ref[...], 1.0)):
            k_ref[hh, rows, :] = kh

    def finish_v(acc, rows):
        sum_rows = (lax.broadcasted_iota(jnp.int32, (_SUM_ROWS, tk), 0) == 0).astype(_BF16)
        c = rows.start // tk
        for hh in range(heads):
            vt_ref[hh, c, 0:hd, :] = acc[:, hh * hd:(hh + 1) * hd].T.astype(_BF16)
            vt_ref[hh, c, hd:hd + _SUM_ROWS, :] = sum_rows

    def finish_u(acc, rows):
        u_ref[rows, :] = acc

    def finish_gate(acc, rows):
        gate_ref[rows, :] = (0.5 * jnp.tanh(0.5 * acc) + 0.5).astype(_BF16)

    def row_parts(parts):
        return [slice(r * (tm // parts), (r + 1) * (tm // parts)) for r in range(parts)]

    @pl.when(j == 0)
    def _():
        for rows in row_parts(_ROW_PARTS):
            x = h_ref[rows, :]
            xn_sc[rows, :] = (x * _rms_scale(x) * g_ref[...]).astype(_BF16)
            matmul_rows(0, rows)

    t = j - 1
    lo = 0
    for cnt, finish, parts in ((nq, finish_q, 1), (nk, finish_k, 1),
                               (nv, finish_v, tm // tk), (nu, finish_u, _ROW_PARTS),
                               (ng, finish_gate, _ROW_PARTS)):
        @pl.when((t >= lo) & (t < lo + cnt) & (j < ncol))
        def _(finish=finish, parts=parts):
            for rows in row_parts(parts):
                finish(acc_sc[t % 2, rows, :], rows)
                matmul_rows(j % 2, rows)
        lo += cnt

    @pl.when(j == ncol)
    def _():
        finish_gate(acc_sc[t % 2], slice(0, tm))


def _attn_tiles(seq):
    tq = _largest_tile(seq, 256, _LANES)
    tk = _largest_tile(seq // 2, 512, _LANES)
    assert (seq // tk) % 2 == 0
    return tq, tk


def _in_proj(h, g1, w, layer, cos, sin_signed, qg, kg, dims):
    n, d = h.shape
    q_dim, kv_dim, p_dim, hd, seq = dims
    _, tk = _attn_tiles(seq)
    cols = w.shape[2]
    tm = _largest_tile(seq, 1024, _SUBLANES)
    tn = _largest_tile(_gcd_all(q_dim, kv_dim, p_dim, d), 512, hd)
    nq, nk, nv, nu, ng = q_dim // tn, kv_dim // tn, kv_dim // tn, p_dim // tn, 2 * d // tn
    ncol = nq + nk + nv + nu + ng
    assert ncol * tn == cols and tm % tk == 0
    s_tiles = seq // tm
    heads = tn // hd

    def tile(j, lo, cnt):
        return jnp.clip(j - 1 - lo, 0, cnt - 1)

    kern = functools.partial(_in_proj_kernel, nq=nq, nk=nk, nv=nv, nu=nu, ng=ng, hd=hd, tk=tk,
                             qscale=_LOG2_E * float(hd) ** -0.5)
    return pl.pallas_call(
        kern,
        grid=(n // tm, ncol + 1),
        in_specs=[
            pl.BlockSpec((tm, d), lambda i, j: (i, 0)),
            pl.BlockSpec((1, d), lambda i, j: (0, 0)),
            pl.BlockSpec((None, d, tn), lambda i, j: (layer, 0, jnp.minimum(j, ncol - 1))),
            pl.BlockSpec((tm, hd), lambda i, j: (i % s_tiles, 0)),
            pl.BlockSpec((tm, hd), lambda i, j: (i % s_tiles, 0)),
            pl.BlockSpec((1, hd), lambda i, j: (0, 0)),
            pl.BlockSpec((1, hd), lambda i, j: (0, 0)),
        ],
        out_specs=[
            pl.BlockSpec((tm, tn), lambda i, j: (i, tile(j, 0, nq))),
            pl.BlockSpec((heads, tm, hd), lambda i, j: (tile(j, nq, nk), i, 0)),
            pl.BlockSpec((heads, tm // tk, hd + _SUM_ROWS, tk),
                         lambda i, j: (tile(j, nq + nk, nv), i, 0, 0)),
            pl.BlockSpec((tm, tn), lambda i, j: (i, tile(j, nq + nk + nv, nu))),
            pl.BlockSpec((tm, tn), lambda i, j: (i, tile(j, nq + nk + nv + nu, ng))),
        ],
        out_shape=[
            jax.ShapeDtypeStruct((n, q_dim), _BF16),
            jax.ShapeDtypeStruct((kv_dim // hd, n, hd), _BF16),
            jax.ShapeDtypeStruct((kv_dim // hd, n // tk, hd + _SUM_ROWS, tk), _BF16),
            jax.ShapeDtypeStruct((n, p_dim), _F32),
            jax.ShapeDtypeStruct((n, 2 * d), _BF16),
        ],
        scratch_shapes=[pltpu.VMEM((tm, d), _BF16), pltpu.VMEM((2, tm, tn), _F32)],
        compiler_params=_params("parallel", "arbitrary"),
        name="in_proj",
    )(h, g1, w, cos, sin_signed, qg, kg)


def _gcd_all(*vals):
    import math
    g = 0
    for v in vals:
        g = math.gcd(g, v)
    return g


def _attn_kernel(q_ref, k_ref, vt_ref, o_ref, qs_sc, s_sc, cm_sc, acc_sc,
                 *, groups, hd, tq, tk, seq):
    n_chunks = seq // tk
    cols = groups * tq

    for g in range(groups):
        qs_sc[g * tq:(g + 1) * tq, :] = q_ref[:, g * hd:(g + 1) * hd]
    acc_sc[...] = jnp.zeros(acc_sc.shape, _F32)

    def scores(c, slot):
        k_c = k_ref[pl.ds(pl.multiple_of(c * tk, tk), tk), :]
        s = lax.dot_general(k_c, qs_sc[...], _NT_DIMS, preferred_element_type=_F32)
        s_sc[slot] = s
        cm_sc[slot] = jnp.max(s, axis=0, keepdims=True)

    def softmax_values(c, slot, m_prev):
        m_new = jnp.maximum(m_prev, cm_sc[slot])
        alpha = jnp.exp2(m_prev - m_new)
        p = jnp.exp2(s_sc[slot] - m_new)
        acc_sc[...] = acc_sc[...] * alpha + jnp.dot(vt_ref[c], p.astype(_BF16),
                                                    preferred_element_type=_F32)
        return m_new

    def pair(j, m):
        c = 2 * j
        scores(c + 1, 1)
        m = softmax_values(c, 0, m)
        scores(c + 2, 0)
        return softmax_values(c + 1, 1, m)

    m = jnp.full((1, cols), -jnp.inf, _F32)
    scores(0, 0)
    m = lax.fori_loop(0, n_chunks // 2 - 1, pair, m)
    scores(n_chunks - 1, 1)
    m = softmax_values(n_chunks - 2, 0, m)
    m = softmax_values(n_chunks - 1, 1, m)

    out_t = acc_sc[0:hd, :] / acc_sc[hd:hd + 1, :]
    for g in range(groups):
        o_ref[:, g * hd:(g + 1) * hd] = out_t[:, g * tq:(g + 1) * tq].T.astype(o_ref.dtype)


def _attention(q, k, vt, dims, batch):
    q_dim, kv_dim, _, hd, seq = dims
    n_kv = kv_dim // hd
    groups = q_dim // kv_dim
    tq, tk = _attn_tiles(seq)
    q3 = q.reshape(batch, seq, q_dim)
    cols = groups * tq
    kern = functools.partial(_attn_kernel, groups=groups, hd=hd, tq=tq, tk=tk, seq=seq)
    o = pl.pallas_call(
        kern,
        grid=(batch, n_kv, seq // tq),
        in_specs=[
            pl.BlockSpec((None, tq, groups * hd), lambda b, kh, i: (b, i, kh)),
            pl.BlockSpec((None, seq, hd), lambda b, kh, i: (kh, b, 0)),
            pl.BlockSpec((None, seq // tk, hd + _SUM_ROWS, tk), lambda b, kh, i: (kh, b, 0, 0)),
        ],
        out_specs=pl.BlockSpec((None, tq, groups * hd), lambda b, kh, i: (b, i, kh)),
        out_shape=jax.ShapeDtypeStruct((batch, seq, q_dim), _BF16),
        scratch_shapes=[
            pltpu.VMEM((cols, hd), _BF16),
            pltpu.VMEM((2, tk, cols), _F32),
            pltpu.VMEM((2, 1, cols), _F32),
            pltpu.VMEM((hd + _SUM_ROWS, cols), _F32),
        ],
        compiler_params=_params("parallel", "parallel", "parallel"),
        name="attention",
    )(q3, k, vt)
    return o.reshape(batch * seq, q_dim)


_POOL_HALO = 8


def _pool_kernel(prev_ref, u_ref, next_ref, wmix_ref, scale_ref, o_ref, ext_sc, *, tm, seq, grp):
    i = pl.program_id(0)
    halo = _POOL_HALO
    ext_sc[0:halo, :] = prev_ref[...]
    ext_sc[halo:halo + tm, :] = u_ref[...]
    ext_sc[halo + tm:halo + tm + halo, :] = next_ref[...]
    t = (i % (seq // tm)) * tm + lax.broadcasted_iota(jnp.int32, (tm, 1), 0)
    for g, w in enumerate(_POOL_WINDOWS):
        sl = slice(g * grp, (g + 1) * grp)
        wsum = jnp.zeros((tm, grp), _F32)
        for k in range(-(w // 2), w // 2):
            rows = ext_sc[halo + k:halo + k + tm, sl]
            inside = (t + k >= 0) & (t + k <= seq - 1)
            wsum = wsum + jnp.where(inside, rows, 0.0)
        lo = jnp.maximum(t - w // 2, 0)
        hi = jnp.minimum(t + w // 2 - 1, seq - 1)
        cnt = (hi - lo + 1).astype(_F32)
        dev = wsum / cnt - u_ref[:, sl]
        y = jnp.dot(dev.astype(_BF16), wmix_ref[g], preferred_element_type=_F32)
        o_ref[:, sl] = (y * scale_ref[:, sl]).astype(o_ref.dtype)


def _pool(u, wmix, scale, seq):
    n, p = u.shape
    grp = p // len(_POOL_WINDOWS)
    tm = _largest_tile(seq, 512, _POOL_HALO)
    halo = _POOL_HALO
    hb = tm // halo
    kern = functools.partial(_pool_kernel, tm=tm, seq=seq, grp=grp)
    return pl.pallas_call(
        kern,
        grid=(n // tm,),
        in_specs=[
            pl.BlockSpec((halo, p), lambda i: (jnp.maximum(i * hb - 1, 0), 0)),
            pl.BlockSpec((tm, p), lambda i: (i, 0)),
            pl.BlockSpec((halo, p), lambda i: (jnp.minimum((i + 1) * hb, n // halo - 1), 0)),
            pl.BlockSpec(wmix.shape, lambda i: (0, 0, 0)),
            pl.BlockSpec((1, p), lambda i: (0, 0)),
        ],
        out_specs=pl.BlockSpec((tm, p), lambda i: (i, 0)),
        out_shape=jax.ShapeDtypeStruct((n, p), _BF16),
        scratch_shapes=[pltpu.VMEM((tm + 2 * halo, p), _F32)],
        compiler_params=_params("parallel"),
        name="pool",
    )(u, u, u, wmix, scale)


def _merge_kernel(o_ref, p_ref, wa_ref, wp_ref, ga_ref, gp_ref, out_ref):
    a = jnp.dot(o_ref[...], wa_ref[...], preferred_element_type=_F32)
    p = jnp.dot(p_ref[...], wp_ref[...], preferred_element_type=_F32)
    merged = ga_ref[...].astype(_F32) * a + gp_ref[...].astype(_F32) * p
    out_ref[...] = merged.astype(out_ref.dtype)


def _merge(o, pooled, wa, wp, layer, gates):
    n, q_dim = o.shape
    p_dim = pooled.shape[1]
    d = wa.shape[2]
    tm = _largest_tile(n, 1024, _SUBLANES)
    tn = _largest_tile(d, 1024, _LANES)
    nj = d // tn
    return pl.pallas_call(
        _merge_kernel,
        grid=(n // tm, nj),
        in_specs=[
            pl.BlockSpec((tm, q_dim), lambda i, j: (i, 0)),
            pl.BlockSpec((tm, p_dim), lambda i, j: (i, 0)),
            pl.BlockSpec((None, q_dim, tn), lambda i, j: (layer, 0, j)),
            pl.BlockSpec((None, p_dim, tn), lambda i, j: (layer, 0, j)),
            pl.BlockSpec((tm, tn), lambda i, j: (i, j)),
            pl.BlockSpec((tm, tn), lambda i, j: (i, nj + j)),
        ],
        out_specs=pl.BlockSpec((tm, tn), lambda i, j: (i, j)),
        out_shape=jax.ShapeDtypeStruct((n, d), _BF16),
        compiler_params=_params("parallel", "arbitrary"),
        name="merge",
    )(o, pooled, wa, wp, gates, gates)


def _out_proj_kernel(a_ref, w_ref, res_ref, out_ref):
    out_ref[...] = res_ref[...] + jnp.dot(a_ref[...], w_ref[...], preferred_element_type=_F32)


def _out_proj(merged, w, layer, h):
    n, d = h.shape
    tm = _largest_tile(n, 1024, _SUBLANES)
    tn = _largest_tile(d, 1024, _LANES)
    return pl.pallas_call(
        _out_proj_kernel,
        grid=(n // tm, d // tn),
        in_specs=[
            pl.BlockSpec((tm, merged.shape[1]), lambda i, j: (i, 0)),
            pl.BlockSpec((None, merged.shape[1], tn), lambda i, j: (layer, 0, j)),
            pl.BlockSpec((tm, tn), lambda i, j: (i, j)),
        ],
        out_specs=pl.BlockSpec((tm, tn), lambda i, j: (i, j)),
        out_shape=jax.ShapeDtypeStruct((n, d), _F32),
        compiler_params=_params("parallel", "arbitrary"),
        name="out_proj",
    )(merged, w, h)


def _swiglu_step(xn, wg_ref, wu_ref, wd_ref):
    a = jnp.dot(xn, wg_ref[...], preferred_element_type=_F32)
    b = jnp.dot(xn, wu_ref[...], preferred_element_type=_F32)
    act = (a * jax.nn.sigmoid(a) * b).astype(_BF16)
    return jnp.dot(act, wd_ref[...], preferred_element_type=_F32)


def _ffn_kernel(h_ref, g_ref, wg_ref, wu_ref, wd_ref, out_ref, xn_sc):
    @pl.when(pl.program_id(1) == 0)
    def _():
        x = h_ref[...]
        xn_sc[...] = (x * _rms_scale(x) * g_ref[...]).astype(_BF16)
        out_ref[...] = x

    out_ref[...] += _swiglu_step(xn_sc[...], wg_ref, wu_ref, wd_ref)


def _ffn(h, g2, wg, wu, wd, layer):
    n, d = h.shape
    ff = wg.shape[2]
    tm = _largest_tile(n, 512, _SUBLANES)
    tf = _largest_tile(ff, 512, _LANES)
    return pl.pallas_call(
        _ffn_kernel,
        grid=(n // tm, ff // tf),
        in_specs=[
            pl.BlockSpec((tm, d), lambda i, f: (i, 0)),
            pl.BlockSpec((1, d), lambda i, f: (0, 0)),
            pl.BlockSpec((None, d, tf), lambda i, f: (layer, 0, f)),
            pl.BlockSpec((None, d, tf), lambda i, f: (layer, 0, f)),
            pl.BlockSpec((None, tf, d), lambda i, f: (layer, f, 0)),
        ],
        out_specs=pl.BlockSpec((tm, d), lambda i, f: (i, 0)),
        out_shape=jax.ShapeDtypeStruct((n, d), _F32),
        scratch_shapes=[pltpu.VMEM((tm, d), _BF16)],
        compiler_params=_params("parallel", "arbitrary"),
        name="ffn_dense",
    )(h, g2, wg, wu, wd)


_R_E1, _R_E2, _R_RANK1, _R_RANK2, _R_W1, _R_W2 = range(6)


def _router_kernel(h_ref, g_ref, wr_ref, br_ref, xp_ref, info_ref, cnt_ref, carry_sc, *, n_exp):
    @pl.when(pl.program_id(0) == 0)
    def _():
        carry_sc[...] = jnp.zeros(carry_sc.shape, _F32)

    x = h_ref[...]
    xn = x * _rms_scale(x) * g_ref[...]
    tm, d = xn.shape

    xb = xn.astype(_BF16).astype(_F32)
    lo = lax.bitcast_convert_type(xb[:, :d // 2], jnp.int32)
    hi = lax.bitcast_convert_type(xb[:, d // 2:], jnp.int32)
    xp_ref[...] = lax.shift_right_logical(lo, 16) | hi

    logits = jnp.dot(xn, wr_ref[...], preferred_element_type=_F32,
                     precision=lax.Precision.HIGHEST) + br_ref[...]
    lane = lax.broadcasted_iota(jnp.int32, logits.shape, 1)
    logits = jnp.where(lane < n_exp, logits, -jnp.inf)
    v1 = jnp.max(logits, axis=-1, keepdims=True)
    i1 = jnp.min(jnp.where(logits == v1, lane, _LANES), axis=-1, keepdims=True)
    pick1 = lane == i1
    rest = jnp.where(pick1, -jnp.inf, logits)
    v2 = jnp.max(rest, axis=-1, keepdims=True)
    i2 = jnp.min(jnp.where(rest == v2, lane, _LANES), axis=-1, keepdims=True)
    pick2 = lane == i2
    e = jnp.exp(v2 - v1)
    w1 = 1.0 / (1.0 + e)
    w2 = e / (1.0 + e)

    cnt = (pick1 | pick2).astype(_F32)
    row = lax.broadcasted_iota(jnp.int32, (tm, tm), 0)
    col = lax.broadcasted_iota(jnp.int32, (tm, tm), 1)
    below = (row > col).astype(_BF16)
    before = jnp.dot(below, cnt.astype(_BF16), preferred_element_type=_F32) + carry_sc[...]
    rank1 = jnp.sum(jnp.where(pick1, before, 0.0), axis=-1, keepdims=True)
    rank2 = jnp.sum(jnp.where(pick2, before, 0.0), axis=-1, keepdims=True)
    carry_sc[...] += jnp.sum(cnt, axis=0, keepdims=True)

    info = jnp.zeros(logits.shape, _F32)
    for slot, val in ((_R_E1, i1.astype(_F32)), (_R_E2, i2.astype(_F32)), (_R_RANK1, rank1),
                      (_R_RANK2, rank2), (_R_W1, w1), (_R_W2, w2)):
        info = jnp.where(lane == slot, val, info)
    info_ref[...] = info
    cnt_ref[...] = jnp.broadcast_to(carry_sc[...], cnt_ref.shape)


def _router(h, g2, wr_pad, br_pad, n_exp):
    n, d = h.shape
    tm = _largest_tile(n, 512, _SUBLANES)
    kern = functools.partial(_router_kernel, n_exp=n_exp)
    return pl.pallas_call(
        kern,
        grid=(n // tm,),
        in_specs=[
            pl.BlockSpec((tm, d), lambda i: (i, 0)),
            pl.BlockSpec((1, d), lambda i: (0, 0)),
            pl.BlockSpec((d, _LANES), lambda i: (0, 0)),
            pl.BlockSpec((1, _LANES), lambda i: (0, 0)),
        ],
        out_specs=[
            pl.BlockSpec((tm, d // 2), lambda i: (i, 0)),
            pl.BlockSpec((tm, _LANES), lambda i: (i, 0)),
            pl.BlockSpec((_SUBLANES, _LANES), lambda i: (0, 0)),
        ],
        out_shape=[
            jax.ShapeDtypeStruct((n, d // 2), jnp.int32),
            jax.ShapeDtypeStruct((n, _LANES), _F32),
            jax.ShapeDtypeStruct((_SUBLANES, _LANES), _F32),
        ],
        scratch_shapes=[pltpu.VMEM((1, _LANES), _F32)],
        compiler_params=_params("arbitrary"),
        name="router",
    )(h, g2, wr_pad, br_pad)


_DMA_ISSUE_UNROLL = 8


def _row_copy(src_ref, src_row, dst_ref, dst_row, sem):
    return pltpu.make_async_copy(src_ref.at[pl.ds(src_row, 1)], dst_ref.at[pl.ds(dst_row, 1)], sem)


def _dispatch_kernel(pos_ref, xp_ref, zeros_ref, xs_ref, sem, *, tc):
    del zeros_ref

    def start(t, carry):
        for k in range(_TOP_K):
            _row_copy(xp_ref, t, xs_ref, pos_ref[_TOP_K * t + k], sem).start()
        return carry

    lax.fori_loop(0, tc, start, 0, unroll=_DMA_ISSUE_UNROLL)
    for _ in range(_TOP_K):
        pltpu.make_async_copy(xp_ref, xs_ref.at[pl.ds(0, tc)], sem).wait()


def _dispatch(pos, xp, rows):
    n, half = xp.shape
    tc = _largest_tile(n, 512, _SUBLANES)
    kern = functools.partial(_dispatch_kernel, tc=tc)
    return pl.pallas_call(
        kern,
        grid=(n // tc,),
        in_specs=[
            pl.BlockSpec((_TOP_K * tc,), lambda i: (i,), memory_space=pltpu.SMEM),
            pl.BlockSpec((tc, half), lambda i: (i, 0)),
            pl.BlockSpec(memory_space=pl.ANY),
        ],
        out_specs=pl.BlockSpec(memory_space=pl.ANY),
        out_shape=jax.ShapeDtypeStruct((rows, half), jnp.int32),
        scratch_shapes=[pltpu.SemaphoreType.DMA(())],
        input_output_aliases={2: 0},
        compiler_params=_params("arbitrary"),
        name="moe_dispatch",
    )(pos, xp, jnp.zeros((rows, half), jnp.int32))


def _expert_changed(te_ref, i):
    return (i == 0) | (te_ref[i] != te_ref[jnp.maximum(i - 1, 0)])


def _moe_up_kernel(te_ref, nact_ref, xs_ref, wg_ref, wu_ref, act_ref, wg_sc, wu_sc):
    i = pl.program_id(1)
    active = i < nact_ref[0]

    @pl.when(active & _expert_changed(te_ref, i))
    def _():
        wg_sc[...] = wg_ref[...].astype(_BF16)
        wu_sc[...] = wu_ref[...].astype(_BF16)

    @pl.when(active)
    def _():
        packed = xs_ref[...]
        lo = lax.bitcast_convert_type(lax.shift_left(packed, 16), _F32)
        hi = lax.bitcast_convert_type(packed & jnp.int32(-65536), _F32)
        x = jnp.concatenate([lo, hi], axis=1).astype(_BF16)
        a = jnp.dot(x, wg_sc[...], preferred_element_type=_F32)
        b = jnp.dot(x, wu_sc[...], preferred_element_type=_F32)
        act_ref[...] = (a * jax.nn.sigmoid(a) * b).astype(act_ref.dtype)

    @pl.when(jnp.logical_not(active))
    def _():
        act_ref[...] = jnp.zeros(act_ref.shape, act_ref.dtype)


def _moe_down_kernel(te_ref, nact_ref, act_ref, wd_ref, y_ref, wd_sc):
    i = pl.program_id(1)
    active = i < nact_ref[0]

    @pl.when(active & _expert_changed(te_ref, i))
    def _():
        wd_sc[...] = wd_ref[...].astype(_BF16)

    @pl.when(active)
    def _():
        y_ref[...] = jnp.dot(act_ref[...], wd_sc[...], preferred_element_type=_F32)

    @pl.when(jnp.logical_not(active))
    def _():
        y_ref[...] = jnp.zeros(y_ref.shape, y_ref.dtype)


def _moe_ffn(tile_expert, n_active, xs, wg, wu, wd, tm):
    rows, half = xs.shape
    d = 2 * half
    ff = wg.shape[2]
    tf = _largest_tile(ff, 512, _LANES)
    tn = _largest_tile(d, 512, _LANES)
    n_tiles = rows // tm

    def row_idx(i, nact):
        return jnp.maximum(jnp.minimum(i, nact[0] - 1), 0)

    up_spec = pltpu.PrefetchScalarGridSpec(
        num_scalar_prefetch=2,
        grid=(ff // tf, n_tiles),
        in_specs=[
            pl.BlockSpec((tm, half), lambda f, i, te, na: (row_idx(i, na), 0)),
            pl.BlockSpec((None, d, tf), lambda f, i, te, na: (te[i], 0, f)),
            pl.BlockSpec((None, d, tf), lambda f, i, te, na: (te[i], 0, f)),
        ],
        out_specs=pl.BlockSpec((tm, tf), lambda f, i, te, na: (i, f)),
        scratch_shapes=[pltpu.VMEM((d, tf), _BF16), pltpu.VMEM((d, tf), _BF16)],
    )
    act = pl.pallas_call(
        _moe_up_kernel,
        grid_spec=up_spec,
        out_shape=jax.ShapeDtypeStruct((rows, ff), _BF16),
        compiler_params=_params("arbitrary", "arbitrary"),
        name="moe_up",
    )(tile_expert, n_active, xs, wg, wu)

    down_spec = pltpu.PrefetchScalarGridSpec(
        num_scalar_prefetch=2,
        grid=(d // tn, n_tiles),
        in_specs=[
            pl.BlockSpec((tm, ff), lambda n, i, te, na: (row_idx(i, na), 0)),
            pl.BlockSpec((None, ff, tn), lambda n, i, te, na: (te[i], 0, n)),
        ],
        out_specs=pl.BlockSpec((tm, tn), lambda n, i, te, na: (i, n)),
        scratch_shapes=[pltpu.VMEM((ff, tn), _BF16)],
    )
    return pl.pallas_call(
        _moe_down_kernel,
        grid_spec=down_spec,
        out_shape=jax.ShapeDtypeStruct((rows, d), _F32),
        compiler_params=_params("arbitrary", "arbitrary"),
        name="moe_down",
    )(tile_expert, n_active, act, wd)


def _combine_kernel(pos_ref, info_ref, h_ref, y_ref, g_ref, out_ref, buf, sem, *, tc, final_norm):
    def start(t, carry):
        for k in range(_TOP_K):
            pltpu.make_async_copy(y_ref.at[pl.ds(pos_ref[_TOP_K * t + k], 1)],
                                  buf.at[k, pl.ds(t, 1)], sem).start()
        return carry

    lax.fori_loop(0, tc, start, 0, unroll=_DMA_ISSUE_UNROLL)
    for k in range(_TOP_K):
        pltpu.make_async_copy(y_ref.at[pl.ds(0, tc)], buf.at[k], sem).wait()

    info = info_ref[...]
    w1 = info[:, _R_W1:_R_W1 + 1]
    w2 = info[:, _R_W2:_R_W2 + 1]
    out = h_ref[...] + (w1 * buf[0] + w2 * buf[1])
    if final_norm:
        out = out * _rms_scale(out) * g_ref[...]
    out_ref[...] = out


def _combine(pos, info, h, y, final_g, final_norm):
    n, d = h.shape
    tc = _largest_tile(n, 256, _SUBLANES)
    kern = functools.partial(_combine_kernel, tc=tc, final_norm=final_norm)
    return pl.pallas_call(
        kern,
        grid=(n // tc,),
        in_specs=[
            pl.BlockSpec((_TOP_K * tc,), lambda i: (i,), memory_space=pltpu.SMEM),
            pl.BlockSpec((tc, _LANES), lambda i: (i, 0)),
            pl.BlockSpec((tc, d), lambda i: (i, 0)),
            pl.BlockSpec(memory_space=pl.ANY),
            pl.BlockSpec((1, d), lambda i: (0, 0)),
        ],
        out_specs=pl.BlockSpec((tc, d), lambda i: (i, 0)),
        out_shape=jax.ShapeDtypeStruct((n, d), _F32),
        scratch_shapes=[pltpu.VMEM((_TOP_K, tc, d), _F32), pltpu.SemaphoreType.DMA(())],
        compiler_params=_params("arbitrary"),
        name="moe_combine",
    )(pos, info, h, y, final_g)


def _final_norm_kernel(h_ref, g_ref, out_ref):
    x = h_ref[...]
    out_ref[...] = x * _rms_scale(x) * g_ref[...]


def _final_norm(h, g):
    n, d = h.shape
    tm = _largest_tile(n, 512, _SUBLANES)
    return pl.pallas_call(
        _final_norm_kernel,
        grid=(n // tm,),
        in_specs=[pl.BlockSpec((tm, d), lambda i: (i, 0)), pl.BlockSpec((1, d), lambda i: (0, 0))],
        out_specs=pl.BlockSpec((tm, d), lambda i: (i, 0)),
        out_shape=jax.ShapeDtypeStruct((n, d), _F32),
        compiler_params=_params("parallel"),
        name="final_norm",
    )(h, g)


_MOE_ROW_TILE = 512


def _moe(h, g2, wr, br, wg, wu, wd, layer, final_g, final_norm):
    n, d = h.shape
    n_exp = wr.shape[1]
    tm = _largest_tile(n, _MOE_ROW_TILE, _SUBLANES)
    wr_pad = jnp.zeros((d, _LANES), _F32).at[:, :n_exp].set(wr)
    br_pad = jnp.zeros((1, _LANES), _F32).at[0, :n_exp].set(br)
    xp, info, cnt = _router(h, g2, wr_pad, br_pad, n_exp)

    counts = cnt[0, :n_exp].astype(jnp.int32)
    tiles_per = (counts + tm - 1) // tm
    tile_end = jnp.cumsum(tiles_per)
    starts = (tile_end - tiles_per) * tm
    n_active = tile_end[-1:]
    n_tiles = (_TOP_K * n) // tm + n_exp
    tile_ids = jnp.minimum(jnp.arange(n_tiles, dtype=jnp.int32), n_active - 1)
    tile_expert = jnp.sum((tile_ids[:, None] >= tile_end[None, :]).astype(jnp.int32), axis=1)
    tile_expert = tile_expert + layer * n_exp
    experts = info[:, _R_E1:_R_E2 + 1].astype(jnp.int32)
    ranks = info[:, _R_RANK1:_R_RANK2 + 1].astype(jnp.int32)
    onehot = experts[:, :, None] == jnp.arange(n_exp, dtype=jnp.int32)[None, None, :]
    pos = (jnp.sum(jnp.where(onehot, starts[None, None, :], 0), axis=-1) + ranks).reshape(-1)

    xs = _dispatch(pos, xp, n_tiles * tm)
    y = _moe_ffn(tile_expert, n_active, xs, wg, wu, wd, tm)
    return _combine(pos, info, h, y, final_g, final_norm)


def _rope_tables(seq, hd):
    axis_dim = hd // 2
    nfreq = axis_dim // 2
    rows = seq // _GRID_W
    row = jnp.broadcast_to(jnp.arange(rows)[:, None], (rows, _GRID_W)).reshape(seq).astype(_F32)
    col = jnp.broadcast_to(jnp.arange(_GRID_W)[None, :], (rows, _GRID_W)).reshape(seq).astype(_F32)
    inv = _ROPE_THETA ** (-jnp.arange(nfreq, dtype=_F32) * 2.0 / axis_dim)
    ar = row[:, None] * inv
    ac = col[:, None] * inv
    ang = jnp.concatenate([ar, ar, ac, ac], axis=-1)
    lane = jnp.arange(hd)
    sign = jnp.where((lane % axis_dim) < nfreq, -1.0, 1.0).astype(_F32)
    return jnp.cos(ang), jnp.sin(ang) * sign


def kernel(x, norm1_g, w_in, q_norm_g, k_norm_g, w_attn_o, w_pool_mix, pool_scale, w_pool_o,
           w_out, norm2_g, ffn_w_gate, ffn_w_up, ffn_w_down, router_w, router_b,
           moe_w_gate, moe_w_up, moe_w_down, final_g):
    batch, seq, d = x.shape
    depth = w_in.shape[0]
    hd = q_norm_g.shape[1]
    q_dim = w_attn_o.shape[1]
    p_dim = pool_scale.shape[1]
    kv_dim = (w_in.shape[2] - q_dim - p_dim - 2 * d) // 2
    dims = (q_dim, kv_dim, p_dim, hd, seq)
    assert seq % _GRID_W == 0 and q_dim % kv_dim == 0 and kv_dim % hd == 0

    cos, sin_signed = _rope_tables(seq, hd)
    bf = lambda w: w.astype(_BF16)
    row = lambda v: v.reshape(1, -1)
    experts = lambda w: w.reshape((-1,) + w.shape[2:])

    w_in, w_attn_o, w_pool_o, w_out = bf(w_in), bf(w_attn_o), bf(w_pool_o), bf(w_out)
    ffn_w_gate, ffn_w_up, ffn_w_down = bf(ffn_w_gate), bf(ffn_w_up), bf(ffn_w_down)
    moe_w_gate, moe_w_up, moe_w_down = experts(moe_w_gate), experts(moe_w_up), experts(moe_w_down)

    h = x.reshape(batch * seq, d)
    fg = row(final_g)
    for l in range(depth):
        q, k, vt, u, gates = _in_proj(h, row(norm1_g[l]), w_in, l, cos, sin_signed,
                                      row(q_norm_g[l]), row(k_norm_g[l]), dims)
        o = _attention(q, k, vt, dims, batch)
        pooled = _pool(u, bf(w_pool_mix[l]), row(pool_scale[l]), seq)
        merged = _merge(o, pooled, w_attn_o, w_pool_o, l, gates)
        h = _out_proj(merged, w_out, l, h)
        i = l // 2
        if l % 2 == 0:
            h = _ffn(h, row(norm2_g[l]), ffn_w_gate, ffn_w_up, ffn_w_down, i)
        else:
            last = l == depth - 1
            h = _moe(h, row(norm2_g[l]), router_w[i], router_b[i], moe_w_gate, moe_w_up,
                     moe_w_down, i, fg, last)
    if depth % 2 == 1 or depth == 0:
        h = _final_norm(h, fg)
    return h.reshape(batch, seq, d)
```

```python
import functools

import jax
import jax.numpy as jnp
from jax import lax
from jax.experimental import pallas as pl
from jax.experimental.pallas import tpu as pltpu

_GRID_W = 64
_ROPE_THETA = 10000.0
_POOL_WINDOWS = (2, 4, 8, 16)
_TOP_K = 2
_NORM_EPS = 1e-6
_LOG2_E = 1.4426950408889634

_LANES = 128
_SUBLANES = 8
_SUM_ROWS = 16
_V7X_VMEM_LIMIT_BYTES = 56 * 1024 * 1024

_F32 = jnp.float32
_BF16 = jnp.bfloat16


def _params(*semantics):
    return pltpu.CompilerParams(dimension_semantics=semantics,
                                vmem_limit_bytes=_V7X_VMEM_LIMIT_BYTES)


def _largest_tile(n, cap, quantum):
    t = min(cap, n)
    t -= t % quantum
    while t > quantum and n % t:
        t -= quantum
    assert t >= quantum and n % t == 0, (n, cap, quantum)
    return t


def _rms_scale(x):
    return lax.rsqrt(jnp.mean(x * x, axis=-1, keepdims=True) + _NORM_EPS)


_ROW_PARTS = 4


def _in_proj_kernel(h_ref, g_ref, w_ref, cos_ref, sin_ref, qg_ref, kg_ref,
                    q_ref, k_ref, vt_ref, u_ref, gate_ref, xn_sc, acc_sc,
                    *, nq, nk, nv, nu, ng, hd, tk, qscale):
    j = pl.program_id(1)
    ncol = nq + nk + nv + nu + ng
    tn = w_ref.shape[1]
    heads = tn // hd

    tm = xn_sc.shape[0]

    def matmul_rows(slot, rows):
        acc_sc[slot, rows, :] = jnp.dot(xn_sc[rows, :], w_ref[...], preferred_element_type=_F32)

    def norm_rope(acc, rows, gain, scale):
        cos = cos_ref[rows, :]
        sin = sin_ref[rows, :]
        lane = lax.broadcasted_iota(jnp.int32, (1, hd), 1)
        first_half = (lane % (hd // 2)) < (hd // 4)
        outs = []
        for hh in range(heads):
            xh = acc[:, hh * hd:(hh + 1) * hd]
            xh = xh * _rms_scale(xh) * gain
            partner = jnp.where(first_half,
                                pltpu.roll(xh, hd - hd // 4, 1),
                                pltpu.roll(xh, hd // 4, 1))
            outs.append(((xh * cos + partner * sin) * scale).astype(_BF16))
        return outs

    def finish_q(acc, rows):
        q_ref[rows, :] = jnp.concatenate(norm_rope(acc, rows, qg_ref[...], qscale), axis=1)

    def finish_k(acc, rows):
        for hh, kh in enumerate(norm_rope(acc, rows, kg_ref[...], 1.0)):
            k_ref[hh, rows, :] = kh

    def finish_v(acc, rows):
        sum_rows = (lax.broadcasted_iota(jnp.int32, (_SUM_ROWS, tk), 0) == 0).astype(_BF16)
        c = rows.start // tk
        for hh in range(heads):
            vt_ref[hh, c, 0:hd, :] = acc[:, hh * hd:(hh + 1) * hd].T.astype(_BF16)
            vt_ref[hh, c, hd:hd + _SUM_ROWS, :] = sum_rows

    def finish_u(acc, rows):
        u_ref[rows, :] = acc

    def finish_gate(acc, rows):
        gate_ref[rows, :] = (0.5 * jnp.tanh(0.5 * acc) + 0.5).astype(_BF16)

    def row_parts(parts):
        return [slice(r * (tm // parts), (r + 1) * (tm // parts)) for r in range(parts)]

    @pl.when(j == 0)
    def _():
        for rows in row_parts(_ROW_PARTS):
            x = h_ref[rows, :]
            xn_sc[rows, :] = (x * _rms_scale(x) * g_ref[...]).astype(_BF16)
            matmul_rows(0, rows)

    t = j - 1
    lo = 0
    for cnt, finish, parts in ((nq, finish_q, 1), (nk, finish_k, 1),
                               (nv, finish_v, tm // tk), (nu, finish_u, _ROW_PARTS),
                               (ng, finish_gate, _ROW_PARTS)):
        @pl.when((t >= lo) & (t < lo + cnt) & (j < ncol))
        def _(finish=finish, parts=parts):
            for rows in row_parts(parts):
                finish(acc_sc[t % 2, rows, :], rows)
                matmul_rows(j % 2, rows)
        lo += cnt

    @pl.when(j == ncol)
    def _():
        finish_gate(acc_sc[t % 2], slice(0, tm))


def _attn_tiles(seq):
    tq = _largest_tile(seq, 256, _LANES)
    tk = _largest_tile(seq // 2, 512, _LANES)
    assert (seq // tk) % 2 == 0
    return tq, tk


def _in_proj(h, g1, w, layer, cos, sin_signed, qg, kg, dims):
    n, d = h.shape
    q_dim, kv_dim, p_dim, hd, seq = dims
    _, tk = _attn_tiles(seq)
    cols = w.shape[2]
    tm = _largest_tile(seq, 1024, _SUBLANES)
    tn = _largest_tile(_gcd_all(q_dim, kv_dim, p_dim, d), 512, hd)
    nq, nk, nv, nu, ng = q_dim // tn, kv_dim // tn, kv_dim // tn, p_dim // tn, 2 * d // tn
    ncol = nq + nk + nv + nu + ng
    assert ncol * tn == cols and tm % tk == 0
    s_tiles = seq // tm
    heads = tn // hd

    def tile(j, lo, cnt):
        return jnp.clip(j - 1 - lo, 0, cnt - 1)

    kern = functools.partial(_in_proj_kernel, nq=nq, nk=nk, nv=nv, nu=nu, ng=ng, hd=hd, tk=tk,
                             qscale=_LOG2_E * float(hd) ** -0.5)
    return pl.pallas_call(
        kern,
        grid=(n // tm, ncol + 1),
        in_specs=[
            pl.BlockSpec((tm, d), lambda i, j: (i, 0)),
            pl.BlockSpec((1, d), lambda i, j: (0, 0)),
            pl.BlockSpec((None, d, tn), lambda i, j: (layer, 0, jnp.minimum(j, ncol - 1))),
            pl.BlockSpec((tm, hd), lambda i, j: (i % s_tiles, 0)),
            pl.BlockSpec((tm, hd), lambda i, j: (i % s_tiles, 0)),
            pl.BlockSpec((1, hd), lambda i, j: (0, 0)),
            pl.BlockSpec((1, hd), lambda i, j: (0, 0)),
        ],
        out_specs=[
            pl.BlockSpec((tm, tn), lambda i, j: (i, tile(j, 0, nq))),
            pl.BlockSpec((heads, tm, hd), lambda i, j: (tile(j, nq, nk), i, 0)),
            pl.BlockSpec((heads, tm // tk, hd + _SUM_ROWS, tk),
                         lambda i, j: (tile(j, nq + nk, nv), i, 0, 0)),
            pl.BlockSpec((tm, tn), lambda i, j: (i, tile(j, nq + nk + nv, nu))),
            pl.BlockSpec((tm, tn), lambda i, j: (i, tile(j, nq + nk + nv + nu, ng))),
        ],
        out_shape=[
            jax.ShapeDtypeStruct((n, q_dim), _BF16),
            jax.ShapeDtypeStruct((kv_dim // hd, n, hd), _BF16),
            jax.ShapeDtypeStruct((kv_dim // hd, n // tk, hd + _SUM_ROWS, tk), _BF16),
            jax.ShapeDtypeStruct((n, p_dim), _F32),
            jax.ShapeDtypeStruct((n, 2 * d), _BF16),
        ],
        scratch_shapes=[pltpu.VMEM((tm, d), _BF16), pltpu.VMEM((2, tm, tn), _F32)],
        compiler_params=_params("parallel", "arbitrary"),
        name="in_proj",
    )(h, g1, w, cos, sin_signed, qg, kg)


def _gcd_all(*vals):
    import math
    g = 0
    for v in vals:
        g = math.gcd(g, v)
    return g


def _attn_kernel(q_ref, k_ref, vt_ref, o_ref, qs_sc, s_sc, cm_sc, acc_sc,
                 *, groups, hd, tq, tk, seq):
    n_chunks = seq // tk
    cols = groups * tq

    for g in range(groups):
        qs_sc[:, g * tq:(g + 1) * tq] = q_ref[:, g * hd:(g + 1) * hd].astype(_F32).T.astype(_BF16)
    acc_sc[...] = jnp.zeros(acc_sc.shape, _F32)

    def scores(c, slot):
        k_c = k_ref[pl.ds(pl.multiple_of(c * tk, tk), tk), :]
        s = jnp.dot(k_c, qs_sc[...], preferred_element_type=_F32)
        s_sc[slot] = s
        cm_sc[slot] = jnp.max(s, axis=0, keepdims=True)

    def softmax_values(c, slot, m_prev):
        m_new = jnp.maximum(m_prev, cm_sc[slot])
        alpha = jnp.exp2(m_prev - m_new)
        p = jnp.exp2(s_sc[slot] - m_new)
        acc_sc[...] = acc_sc[...] * alpha + jnp.dot(vt_ref[c], p.astype(_BF16),
                                                    preferred_element_type=_F32)
        return m_new

    def pair(j, m):
        c = 2 * j
        scores(c + 1, 1)
        m = softmax_values(c, 0, m)
        scores(c + 2, 0)
        return softmax_values(c + 1, 1, m)

    m = jnp.full((1, cols), -jnp.inf, _F32)
    scores(0, 0)
    m = lax.fori_loop(0, n_chunks // 2 - 1, pair, m)
    scores(n_chunks - 1, 1)
    m = softmax_values(n_chunks - 2, 0, m)
    m = softmax_values(n_chunks - 1, 1, m)

    out_t = acc_sc[0:hd, :] / acc_sc[hd:hd + 1, :]
    for g in range(groups):
        o_ref[:, g * hd:(g + 1) * hd] = out_t[:, g * tq:(g + 1) * tq].T.astype(o_ref.dtype)


def _attention(q, k, vt, dims, batch):
    q_dim, kv_dim, _, hd, seq = dims
    n_kv = kv_dim // hd
    groups = q_dim // kv_dim
    tq, tk = _attn_tiles(seq)
    q3 = q.reshape(batch, seq, q_dim)
    cols = groups * tq
    kern = functools.partial(_attn_kernel, groups=groups, hd=hd, tq=tq, tk=tk, seq=seq)
    o = pl.pallas_call(
        kern,
        grid=(batch, n_kv, seq // tq),
        in_specs=[
            pl.BlockSpec((None, tq, groups * hd), lambda b, kh, i: (b, i, kh)),
            pl.BlockSpec((None, seq, hd), lambda b, kh, i: (kh, b, 0)),
            pl.BlockSpec((None, seq // tk, hd + _SUM_ROWS, tk), lambda b, kh, i: (kh, b, 0, 0)),
        ],
        out_specs=pl.BlockSpec((None, tq, groups * hd), lambda b, kh, i: (b, i, kh)),
        out_shape=jax.ShapeDtypeStruct((batch, seq, q_dim), _BF16),
        scratch_shapes=[
            pltpu.VMEM((hd, cols), _BF16),
            pltpu.VMEM((2, tk, cols), _F32),
            pltpu.VMEM((2, 1, cols), _F32),
            pltpu.VMEM((hd + _SUM_ROWS, cols), _F32),
        ],
        compiler_params=_params("parallel", "parallel", "parallel"),
        name="attention",
    )(q3, k, vt)
    return o.reshape(batch * seq, q_dim)


_POOL_HALO = 8


def _pool_kernel(prev_ref, u_ref, next_ref, wmix_ref, scale_ref, o_ref, ext_sc, run_sc,
                 *, tm, seq, grp):
    halo = _POOL_HALO
    tile = pl.program_id(0) % (seq // tm)
    ext_sc[0:halo, :] = jnp.where(tile == 0, 0.0, prev_ref[...])
    ext_sc[halo:halo + tm, :] = u_ref[...]
    ext_sc[halo + tm:halo + tm + halo, :] = jnp.where(tile == seq // tm - 1, 0.0, next_ref[...])
    t = tile * tm + lax.broadcasted_iota(jnp.int32, (tm, 1), 0)
    for g, w in enumerate(_POOL_WINDOWS):
        sl = slice(g * grp, (g + 1) * grp)
        run = ext_sc[:, sl]
        span = 1
        while span < w:
            run = run + pltpu.roll(run, span, 0)
            span *= 2
        run_sc[...] = run
        end = halo + w // 2 - 1
        wsum = run_sc[end:end + tm, :]
        lo = jnp.maximum(t - w // 2, 0)
        hi = jnp.minimum(t + w // 2 - 1, seq - 1)
        cnt = (hi - lo + 1).astype(_F32)
        dev = wsum / cnt - u_ref[:, sl]
        y = jnp.dot(dev.astype(_BF16), wmix_ref[g], preferred_element_type=_F32)
        o_ref[:, sl] = (y * scale_ref[:, sl]).astype(o_ref.dtype)


def _pool(u, wmix, scale, seq):
    n, p = u.shape
    grp = p // len(_POOL_WINDOWS)
    tm = _largest_tile(seq, 512, _POOL_HALO)
    halo = _POOL_HALO
    hb = tm // halo
    kern = functools.partial(_pool_kernel, tm=tm, seq=seq, grp=grp)
    return pl.pallas_call(
        kern,
        grid=(n // tm,),
        in_specs=[
            pl.BlockSpec((halo, p), lambda i: (jnp.maximum(i * hb - 1, 0), 0)),
            pl.BlockSpec((tm, p), lambda i: (i, 0)),
            pl.BlockSpec((halo, p), lambda i: (jnp.minimum((i + 1) * hb, n // halo - 1), 0)),
            pl.BlockSpec(wmix.shape, lambda i: (0, 0, 0)),
            pl.BlockSpec((1, p), lambda i: (0, 0)),
        ],
        out_specs=pl.BlockSpec((tm, p), lambda i: (i, 0)),
        out_shape=jax.ShapeDtypeStruct((n, p), _BF16),
        scratch_shapes=[pltpu.VMEM((tm + 2 * halo, p), _F32),
                        pltpu.VMEM((tm + 2 * halo, grp), _F32)],
        compiler_params=_params("parallel"),
        name="pool",
    )(u, u, u, wmix, scale)


def _merge_kernel(o_ref, p_ref, wa_ref, wp_ref, ga_ref, gp_ref, out_ref):
    a = jnp.dot(o_ref[...], wa_ref[...], preferred_element_type=_F32)
    p = jnp.dot(p_ref[...], wp_ref[...], preferred_element_type=_F32)
    merged = ga_ref[...].astype(_F32) * a + gp_ref[...].astype(_F32) * p
    out_ref[...] = merged.astype(out_ref.dtype)


def _merge(o, pooled, wa, wp, layer, gates):
    n, q_dim = o.shape
    p_dim = pooled.shape[1]
    d = wa.shape[2]
    tm = _largest_tile(n, 1024, _SUBLANES)
    tn = _largest_tile(d, 1024, _LANES)
    nj = d // tn
    return pl.pallas_call(
        _merge_kernel,
        grid=(n // tm, nj),
        in_specs=[
            pl.BlockSpec((tm, q_dim), lambda i, j: (i, 0)),
            pl.BlockSpec((tm, p_dim), lambda i, j: (i, 0)),
            pl.BlockSpec((None, q_dim, tn), lambda i, j: (layer, 0, j)),
            pl.BlockSpec((None, p_dim, tn), lambda i, j: (layer, 0, j)),
            pl.BlockSpec((tm, tn), lambda i, j: (i, j)),
            pl.BlockSpec((tm, tn), lambda i, j: (i, nj + j)),
        ],
        out_specs=pl.BlockSpec((tm, tn), lambda i, j: (i, j)),
        out_shape=jax.ShapeDtypeStruct((n, d), _BF16),
        compiler_params=_params("parallel", "arbitrary"),
        name="merge",
    )(o, pooled, wa, wp, gates, gates)


def _out_proj_kernel(a_ref, w_ref, res_ref, out_ref):
    out_ref[...] = res_ref[...] + jnp.dot(a_ref[...], w_ref[...], preferred_element_type=_F32)


def _out_proj(merged, w, layer, h):
    n, d = h.shape
    tm = _largest_tile(n, 1024, _SUBLANES)
    tn = _largest_tile(d, 1024, _LANES)
    return pl.pallas_call(
        _out_proj_kernel,
        grid=(n // tm, d // tn),
        in_specs=[
            pl.BlockSpec((tm, merged.shape[1]), lambda i, j: (i, 0)),
            pl.BlockSpec((None, merged.shape[1], tn), lambda i, j: (layer, 0, j)),
            pl.BlockSpec((tm, tn), lambda i, j: (i, j)),
        ],
        out_specs=pl.BlockSpec((tm, tn), lambda i, j: (i, j)),
        out_shape=jax.ShapeDtypeStruct((n, d), _F32),
        compiler_params=_params("parallel", "arbitrary"),
        name="out_proj",
    )(merged, w, h)


def _swiglu_step(xn, wg_ref, wu_ref, wd_ref):
    a = jnp.dot(xn, wg_ref[...], preferred_element_type=_F32)
    b = jnp.dot(xn, wu_ref[...], preferred_element_type=_F32)
    act = (a * jax.nn.sigmoid(a) * b).astype(_BF16)
    return jnp.dot(act, wd_ref[...], preferred_element_type=_F32)


def _ffn_kernel(h_ref, g_ref, wg_ref, wu_ref, wd_ref, out_ref, xn_sc):
    @pl.when(pl.program_id(1) == 0)
    def _():
        x = h_ref[...]
        xn_sc[...] = (x * _rms_scale(x) * g_ref[...]).astype(_BF16)
        out_ref[...] = x

    out_ref[...] += _swiglu_step(xn_sc[...], wg_ref, wu_ref, wd_ref)


def _ffn(h, g2, wg, wu, wd, layer):
    n, d = h.shape
    ff = wg.shape[2]
    tm = _largest_tile(n, 512, _SUBLANES)
    tf = _largest_tile(ff, 512, _LANES)
    return pl.pallas_call(
        _ffn_kernel,
        grid=(n // tm, ff // tf),
        in_specs=[
            pl.BlockSpec((tm, d), lambda i, f: (i, 0)),
            pl.BlockSpec((1, d), lambda i, f: (0, 0)),
            pl.BlockSpec((None, d, tf), lambda i, f: (layer, 0, f)),
            pl.BlockSpec((None, d, tf), lambda i, f: (layer, 0, f)),
            pl.BlockSpec((None, tf, d), lambda i, f: (layer, f, 0)),
        ],
        out_specs=pl.BlockSpec((tm, d), lambda i, f: (i, 0)),
        out_shape=jax.ShapeDtypeStruct((n, d), _F32),
        scratch_shapes=[pltpu.VMEM((tm, d), _BF16)],
        compiler_params=_params("parallel", "arbitrary"),
        name="ffn_dense",
    )(h, g2, wg, wu, wd)


_R_E1, _R_E2, _R_RANK1, _R_RANK2, _R_W1, _R_W2 = range(6)


def _router_kernel(h_ref, g_ref, wr_ref, br_ref, xp_ref, info_ref, cnt_ref, carry_sc, *, n_exp):
    @pl.when(pl.program_id(0) == 0)
    def _():
        carry_sc[...] = jnp.zeros(carry_sc.shape, _F32)

    x = h_ref[...]
    xn = x * _rms_scale(x) * g_ref[...]
    tm, d = xn.shape

    xb = xn.astype(_BF16).astype(_F32)
    lo = lax.bitcast_convert_type(xb[:, :d // 2], jnp.int32)
    hi = lax.bitcast_convert_type(xb[:, d // 2:], jnp.int32)
    xp_ref[...] = lax.shift_right_logical(lo, 16) | hi

    logits = jnp.dot(xn, wr_ref[...], preferred_element_type=_F32,
                     precision=lax.Precision.HIGHEST) + br_ref[...]
    lane = lax.broadcasted_iota(jnp.int32, logits.shape, 1)
    logits = jnp.where(lane < n_exp, logits, -jnp.inf)
    v1 = jnp.max(logits, axis=-1, keepdims=True)
    i1 = jnp.min(jnp.where(logits == v1, lane, _LANES), axis=-1, keepdims=True)
    pick1 = lane == i1
    rest = jnp.where(pick1, -jnp.inf, logits)
    v2 = jnp.max(rest, axis=-1, keepdims=True)
    i2 = jnp.min(jnp.where(rest == v2, lane, _LANES), axis=-1, keepdims=True)
    pick2 = lane == i2
    e = jnp.exp(v2 - v1)
    w1 = 1.0 / (1.0 + e)
    w2 = e / (1.0 + e)

    cnt = (pick1 | pick2).astype(_F32)
    row = lax.broadcasted_iota(jnp.int32, (tm, tm), 0)
    col = lax.broadcasted_iota(jnp.int32, (tm, tm), 1)
    below = (row > col).astype(_BF16)
    before = jnp.dot(below, cnt.astype(_BF16), preferred_element_type=_F32) + carry_sc[...]
    rank1 = jnp.sum(jnp.where(pick1, before, 0.0), axis=-1, keepdims=True)
    rank2 = jnp.sum(jnp.where(pick2, before, 0.0), axis=-1, keepdims=True)
    carry_sc[...] += jnp.sum(cnt, axis=0, keepdims=True)

    info = jnp.zeros(logits.shape, _F32)
    for slot, val in ((_R_E1, i1.astype(_F32)), (_R_E2, i2.astype(_F32)), (_R_RANK1, rank1),
                      (_R_RANK2, rank2), (_R_W1, w1), (_R_W2, w2)):
        info = jnp.where(lane == slot, val, info)
    info_ref[...] = info
    cnt_ref[...] = jnp.broadcast_to(carry_sc[...], cnt_ref.shape)


def _router(h, g2, wr_pad, br_pad, n_exp):
    n, d = h.shape
    tm = _largest_tile(n, 512, _SUBLANES)
    kern = functools.partial(_router_kernel, n_exp=n_exp)
    return pl.pallas_call(
        kern,
        grid=(n // tm,),
        in_specs=[
            pl.BlockSpec((tm, d), lambda i: (i, 0)),
            pl.BlockSpec((1, d), lambda i: (0, 0)),
            pl.BlockSpec((d, _LANES), lambda i: (0, 0)),
            pl.BlockSpec((1, _LANES), lambda i: (0, 0)),
        ],
        out_specs=[
            pl.BlockSpec((tm, d // 2), lambda i: (i, 0)),
            pl.BlockSpec((tm, _LANES), lambda i: (i, 0)),
            pl.BlockSpec((_SUBLANES, _LANES), lambda i: (0, 0)),
        ],
        out_shape=[
            jax.ShapeDtypeStruct((n, d // 2), jnp.int32),
            jax.ShapeDtypeStruct((n, _LANES), _F32),
            jax.ShapeDtypeStruct((_SUBLANES, _LANES), _F32),
        ],
        scratch_shapes=[pltpu.VMEM((1, _LANES), _F32)],
        compiler_params=_params("arbitrary"),
        name="router",
    )(h, g2, wr_pad, br_pad)


_DMA_ISSUE_UNROLL = 8


def _row_copy(src_ref, src_row, dst_ref, dst_row, sem):
    return pltpu.make_async_copy(src_ref.at[pl.ds(src_row, 1)], dst_ref.at[pl.ds(dst_row, 1)], sem)


def _dispatch_kernel(pos_ref, xp_ref, zeros_ref, xs_ref, sem, *, tc):
    del zeros_ref

    def start(t, carry):
        for k in range(_TOP_K):
            _row_copy(xp_ref, t, xs_ref, pos_ref[_TOP_K * t + k], sem).start()
        return carry

    lax.fori_loop(0, tc, start, 0, unroll=_DMA_ISSUE_UNROLL)
    for _ in range(_TOP_K):
        pltpu.make_async_copy(xp_ref, xs_ref.at[pl.ds(0, tc)], sem).wait()


def _dispatch(pos, xp, rows):
    n, half = xp.shape
    tc = _largest_tile(n, 512, _SUBLANES)
    kern = functools.partial(_dispatch_kernel, tc=tc)
    return pl.pallas_call(
        kern,
        grid=(n // tc,),
        in_specs=[
            pl.BlockSpec((_TOP_K * tc,), lambda i: (i,), memory_space=pltpu.SMEM),
            pl.BlockSpec((tc, half), lambda i: (i, 0)),
            pl.BlockSpec(memory_space=pl.ANY),
        ],
        out_specs=pl.BlockSpec(memory_space=pl.ANY),
        out_shape=jax.ShapeDtypeStruct((rows, half), jnp.int32),
        scratch_shapes=[pltpu.SemaphoreType.DMA(())],
        input_output_aliases={2: 0},
        compiler_params=_params("arbitrary"),
        name="moe_dispatch",
    )(pos, xp, jnp.zeros((rows, half), jnp.int32))


def _expert_changed(te_ref, i):
    return (i == 0) | (te_ref[i] != te_ref[jnp.maximum(i - 1, 0)])


def _moe_up_kernel(te_ref, nact_ref, xs_ref, wg_ref, wu_ref, act_ref, wg_sc, wu_sc):
    i = pl.program_id(1)
    active = i < nact_ref[0]

    @pl.when(active & _expert_changed(te_ref, i))
    def _():
        wg_sc[...] = wg_ref[...].astype(_BF16)
        wu_sc[...] = wu_ref[...].astype(_BF16)

    @pl.when(active)
    def _():
        packed = xs_ref[...]
        lo = lax.bitcast_convert_type(lax.shift_left(packed, 16), _F32)
        hi = lax.bitcast_convert_type(packed & jnp.int32(-65536), _F32)
        x = jnp.concatenate([lo, hi], axis=1).astype(_BF16)
        a = jnp.dot(x, wg_sc[...], preferred_element_type=_F32)
        b = jnp.dot(x, wu_sc[...], preferred_element_type=_F32)
        act_ref[...] = (a * jax.nn.sigmoid(a) * b).astype(act_ref.dtype)

    @pl.when(jnp.logical_not(active))
    def _():
        act_ref[...] = jnp.zeros(act_ref.shape, act_ref.dtype)


def _moe_down_kernel(te_ref, nact_ref, act_ref, wd_ref, y_ref, wd_sc):
    i = pl.program_id(1)
    active = i < nact_ref[0]

    @pl.when(active & _expert_changed(te_ref, i))
    def _():
        wd_sc[...] = wd_ref[...].astype(_BF16)

    @pl.when(active)
    def _():
        y_ref[...] = jnp.dot(act_ref[...], wd_sc[...], preferred_element_type=_F32)

    @pl.when(jnp.logical_not(active))
    def _():
        y_ref[...] = jnp.zeros(y_ref.shape, y_ref.dtype)


def _moe_ffn(tile_expert, n_active, xs, wg, wu, wd, tm):
    rows, half = xs.shape
    d = 2 * half
    ff = wg.shape[2]
    tf = _largest_tile(ff, 512, _LANES)
    tn = _largest_tile(d, 512, _LANES)
    n_tiles = rows // tm

    def row_idx(i, nact):
        return jnp.maximum(jnp.minimum(i, nact[0] - 1), 0)

    up_spec = pltpu.PrefetchScalarGridSpec(
        num_scalar_prefetch=2,
        grid=(ff // tf, n_tiles),
        in_specs=[
            pl.BlockSpec((tm, half), lambda f, i, te, na: (row_idx(i, na), 0)),
            pl.BlockSpec((None, d, tf), lambda f, i, te, na: (te[i], 0, f)),
            pl.BlockSpec((None, d, tf), lambda f, i, te, na: (te[i], 0, f)),
        ],
        out_specs=pl.BlockSpec((tm, tf), lambda f, i, te, na: (i, f)),
        scratch_shapes=[pltpu.VMEM((d, tf), _BF16), pltpu.VMEM((d, tf), _BF16)],
    )
    act = pl.pallas_call(
        _moe_up_kernel,
        grid_spec=up_spec,
        out_shape=jax.ShapeDtypeStruct((rows, ff), _BF16),
        compiler_params=_params("arbitrary", "arbitrary"),
        name="moe_up",
    )(tile_expert, n_active, xs, wg, wu)

    down_spec = pltpu.PrefetchScalarGridSpec(
        num_scalar_prefetch=2,
        grid=(d // tn, n_tiles),
        in_specs=[
            pl.BlockSpec((tm, ff), lambda n, i, te, na: (row_idx(i, na), 0)),
            pl.BlockSpec((None, ff, tn), lambda n, i, te, na: (te[i], 0, n)),
        ],
        out_specs=pl.BlockSpec((tm, tn), lambda n, i, te, na: (i, n)),
        scratch_shapes=[pltpu.VMEM((ff, tn), _BF16)],
    )
    return pl.pallas_call(
        _moe_down_kernel,
        grid_spec=down_spec,
        out_shape=jax.ShapeDtypeStruct((rows, d), _F32),
        compiler_params=_params("arbitrary", "arbitrary"),
        name="moe_down",
    )(tile_expert, n_active, act, wd)


def _combine_kernel(pos_ref, info_ref, h_ref, y_ref, g_ref, out_ref, buf, sem, *, tc, final_norm):
    def start(t, carry):
        for k in range(_TOP_K):
            pltpu.make_async_copy(y_ref.at[pl.ds(pos_ref[_TOP_K * t + k], 1)],
                                  buf.at[k, pl.ds(t, 1)], sem).start()
        return carry

    lax.fori_loop(0, tc, start, 0, unroll=_DMA_ISSUE_UNROLL)
    for k in range(_TOP_K):
        pltpu.make_async_copy(y_ref.at[pl.ds(0, tc)], buf.at[k], sem).wait()

    info = info_ref[...]
    w1 = info[:, _R_W1:_R_W1 + 1]
    w2 = info[:, _R_W2:_R_W2 + 1]
    out = h_ref[...] + (w1 * buf[0] + w2 * buf[1])
    if final_norm:
        out = out * _rms_scale(out) * g_ref[...]
    out_ref[...] = out


def _combine(pos, info, h, y, final_g, final_norm):
    n, d = h.shape
    tc = _largest_tile(n, 256, _SUBLANES)
    kern = functools.partial(_combine_kernel, tc=tc, final_norm=final_norm)
    return pl.pallas_call(
        kern,
        grid=(n // tc,),
        in_specs=[
            pl.BlockSpec((_TOP_K * tc,), lambda i: (i,), memory_space=pltpu.SMEM),
            pl.BlockSpec((tc, _LANES), lambda i: (i, 0)),
            pl.BlockSpec((tc, d), lambda i: (i, 0)),
            pl.BlockSpec(memory_space=pl.ANY),
            pl.BlockSpec((1, d), lambda i: (0, 0)),
        ],
        out_specs=pl.BlockSpec((tc, d), lambda i: (i, 0)),
        out_shape=jax.ShapeDtypeStruct((n, d), _F32),
        scratch_shapes=[pltpu.VMEM((_TOP_K, tc, d), _F32), pltpu.SemaphoreType.DMA(())],
        compiler_params=_params("arbitrary"),
        name="moe_combine",
    )(pos, info, h, y, final_g)


def _final_norm_kernel(h_ref, g_ref, out_ref):
    x = h_ref[...]
    out_ref[...] = x * _rms_scale(x) * g_ref[...]


def _final_norm(h, g):
    n, d = h.shape
    tm = _largest_tile(n, 512, _SUBLANES)
    return pl.pallas_call(
        _final_norm_kernel,
        grid=(n // tm,),
        in_specs=[pl.BlockSpec((tm, d), lambda i: (i, 0)), pl.BlockSpec((1, d), lambda i: (0, 0))],
        out_specs=pl.BlockSpec((tm, d), lambda i: (i, 0)),
        out_shape=jax.ShapeDtypeStruct((n, d), _F32),
        compiler_params=_params("parallel"),
        name="final_norm",
    )(h, g)


_MOE_ROW_TILE = 512


def _moe(h, g2, wr, br, wg, wu, wd, layer, final_g, final_norm):
    n, d = h.shape
    n_exp = wr.shape[1]
    tm = _largest_tile(n, _MOE_ROW_TILE, _SUBLANES)
    wr_pad = jnp.zeros((d, _LANES), _F32).at[:, :n_exp].set(wr)
    br_pad = jnp.zeros((1, _LANES), _F32).at[0, :n_exp].set(br)
    xp, info, cnt = _router(h, g2, wr_pad, br_pad, n_exp)

    counts = cnt[0, :n_exp].astype(jnp.int32)
    tiles_per = (counts + tm - 1) // tm
    tile_end = jnp.cumsum(tiles_per)
    starts = (tile_end - tiles_per) * tm
    n_active = tile_end[-1:]
    n_tiles = (_TOP_K * n) // tm + n_exp
    tile_ids = jnp.minimum(jnp.arange(n_tiles, dtype=jnp.int32), n_active - 1)
    tile_expert = jnp.sum((tile_ids[:, None] >= tile_end[None, :]).astype(jnp.int32), axis=1)
    tile_expert = tile_expert + layer * n_exp
    experts = info[:, _R_E1:_R_E2 + 1].astype(jnp.int32)
    ranks = info[:, _R_RANK1:_R_RANK2 + 1].astype(jnp.int32)
    onehot = experts[:, :, None] == jnp.arange(n_exp, dtype=jnp.int32)[None, None, :]
    pos = (jnp.sum(jnp.where(onehot, starts[None, None, :], 0), axis=-1) + ranks).reshape(-1)

    xs = _dispatch(pos, xp, n_tiles * tm)
    y = _moe_ffn(tile_expert, n_active, xs, wg, wu, wd, tm)
    return _combine(pos, info, h, y, final_g, final_norm)


def _rope_tables(seq, hd):
    axis_dim = hd // 2
    nfreq = axis_dim // 2
    rows = seq // _GRID_W
    row = jnp.broadcast_to(jnp.arange(rows)[:, None], (rows, _GRID_W)).reshape(seq).astype(_F32)
    col = jnp.broadcast_to(jnp.arange(_GRID_W)[None, :], (rows, _GRID_W)).reshape(seq).astype(_F32)
    inv = _ROPE_THETA ** (-jnp.arange(nfreq, dtype=_F32) * 2.0 / axis_dim)
    ar = row[:, None] * inv
    ac = col[:, None] * inv
    ang = jnp.concatenate([ar, ar, ac, ac], axis=-1)
    lane = jnp.arange(hd)
    sign = jnp.where((lane % axis_dim) < nfreq, -1.0, 1.0).astype(_F32)
    return jnp.cos(ang), jnp.sin(ang) * sign


def kernel(x, norm1_g, w_in, q_norm_g, k_norm_g, w_attn_o, w_pool_mix, pool_scale, w_pool_o,
           w_out, norm2_g, ffn_w_gate, ffn_w_up, ffn_w_down, router_w, router_b,
           moe_w_gate, moe_w_up, moe_w_down, final_g):
    batch, seq, d = x.shape
    depth = w_in.shape[0]
    hd = q_norm_g.shape[1]
    q_dim = w_attn_o.shape[1]
    p_dim = pool_scale.shape[1]
    kv_dim = (w_in.shape[2] - q_dim - p_dim - 2 * d) // 2
    dims = (q_dim, kv_dim, p_dim, hd, seq)
    assert seq % _GRID_W == 0 and q_dim % kv_dim == 0 and kv_dim % hd == 0

    cos, sin_signed = _rope_tables(seq, hd)
    bf = lambda w: w.astype(_BF16)
    row = lambda v: v.reshape(1, -1)
    experts = lambda w: w.reshape((-1,) + w.shape[2:])

    w_in, w_attn_o, w_pool_o, w_out = bf(w_in), bf(w_attn_o), bf(w_pool_o), bf(w_out)
    ffn_w_gate, ffn_w_up, ffn_w_down = bf(ffn_w_gate), bf(ffn_w_up), bf(ffn_w_down)
    moe_w_gate, moe_w_up, moe_w_down = experts(moe_w_gate), experts(moe_w_up), experts(moe_w_down)

    h = x.reshape(batch * seq, d)
    fg = row(final_g)
    for l in range(depth):
        q, k, vt, u, gates = _in_proj(h, row(norm1_g[l]), w_in, l, cos, sin_signed,
                                      row(q_norm_g[l]), row(k_norm_g[l]), dims)
        o = _attention(q, k, vt, dims, batch)
        pooled = _pool(u, bf(w_pool_mix[l]), row(pool_scale[l]), seq)
        merged = _merge(o, pooled, w_attn_o, w_pool_o, l, gates)
        h = _out_proj(merged, w_out, l, h)
        i = l // 2
        if l % 2 == 0:
            h = _ffn(h, row(norm2_g[l]), ffn_w_gate, ffn_w_up, ffn_w_down, i)
        else:
            last = l == depth - 1
            h = _moe(h, row(norm2_g[l]), router_w[i], router_b[i], moe_w_gate, moe_w_up,
                     moe_w_down, i, fg, last)
    if depth % 2 == 1 or depth == 0:
        h = _final_norm(h, fg)
    return h.reshape(batch, seq, d)
```

```python
import functools

import jax
import jax.numpy as jnp
from jax import lax
from jax.experimental import pallas as pl
from jax.experimental.pallas import tpu as pltpu

_GRID_W = 64
_ROPE_THETA = 10000.0
_POOL_WINDOWS = (2, 4, 8, 16)
_TOP_K = 2
_NORM_EPS = 1e-6
_LOG2_E = 1.4426950408889634

_LANES = 128
_SUBLANES = 8
_SUM_ROWS = 16
_V7X_VMEM_LIMIT_BYTES = 56 * 1024 * 1024

_F32 = jnp.float32
_BF16 = jnp.bfloat16


def _params(*semantics):
    return pltpu.CompilerParams(dimension_semantics=semantics,
                                vmem_limit_bytes=_V7X_VMEM_LIMIT_BYTES)


def _largest_tile(n, cap, quantum):
    t = min(cap, n)
    t -= t % quantum
    while t > quantum and n % t:
        t -= quantum
    assert t >= quantum and n % t == 0, (n, cap, quantum)
    return t


def _rms_scale(x):
    return lax.rsqrt(jnp.mean(x * x, axis=-1, keepdims=True) + _NORM_EPS)


_ROW_PARTS = 4


def _in_proj_kernel(h_ref, g_ref, w_ref, cos_ref, sin_ref, qg_ref, kg_ref,
                    q_ref, k_ref, vt_ref, u_ref, gate_ref, xn_sc, acc_sc,
                    *, nq, nk, nv, nu, ng, hd, tk, qscale):
    j = pl.program_id(1)
    ncol = nq + nk + nv + nu + ng
    tn = w_ref.shape[1]
    heads = tn // hd

    tm = xn_sc.shape[0]

    def matmul_rows(slot, rows):
        acc_sc[slot, rows, :] = jnp.dot(xn_sc[rows, :], w_ref[...], preferred_element_type=_F32)

    def norm_rope(acc, rows, gain, scale):
        cos = cos_ref[rows, :]
        sin = sin_ref[rows, :]
        lane = lax.broadcasted_iota(jnp.int32, (1, hd), 1)
        first_half = (lane % (hd // 2)) < (hd // 4)
        outs = []
        for hh in range(heads):
            xh = acc[:, hh * hd:(hh + 1) * hd]
            xh = xh * _rms_scale(xh) * gain
            partner = jnp.where(first_half,
                                pltpu.roll(xh, hd - hd // 4, 1),
                                pltpu.roll(xh, hd // 4, 1))
            outs.append(((xh * cos + partner * sin) * scale).astype(_BF16))
        return outs

    def finish_q(acc, rows):
        q_ref[rows, :] = jnp.concatenate(norm_rope(acc, rows, qg_ref[...], qscale), axis=1)

    def finish_k(acc, rows):
        for hh, kh in enumerate(norm_rope(acc, rows, kg_ref[...], 1.0)):
            k_ref[hh, rows, :] = kh

    def finish_v(acc, rows):
        sum_rows = (lax.broadcasted_iota(jnp.int32, (_SUM_ROWS, tk), 0) == 0).astype(_BF16)
        c = rows.start // tk
        for hh in range(heads):
            vt_ref[hh, c, 0:hd, :] = acc[:, hh * hd:(hh + 1) * hd].T.astype(_BF16)
            vt_ref[hh, c, hd:hd + _SUM_ROWS, :] = sum_rows

    def finish_u(acc, rows):
        u_ref[rows, :] = acc

    def finish_gate(acc, rows):
        gate_ref[rows, :] = (0.5 * jnp.tanh(0.5 * acc) + 0.5).astype(_BF16)

    def row_parts(parts):
        return [slice(r * (tm // parts), (r + 1) * (tm // parts)) for r in range(parts)]

    @pl.when(j == 0)
    def _():
        for rows in row_parts(_ROW_PARTS):
            x = h_ref[rows, :]
            xn_sc[rows, :] = (x * _rms_scale(x) * g_ref[...]).astype(_BF16)
            matmul_rows(0, rows)

    t = j - 1
    lo = 0
    for cnt, finish, parts in ((nq, finish_q, 1), (nk, finish_k, 1),
                               (nv, finish_v, tm // tk), (nu, finish_u, _ROW_PARTS),
                               (ng, finish_gate, _ROW_PARTS)):
        @pl.when((t >= lo) & (t < lo + cnt) & (j < ncol))
        def _(finish=finish, parts=parts):
            for rows in row_parts(parts):
                finish(acc_sc[t % 2, rows, :], rows)
                matmul_rows(j % 2, rows)
        lo += cnt

    @pl.when(j == ncol)
    def _():
        finish_gate(acc_sc[t % 2], slice(0, tm))


def _attn_tiles(seq):
    tq = _largest_tile(seq, 512, _LANES)
    tk = _largest_tile(seq // 2, 512, _LANES)
    assert (seq // tk) % 2 == 0
    return tq, tk


def _in_proj(h, g1, w, layer, cos, sin_signed, qg, kg, dims):
    n, d = h.shape
    q_dim, kv_dim, p_dim, hd, seq = dims
    _, tk = _attn_tiles(seq)
    cols = w.shape[2]
    tm = _largest_tile(seq, 1024, _SUBLANES)
    tn = _largest_tile(_gcd_all(q_dim, kv_dim, p_dim, d), 512, hd)
    nq, nk, nv, nu, ng = q_dim // tn, kv_dim // tn, kv_dim // tn, p_dim // tn, 2 * d // tn
    ncol = nq + nk + nv + nu + ng
    assert ncol * tn == cols and tm % tk == 0
    s_tiles = seq // tm
    heads = tn // hd

    def tile(j, lo, cnt):
        return jnp.clip(j - 1 - lo, 0, cnt - 1)

    kern = functools.partial(_in_proj_kernel, nq=nq, nk=nk, nv=nv, nu=nu, ng=ng, hd=hd, tk=tk,
                             qscale=_LOG2_E * float(hd) ** -0.5)
    return pl.pallas_call(
        kern,
        grid=(n // tm, ncol + 1),
        in_specs=[
            pl.BlockSpec((tm, d), lambda i, j: (i, 0)),
            pl.BlockSpec((1, d), lambda i, j: (0, 0)),
            pl.BlockSpec((None, d, tn), lambda i, j: (layer, 0, jnp.minimum(j, ncol - 1))),
            pl.BlockSpec((tm, hd), lambda i, j: (i % s_tiles, 0)),
            pl.BlockSpec((tm, hd), lambda i, j: (i % s_tiles, 0)),
            pl.BlockSpec((1, hd), lambda i, j: (0, 0)),
            pl.BlockSpec((1, hd), lambda i, j: (0, 0)),
        ],
        out_specs=[
            pl.BlockSpec((tm, tn), lambda i, j: (i, tile(j, 0, nq))),
            pl.BlockSpec((heads, tm, hd), lambda i, j: (tile(j, nq, nk), i, 0)),
            pl.BlockSpec((heads, tm // tk, hd + _SUM_ROWS, tk),
                         lambda i, j: (tile(j, nq + nk, nv), i, 0, 0)),
            pl.BlockSpec((tm, tn), lambda i, j: (i, tile(j, nq + nk + nv, nu))),
            pl.BlockSpec((tm, tn), lambda i, j: (i, tile(j, nq + nk + nv + nu, ng))),
        ],
        out_shape=[
            jax.ShapeDtypeStruct((n, q_dim), _BF16),
            jax.ShapeDtypeStruct((kv_dim // hd, n, hd), _BF16),
            jax.ShapeDtypeStruct((kv_dim // hd, n // tk, hd + _SUM_ROWS, tk), _BF16),
            jax.ShapeDtypeStruct((n, p_dim), _F32),
            jax.ShapeDtypeStruct((n, 2 * d), _BF16),
        ],
        scratch_shapes=[pltpu.VMEM((tm, d), _BF16), pltpu.VMEM((2, tm, tn), _F32)],
        compiler_params=_params("parallel", "arbitrary"),
        name="in_proj",
    )(h, g1, w, cos, sin_signed, qg, kg)


def _gcd_all(*vals):
    import math
    g = 0
    for v in vals:
        g = math.gcd(g, v)
    return g


def _attn_kernel(q_ref, k_ref, vt_ref, o_ref, qs_sc, s_sc, cm_sc, acc_sc,
                 *, groups, hd, tq, tk, seq):
    n_chunks = seq // tk
    cols = groups * tq

    for g in range(groups):
        qs_sc[:, g * tq:(g + 1) * tq] = q_ref[:, g * hd:(g + 1) * hd].astype(_F32).T.astype(_BF16)
    acc_sc[...] = jnp.zeros(acc_sc.shape, _F32)

    def scores(c, slot):
        k_c = k_ref[pl.ds(pl.multiple_of(c * tk, tk), tk), :]
        s = jnp.dot(k_c, qs_sc[...], preferred_element_type=_F32)
        s_sc[slot] = s
        cm_sc[slot] = jnp.max(s, axis=0, keepdims=True)

    def softmax_values(c, slot, m_prev):
        m_new = jnp.maximum(m_prev, cm_sc[slot])
        alpha = jnp.exp2(m_prev - m_new)
        p = jnp.exp2(s_sc[slot] - m_new)
        acc_sc[...] = acc_sc[...] * alpha + jnp.dot(vt_ref[c], p.astype(_BF16),
                                                    preferred_element_type=_F32)
        return m_new

    def pair(j, m):
        c = 2 * j
        scores(c + 1, 1)
        m = softmax_values(c, 0, m)
        scores(c + 2, 0)
        return softmax_values(c + 1, 1, m)

    m = jnp.full((1, cols), -jnp.inf, _F32)
    scores(0, 0)
    m = lax.fori_loop(0, n_chunks // 2 - 1, pair, m)
    scores(n_chunks - 1, 1)
    m = softmax_values(n_chunks - 2, 0, m)
    m = softmax_values(n_chunks - 1, 1, m)

    out_t = acc_sc[0:hd, :] / acc_sc[hd:hd + 1, :]
    for g in range(groups):
        o_ref[:, g * hd:(g + 1) * hd] = out_t[:, g * tq:(g + 1) * tq].T.astype(o_ref.dtype)


def _attention(q, k, vt, dims, batch):
    q_dim, kv_dim, _, hd, seq = dims
    n_kv = kv_dim // hd
    groups = q_dim // kv_dim
    tq, tk = _attn_tiles(seq)
    q3 = q.reshape(batch, seq, q_dim)
    cols = groups * tq
    kern = functools.partial(_attn_kernel, groups=groups, hd=hd, tq=tq, tk=tk, seq=seq)
    o = pl.pallas_call(
        kern,
        grid=(batch, n_kv, seq // tq),
        in_specs=[
            pl.BlockSpec((None, tq, groups * hd), lambda b, kh, i: (b, i, kh)),
            pl.BlockSpec((None, seq, hd), lambda b, kh, i: (kh, b, 0)),
            pl.BlockSpec((None, seq // tk, hd + _SUM_ROWS, tk), lambda b, kh, i: (kh, b, 0, 0)),
        ],
        out_specs=pl.BlockSpec((None, tq, groups * hd), lambda b, kh, i: (b, i, kh)),
        out_shape=jax.ShapeDtypeStruct((batch, seq, q_dim), _BF16),
        scratch_shapes=[
            pltpu.VMEM((hd, cols), _BF16),
            pltpu.VMEM((2, tk, cols), _F32),
            pltpu.VMEM((2, 1, cols), _F32),
            pltpu.VMEM((hd + _SUM_ROWS, cols), _F32),
        ],
        compiler_params=_params("parallel", "parallel", "parallel"),
        name="attention",
    )(q3, k, vt)
    return o.reshape(batch * seq, q_dim)


_POOL_HALO = 8


def _pool_kernel(prev_ref, u_ref, next_ref, wmix_ref, scale_ref, o_ref, ext_sc, run_sc,
                 *, tm, seq, grp):
    halo = _POOL_HALO
    tile = pl.program_id(0) % (seq // tm)
    ext_sc[0:halo, :] = jnp.where(tile == 0, 0.0, prev_ref[...])
    ext_sc[halo:halo + tm, :] = u_ref[...]
    ext_sc[halo + tm:halo + tm + halo, :] = jnp.where(tile == seq // tm - 1, 0.0, next_ref[...])
    t = tile * tm + lax.broadcasted_iota(jnp.int32, (tm, 1), 0)
    for g, w in enumerate(_POOL_WINDOWS):
        sl = slice(g * grp, (g + 1) * grp)
        run = ext_sc[:, sl]
        span = 1
        while span < w:
            run = run + pltpu.roll(run, span, 0)
            span *= 2
        run_sc[...] = run
        end = halo + w // 2 - 1
        wsum = run_sc[end:end + tm, :]
        lo = jnp.maximum(t - w // 2, 0)
        hi = jnp.minimum(t + w // 2 - 1, seq - 1)
        cnt = (hi - lo + 1).astype(_F32)
        dev = wsum / cnt - u_ref[:, sl]
        y = jnp.dot(dev.astype(_BF16), wmix_ref[g], preferred_element_type=_F32)
        o_ref[:, sl] = (y * scale_ref[:, sl]).astype(o_ref.dtype)


def _pool(u, wmix, scale, seq):
    n, p = u.shape
    grp = p // len(_POOL_WINDOWS)
    tm = _largest_tile(seq, 512, _POOL_HALO)
    halo = _POOL_HALO
    hb = tm // halo
    kern = functools.partial(_pool_kernel, tm=tm, seq=seq, grp=grp)
    return pl.pallas_call(
        kern,
        grid=(n // tm,),
        in_specs=[
            pl.BlockSpec((halo, p), lambda i: (jnp.maximum(i * hb - 1, 0), 0)),
            pl.BlockSpec((tm, p), lambda i: (i, 0)),
            pl.BlockSpec((halo, p), lambda i: (jnp.minimum((i + 1) * hb, n // halo - 1), 0)),
            pl.BlockSpec(wmix.shape, lambda i: (0, 0, 0)),
            pl.BlockSpec((1, p), lambda i: (0, 0)),
        ],
        out_specs=pl.BlockSpec((tm, p), lambda i: (i, 0)),
        out_shape=jax.ShapeDtypeStruct((n, p), _BF16),
        scratch_shapes=[pltpu.VMEM((tm + 2 * halo, p), _F32),
                        pltpu.VMEM((tm + 2 * halo, grp), _F32)],
        compiler_params=_params("parallel"),
        name="pool",
    )(u, u, u, wmix, scale)


def _merge_kernel(o_ref, p_ref, wa_ref, wp_ref, ga_ref, gp_ref, out_ref):
    a = jnp.dot(o_ref[...], wa_ref[...], preferred_element_type=_F32)
    p = jnp.dot(p_ref[...], wp_ref[...], preferred_element_type=_F32)
    merged = ga_ref[...].astype(_F32) * a + gp_ref[...].astype(_F32) * p
    out_ref[...] = merged.astype(out_ref.dtype)


def _merge(o, pooled, wa, wp, layer, gates):
    n, q_dim = o.shape
    p_dim = pooled.shape[1]
    d = wa.shape[2]
    tm = _largest_tile(n, 1024, _SUBLANES)
    tn = _largest_tile(d, 1024, _LANES)
    nj = d // tn
    return pl.pallas_call(
        _merge_kernel,
        grid=(n // tm, nj),
        in_specs=[
            pl.BlockSpec((tm, q_dim), lambda i, j: (i, 0)),
            pl.BlockSpec((tm, p_dim), lambda i, j: (i, 0)),
            pl.BlockSpec((None, q_dim, tn), lambda i, j: (layer, 0, j)),
            pl.BlockSpec((None, p_dim, tn), lambda i, j: (layer, 0, j)),
            pl.BlockSpec((tm, tn), lambda i, j: (i, j)),
            pl.BlockSpec((tm, tn), lambda i, j: (i, nj + j)),
        ],
        out_specs=pl.BlockSpec((tm, tn), lambda i, j: (i, j)),
        out_shape=jax.ShapeDtypeStruct((n, d), _BF16),
        compiler_params=_params("parallel", "arbitrary"),
        name="merge",
    )(o, pooled, wa, wp, gates, gates)


def _out_proj_kernel(a_ref, w_ref, res_ref, out_ref):
    out_ref[...] = res_ref[...] + jnp.dot(a_ref[...], w_ref[...], preferred_element_type=_F32)


def _out_proj(merged, w, layer, h):
    n, d = h.shape
    tm = _largest_tile(n, 1024, _SUBLANES)
    tn = _largest_tile(d, 1024, _LANES)
    return pl.pallas_call(
        _out_proj_kernel,
        grid=(n // tm, d // tn),
        in_specs=[
            pl.BlockSpec((tm, merged.shape[1]), lambda i, j: (i, 0)),
            pl.BlockSpec((None, merged.shape[1], tn), lambda i, j: (layer, 0, j)),
            pl.BlockSpec((tm, tn), lambda i, j: (i, j)),
        ],
        out_specs=pl.BlockSpec((tm, tn), lambda i, j: (i, j)),
        out_shape=jax.ShapeDtypeStruct((n, d), _F32),
        compiler_params=_params("parallel", "arbitrary"),
        name="out_proj",
    )(merged, w, h)


def _swiglu_step(xn, wg_ref, wu_ref, wd_ref):
    a = jnp.dot(xn, wg_ref[...], preferred_element_type=_F32)
    b = jnp.dot(xn, wu_ref[...], preferred_element_type=_F32)
    act = (a * jax.nn.sigmoid(a) * b).astype(_BF16)
    return jnp.dot(act, wd_ref[...], preferred_element_type=_F32)


def _ffn_kernel(h_ref, g_ref, wg_ref, wu_ref, wd_ref, out_ref, xn_sc):
    @pl.when(pl.program_id(1) == 0)
    def _():
        x = h_ref[...]
        xn_sc[...] = (x * _rms_scale(x) * g_ref[...]).astype(_BF16)
        out_ref[...] = x

    out_ref[...] += _swiglu_step(xn_sc[...], wg_ref, wu_ref, wd_ref)


def _ffn(h, g2, wg, wu, wd, layer):
    n, d = h.shape
    ff = wg.shape[2]
    tm = _largest_tile(n, 512, _SUBLANES)
    tf = _largest_tile(ff, 512, _LANES)
    return pl.pallas_call(
        _ffn_kernel,
        grid=(n // tm, ff // tf),
        in_specs=[
            pl.BlockSpec((tm, d), lambda i, f: (i, 0)),
            pl.BlockSpec((1, d), lambda i, f: (0, 0)),
            pl.BlockSpec((None, d, tf), lambda i, f: (layer, 0, f)),
            pl.BlockSpec((None, d, tf), lambda i, f: (layer, 0, f)),
            pl.BlockSpec((None, tf, d), lambda i, f: (layer, f, 0)),
        ],
        out_specs=pl.BlockSpec((tm, d), lambda i, f: (i, 0)),
        out_shape=jax.ShapeDtypeStruct((n, d), _F32),
        scratch_shapes=[pltpu.VMEM((tm, d), _BF16)],
        compiler_params=_params("parallel", "arbitrary"),
        name="ffn_dense",
    )(h, g2, wg, wu, wd)


_R_E1, _R_E2, _R_RANK1, _R_RANK2, _R_W1, _R_W2 = range(6)


def _router_kernel(h_ref, g_ref, wr_ref, br_ref, xp_ref, info_ref, cnt_ref, carry_sc, *, n_exp):
    @pl.when(pl.program_id(0) == 0)
    def _():
        carry_sc[...] = jnp.zeros(carry_sc.shape, _F32)

    x = h_ref[...]
    xn = x * _rms_scale(x) * g_ref[...]
    tm, d = xn.shape

    xb = xn.astype(_BF16).astype(_F32)
    lo = lax.bitcast_convert_type(xb[:, :d // 2], jnp.int32)
    hi = lax.bitcast_convert_type(xb[:, d // 2:], jnp.int32)
    xp_ref[...] = lax.shift_right_logical(lo, 16) | hi

    logits = jnp.dot(xn, wr_ref[...], preferred_element_type=_F32,
                     precision=lax.Precision.HIGHEST) + br_ref[...]
    lane = lax.broadcasted_iota(jnp.int32, logits.shape, 1)
    logits = jnp.where(lane < n_exp, logits, -jnp.inf)
    v1 = jnp.max(logits, axis=-1, keepdims=True)
    i1 = jnp.min(jnp.where(logits == v1, lane, _LANES), axis=-1, keepdims=True)
    pick1 = lane == i1
    rest = jnp.where(pick1, -jnp.inf, logits)
    v2 = jnp.max(rest, axis=-1, keepdims=True)
    i2 = jnp.min(jnp.where(rest == v2, lane, _LANES), axis=-1, keepdims=True)
    pick2 = lane == i2
    e = jnp.exp(v2 - v1)
    w1 = 1.0 / (1.0 + e)
    w2 = e / (1.0 + e)

    cnt = (pick1 | pick2).astype(_F32)
    row = lax.broadcasted_iota(jnp.int32, (tm, tm), 0)
    col = lax.broadcasted_iota(jnp.int32, (tm, tm), 1)
    below = (row > col).astype(_BF16)
    before = jnp.dot(below, cnt.astype(_BF16), preferred_element_type=_F32) + carry_sc[...]
    rank1 = jnp.sum(jnp.where(pick1, before, 0.0), axis=-1, keepdims=True)
    rank2 = jnp.sum(jnp.where(pick2, before, 0.0), axis=-1, keepdims=True)
    carry_sc[...] += jnp.sum(cnt, axis=0, keepdims=True)

    info = jnp.zeros(logits.shape, _F32)
    for slot, val in ((_R_E1, i1.astype(_F32)), (_R_E2, i2.astype(_F32)), (_R_RANK1, rank1),
                      (_R_RANK2, rank2), (_R_W1, w1), (_R_W2, w2)):
        info = jnp.where(lane == slot, val, info)
    info_ref[...] = info
    cnt_ref[...] = jnp.broadcast_to(carry_sc[...], cnt_ref.shape)


def _router(h, g2, wr_pad, br_pad, n_exp):
    n, d = h.shape
    tm = _largest_tile(n, 512, _SUBLANES)
    kern = functools.partial(_router_kernel, n_exp=n_exp)
    return pl.pallas_call(
        kern,
        grid=(n // tm,),
        in_specs=[
            pl.BlockSpec((tm, d), lambda i: (i, 0)),
            pl.BlockSpec((1, d), lambda i: (0, 0)),
            pl.BlockSpec((d, _LANES), lambda i: (0, 0)),
            pl.BlockSpec((1, _LANES), lambda i: (0, 0)),
        ],
        out_specs=[
            pl.BlockSpec((tm, d // 2), lambda i: (i, 0)),
            pl.BlockSpec((tm, _LANES), lambda i: (i, 0)),
            pl.BlockSpec((_SUBLANES, _LANES), lambda i: (0, 0)),
        ],
        out_shape=[
            jax.ShapeDtypeStruct((n, d // 2), jnp.int32),
            jax.ShapeDtypeStruct((n, _LANES), _F32),
            jax.ShapeDtypeStruct((_SUBLANES, _LANES), _F32),
        ],
        scratch_shapes=[pltpu.VMEM((1, _LANES), _F32)],
        compiler_params=_params("arbitrary"),
        name="router",
    )(h, g2, wr_pad, br_pad)


_DMA_ISSUE_UNROLL = 8


def _row_copy(src_ref, src_row, dst_ref, dst_row, sem):
    return pltpu.make_async_copy(src_ref.at[pl.ds(src_row, 1)], dst_ref.at[pl.ds(dst_row, 1)], sem)


def _dispatch_kernel(pos_ref, xp_ref, zeros_ref, xs_ref, sem, *, tc):
    del zeros_ref

    def start(t, carry):
        for k in range(_TOP_K):
            _row_copy(xp_ref, t, xs_ref, pos_ref[_TOP_K * t + k], sem).start()
        return carry

    lax.fori_loop(0, tc, start, 0, unroll=_DMA_ISSUE_UNROLL)
    for _ in range(_TOP_K):
        pltpu.make_async_copy(xp_ref, xs_ref.at[pl.ds(0, tc)], sem).wait()


def _dispatch(pos, xp, rows):
    n, half = xp.shape
    tc = _largest_tile(n, 512, _SUBLANES)
    kern = functools.partial(_dispatch_kernel, tc=tc)
    return pl.pallas_call(
        kern,
        grid=(n // tc,),
        in_specs=[
            pl.BlockSpec((_TOP_K * tc,), lambda i: (i,), memory_space=pltpu.SMEM),
            pl.BlockSpec((tc, half), lambda i: (i, 0)),
            pl.BlockSpec(memory_space=pl.ANY),
        ],
        out_specs=pl.BlockSpec(memory_space=pl.ANY),
        out_shape=jax.ShapeDtypeStruct((rows, half), jnp.int32),
        scratch_shapes=[pltpu.SemaphoreType.DMA(())],
        input_output_aliases={2: 0},
        compiler_params=_params("arbitrary"),
        name="moe_dispatch",
    )(pos, xp, jnp.zeros((rows, half), jnp.int32))


def _expert_changed(te_ref, i):
    return (i == 0) | (te_ref[i] != te_ref[jnp.maximum(i - 1, 0)])


def _moe_up_kernel(te_ref, nact_ref, xs_ref, wg_ref, wu_ref, act_ref, wg_sc, wu_sc):
    i = pl.program_id(1)
    active = i < nact_ref[0]

    @pl.when(active & _expert_changed(te_ref, i))
    def _():
        wg_sc[...] = wg_ref[...].astype(_BF16)
        wu_sc[...] = wu_ref[...].astype(_BF16)

    @pl.when(active)
    def _():
        packed = xs_ref[...]
        lo = lax.bitcast_convert_type(lax.shift_left(packed, 16), _F32)
        hi = lax.bitcast_convert_type(packed & jnp.int32(-65536), _F32)
        x = jnp.concatenate([lo, hi], axis=1).astype(_BF16)
        a = jnp.dot(x, wg_sc[...], preferred_element_type=_F32)
        b = jnp.dot(x, wu_sc[...], preferred_element_type=_F32)
        act_ref[...] = (a * jax.nn.sigmoid(a) * b).astype(act_ref.dtype)

    @pl.when(jnp.logical_not(active))
    def _():
        act_ref[...] = jnp.zeros(act_ref.shape, act_ref.dtype)


def _moe_down_kernel(te_ref, nact_ref, act_ref, wd_ref, y_ref, wd_sc):
    i = pl.program_id(1)
    active = i < nact_ref[0]

    @pl.when(active & _expert_changed(te_ref, i))
    def _():
        wd_sc[...] = wd_ref[...].astype(_BF16)

    @pl.when(active)
    def _():
        y_ref[...] = jnp.dot(act_ref[...], wd_sc[...], preferred_element_type=_F32)

    @pl.when(jnp.logical_not(active))
    def _():
        y_ref[...] = jnp.zeros(y_ref.shape, y_ref.dtype)


def _moe_ffn(tile_expert, n_active, xs, wg, wu, wd, tm):
    rows, half = xs.shape
    d = 2 * half
    ff = wg.shape[2]
    tf = _largest_tile(ff, 512, _LANES)
    tn = _largest_tile(d, 512, _LANES)
    n_tiles = rows // tm

    def row_idx(i, nact):
        return jnp.maximum(jnp.minimum(i, nact[0] - 1), 0)

    up_spec = pltpu.PrefetchScalarGridSpec(
        num_scalar_prefetch=2,
        grid=(ff // tf, n_tiles),
        in_specs=[
            pl.BlockSpec((tm, half), lambda f, i, te, na: (row_idx(i, na), 0)),
            pl.BlockSpec((None, d, tf), lambda f, i, te, na: (te[i], 0, f)),
            pl.BlockSpec((None, d, tf), lambda f, i, te, na: (te[i], 0, f)),
        ],
        out_specs=pl.BlockSpec((tm, tf), lambda f, i, te, na: (i, f)),
        scratch_shapes=[pltpu.VMEM((d, tf), _BF16), pltpu.VMEM((d, tf), _BF16)],
    )
    act = pl.pallas_call(
        _moe_up_kernel,
        grid_spec=up_spec,
        out_shape=jax.ShapeDtypeStruct((rows, ff), _BF16),
        compiler_params=_params("arbitrary", "arbitrary"),
        name="moe_up",
    )(tile_expert, n_active, xs, wg, wu)

    down_spec = pltpu.PrefetchScalarGridSpec(
        num_scalar_prefetch=2,
        grid=(d // tn, n_tiles),
        in_specs=[
            pl.BlockSpec((tm, ff), lambda n, i, te, na: (row_idx(i, na), 0)),
            pl.BlockSpec((None, ff, tn), lambda n, i, te, na: (te[i], 0, n)),
        ],
        out_specs=pl.BlockSpec((tm, tn), lambda n, i, te, na: (i, n)),
        scratch_shapes=[pltpu.VMEM((ff, tn), _BF16)],
    )
    return pl.pallas_call(
        _moe_down_kernel,
        grid_spec=down_spec,
        out_shape=jax.ShapeDtypeStruct((rows, d), _F32),
        compiler_params=_params("arbitrary", "arbitrary"),
        name="moe_down",
    )(tile_expert, n_active, act, wd)


def _combine_kernel(pos_ref, info_ref, h_ref, y_ref, g_ref, out_ref, buf, sem, *, tc, final_norm):
    def start(t, carry):
        for k in range(_TOP_K):
            pltpu.make_async_copy(y_ref.at[pl.ds(pos_ref[_TOP_K * t + k], 1)],
                                  buf.at[k, pl.ds(t, 1)], sem).start()
        return carry

    lax.fori_loop(0, tc, start, 0, unroll=_DMA_ISSUE_UNROLL)
    for k in range(_TOP_K):
        pltpu.make_async_copy(y_ref.at[pl.ds(0, tc)], buf.at[k], sem).wait()

    info = info_ref[...]
    w1 = info[:, _R_W1:_R_W1 + 1]
    w2 = info[:, _R_W2:_R_W2 + 1]
    out = h_ref[...] + (w1 * buf[0] + w2 * buf[1])
    if final_norm:
        out = out * _rms_scale(out) * g_ref[...]
    out_ref[...] = out


def _combine(pos, info, h, y, final_g, final_norm):
    n, d = h.shape
    tc = _largest_tile(n, 256, _SUBLANES)
    kern = functools.partial(_combine_kernel, tc=tc, final_norm=final_norm)
    return pl.pallas_call(
        kern,
        grid=(n // tc,),
        in_specs=[
            pl.BlockSpec((_TOP_K * tc,), lambda i: (i,), memory_space=pltpu.SMEM),
            pl.BlockSpec((tc, _LANES), lambda i: (i, 0)),
            pl.BlockSpec((tc, d), lambda i: (i, 0)),
            pl.BlockSpec(memory_space=pl.ANY),
            pl.BlockSpec((1, d), lambda i: (0, 0)),
        ],
        out_specs=pl.BlockSpec((tc, d), lambda i: (i, 0)),
        out_shape=jax.ShapeDtypeStruct((n, d), _F32),
        scratch_shapes=[pltpu.VMEM((_TOP_K, tc, d), _F32), pltpu.SemaphoreType.DMA(())],
        compiler_params=_params("arbitrary"),
        name="moe_combine",
    )(pos, info, h, y, final_g)


def _final_norm_kernel(h_ref, g_ref, out_ref):
    x = h_ref[...]
    out_ref[...] = x * _rms_scale(x) * g_ref[...]


def _final_norm(h, g):
    n, d = h.shape
    tm = _largest_tile(n, 512, _SUBLANES)
    return pl.pallas_call(
        _final_norm_kernel,
        grid=(n // tm,),
        in_specs=[pl.BlockSpec((tm, d), lambda i: (i, 0)), pl.BlockSpec((1, d), lambda i: (0, 0))],
        out_specs=pl.BlockSpec((tm, d), lambda i: (i, 0)),
        out_shape=jax.ShapeDtypeStruct((n, d), _F32),
        compiler_params=_params("parallel"),
        name="final_norm",
    )(h, g)


_MOE_ROW_TILE = 512


def _moe(h, g2, wr, br, wg, wu, wd, layer, final_g, final_norm):
    n, d = h.shape
    n_exp = wr.shape[1]
    tm = _largest_tile(n, _MOE_ROW_TILE, _SUBLANES)
    wr_pad = jnp.zeros((d, _LANES), _F32).at[:, :n_exp].set(wr)
    br_pad = jnp.zeros((1, _LANES), _F32).at[0, :n_exp].set(br)
    xp, info, cnt = _router(h, g2, wr_pad, br_pad, n_exp)

    counts = cnt[0, :n_exp].astype(jnp.int32)
    tiles_per = (counts + tm - 1) // tm
    tile_end = jnp.cumsum(tiles_per)
    starts = (tile_end - tiles_per) * tm
    n_active = tile_end[-1:]
    n_tiles = (_TOP_K * n) // tm + n_exp
    tile_ids = jnp.minimum(jnp.arange(n_tiles, dtype=jnp.int32), n_active - 1)
    tile_expert = jnp.sum((tile_ids[:, None] >= tile_end[None, :]).astype(jnp.int32), axis=1)
    tile_expert = tile_expert + layer * n_exp
    experts = info[:, _R_E1:_R_E2 + 1].astype(jnp.int32)
    ranks = info[:, _R_RANK1:_R_RANK2 + 1].astype(jnp.int32)
    onehot = experts[:, :, None] == jnp.arange(n_exp, dtype=jnp.int32)[None, None, :]
    pos = (jnp.sum(jnp.where(onehot, starts[None, None, :], 0), axis=-1) + ranks).reshape(-1)

    xs = _dispatch(pos, xp, n_tiles * tm)
    y = _moe_ffn(tile_expert, n_active, xs, wg, wu, wd, tm)
    return _combine(pos, info, h, y, final_g, final_norm)


def _rope_tables(seq, hd):
    axis_dim = hd // 2
    nfreq = axis_dim // 2
    rows = seq // _GRID_W
    row = jnp.broadcast_to(jnp.arange(rows)[:, None], (rows, _GRID_W)).reshape(seq).astype(_F32)
    col = jnp.broadcast_to(jnp.arange(_GRID_W)[None, :], (rows, _GRID_W)).reshape(seq).astype(_F32)
    inv = _ROPE_THETA ** (-jnp.arange(nfreq, dtype=_F32) * 2.0 / axis_dim)
    ar = row[:, None] * inv
    ac = col[:, None] * inv
    ang = jnp.concatenate([ar, ar, ac, ac], axis=-1)
    lane = jnp.arange(hd)
    sign = jnp.where((lane % axis_dim) < nfreq, -1.0, 1.0).astype(_F32)
    return jnp.cos(ang), jnp.sin(ang) * sign


def kernel(x, norm1_g, w_in, q_norm_g, k_norm_g, w_attn_o, w_pool_mix, pool_scale, w_pool_o,
           w_out, norm2_g, ffn_w_gate, ffn_w_up, ffn_w_down, router_w, router_b,
           moe_w_gate, moe_w_up, moe_w_down, final_g):
    batch, seq, d = x.shape
    depth = w_in.shape[0]
    hd = q_norm_g.shape[1]
    q_dim = w_attn_o.shape[1]
    p_dim = pool_scale.shape[1]
    kv_dim = (w_in.shape[2] - q_dim - p_dim - 2 * d) // 2
    dims = (q_dim, kv_dim, p_dim, hd, seq)
    assert seq % _GRID_W == 0 and q_dim % kv_dim == 0 and kv_dim % hd == 0

    cos, sin_signed = _rope_tables(seq, hd)
    bf = lambda w: w.astype(_BF16)
    row = lambda v: v.reshape(1, -1)
    experts = lambda w: w.reshape((-1,) + w.shape[2:])

    w_in, w_attn_o, w_pool_o, w_out = bf(w_in), bf(w_attn_o), bf(w_pool_o), bf(w_out)
    ffn_w_gate, ffn_w_up, ffn_w_down = bf(ffn_w_gate), bf(ffn_w_up), bf(ffn_w_down)
    moe_w_gate, moe_w_up, moe_w_down = experts(moe_w_gate), experts(moe_w_up), experts(moe_w_down)

    h = x.reshape(batch * seq, d)
    fg = row(final_g)
    for l in range(depth):
        q, k, vt, u, gates = _in_proj(h, row(norm1_g[l]), w_in, l, cos, sin_signed,
                                      row(q_norm_g[l]), row(k_norm_g[l]), dims)
        o = _attention(q, k, vt, dims, batch)
        pooled = _pool(u, bf(w_pool_mix[l]), row(pool_scale[l]), seq)
        merged = _merge(o, pooled, w_attn_o, w_pool_o, l, gates)
        h = _out_proj(merged, w_out, l, h)
        i = l // 2
        if l % 2 == 0:
            h = _ffn(h, row(norm2_g[l]), ffn_w_gate, ffn_w_up, ffn_w_down, i)
        else:
            last = l == depth - 1
            h = _moe(h, row(norm2_g[l]), router_w[i], router_b[i], moe_w_gate, moe_w_up,
                     moe_w_down, i, fg, last)
    if depth % 2 == 1 or depth == 0:
        h = _final_norm(h, fg)
    return h.reshape(batch, seq, d)
```

```python
import functools

import jax
import jax.numpy as jnp
from jax import lax
from jax.experimental import pallas as pl
from jax.experimental.pallas import tpu as pltpu

_GRID_W = 64
_ROPE_THETA = 10000.0
_POOL_WINDOWS = (2, 4, 8, 16)
_TOP_K = 2
_NORM_EPS = 1e-6
_LOG2_E = 1.4426950408889634

_LANES = 128
_SUBLANES = 8
_SUM_ROWS = 16
_V7X_VMEM_LIMIT_BYTES = 56 * 1024 * 1024

_F32 = jnp.float32
_BF16 = jnp.bfloat16


def _params(*semantics):
    return pltpu.CompilerParams(dimension_semantics=semantics,
                                vmem_limit_bytes=_V7X_VMEM_LIMIT_BYTES)


def _largest_tile(n, cap, quantum):
    t = min(cap, n)
    t -= t % quantum
    while t > quantum and n % t:
        t -= quantum
    assert t >= quantum and n % t == 0, (n, cap, quantum)
    return t


def _rms_scale(x):
    return lax.rsqrt(jnp.mean(x * x, axis=-1, keepdims=True) + _NORM_EPS)


_ROW_PARTS = 4


def _in_proj_kernel(h_ref, g_ref, w_ref, cos_ref, sin_ref, qg_ref, kg_ref,
                    q_ref, k_ref, vt_ref, u_ref, gate_ref, xn_sc, acc_sc,
                    *, nq, nk, nv, nu, ng, hd, tk, qscale):
    j = pl.program_id(1)
    ncol = nq + nk + nv + nu + ng
    tn = w_ref.shape[1]
    heads = tn // hd

    tm = xn_sc.shape[0]

    def matmul_rows(slot, rows):
        acc_sc[slot, rows, :] = jnp.dot(xn_sc[rows, :], w_ref[...], preferred_element_type=_F32)

    def norm_rope(acc, rows, gain, scale):
        cos = cos_ref[rows, :]
        sin = sin_ref[rows, :]
        lane = lax.broadcasted_iota(jnp.int32, (1, hd), 1)
        first_half = (lane % (hd // 2)) < (hd // 4)
        outs = []
        for hh in range(heads):
            xh = acc[:, hh * hd:(hh + 1) * hd]
            xh = xh * _rms_scale(xh) * gain
            partner = jnp.where(first_half,
                                pltpu.roll(xh, hd - hd // 4, 1),
                                pltpu.roll(xh, hd // 4, 1))
            outs.append(((xh * cos + partner * sin) * scale).astype(_BF16))
        return outs

    def finish_q(acc, rows):
        q_ref[rows, :] = jnp.concatenate(norm_rope(acc, rows, qg_ref[...], qscale), axis=1)

    def finish_k(acc, rows):
        for hh, kh in enumerate(norm_rope(acc, rows, kg_ref[...], 1.0)):
            k_ref[hh, rows, :] = kh

    def finish_v(acc, rows):
        sum_rows = (lax.broadcasted_iota(jnp.int32, (_SUM_ROWS, tk), 0) == 0).astype(_BF16)
        c = rows.start // tk
        for hh in range(heads):
            vt_ref[hh, c, 0:hd, :] = acc[:, hh * hd:(hh + 1) * hd].T.astype(_BF16)
            vt_ref[hh, c, hd:hd + _SUM_ROWS, :] = sum_rows

    def finish_u(acc, rows):
        u_ref[rows, :] = acc

    def finish_gate(acc, rows):
        gate_ref[rows, :] = (0.5 * jnp.tanh(0.5 * acc) + 0.5).astype(_BF16)

    def row_parts(parts):
        return [slice(r * (tm // parts), (r + 1) * (tm // parts)) for r in range(parts)]

    @pl.when(j == 0)
    def _():
        for rows in row_parts(_ROW_PARTS):
            x = h_ref[rows, :]
            xn_sc[rows, :] = (x * _rms_scale(x) * g_ref[...]).astype(_BF16)
            matmul_rows(0, rows)

    t = j - 1
    lo = 0
    for cnt, finish, parts in ((nq, finish_q, 1), (nk, finish_k, 1),
                               (nv, finish_v, tm // tk), (nu, finish_u, _ROW_PARTS),
                               (ng, finish_gate, _ROW_PARTS)):
        @pl.when((t >= lo) & (t < lo + cnt) & (j < ncol))
        def _(finish=finish, parts=parts):
            for rows in row_parts(parts):
                finish(acc_sc[t % 2, rows, :], rows)
                matmul_rows(j % 2, rows)
        lo += cnt

    @pl.when(j == ncol)
    def _():
        finish_gate(acc_sc[t % 2], slice(0, tm))


def _attn_tiles(seq):
    tq = _largest_tile(seq, 1024, _LANES)
    tk = _largest_tile(seq // 2, 512, _LANES)
    assert (seq // tk) % 2 == 0
    return tq, tk


def _in_proj(h, g1, w, layer, cos, sin_signed, qg, kg, dims):
    n, d = h.shape
    q_dim, kv_dim, p_dim, hd, seq = dims
    _, tk = _attn_tiles(seq)
    cols = w.shape[2]
    tm = _largest_tile(seq, 1024, _SUBLANES)
    tn = _largest_tile(_gcd_all(q_dim, kv_dim, p_dim, d), 512, hd)
    nq, nk, nv, nu, ng = q_dim // tn, kv_dim // tn, kv_dim // tn, p_dim // tn, 2 * d // tn
    ncol = nq + nk + nv + nu + ng
    assert ncol * tn == cols and tm % tk == 0
    s_tiles = seq // tm
    heads = tn // hd

    def tile(j, lo, cnt):
        return jnp.clip(j - 1 - lo, 0, cnt - 1)

    kern = functools.partial(_in_proj_kernel, nq=nq, nk=nk, nv=nv, nu=nu, ng=ng, hd=hd, tk=tk,
                             qscale=_LOG2_E * float(hd) ** -0.5)
    return pl.pallas_call(
        kern,
        grid=(n // tm, ncol + 1),
        in_specs=[
            pl.BlockSpec((tm, d), lambda i, j: (i, 0)),
            pl.BlockSpec((1, d), lambda i, j: (0, 0)),
            pl.BlockSpec((None, d, tn), lambda i, j: (layer, 0, jnp.minimum(j, ncol - 1))),
            pl.BlockSpec((tm, hd), lambda i, j: (i % s_tiles, 0)),
            pl.BlockSpec((tm, hd), lambda i, j: (i % s_tiles, 0)),
            pl.BlockSpec((1, hd), lambda i, j: (0, 0)),
            pl.BlockSpec((1, hd), lambda i, j: (0, 0)),
        ],
        out_specs=[
            pl.BlockSpec((tm, tn), lambda i, j: (i, tile(j, 0, nq))),
            pl.BlockSpec((heads, tm, hd), lambda i, j: (tile(j, nq, nk), i, 0)),
            pl.BlockSpec((heads, tm // tk, hd + _SUM_ROWS, tk),
                         lambda i, j: (tile(j, nq + nk, nv), i, 0, 0)),
            pl.BlockSpec((tm, tn), lambda i, j: (i, tile(j, nq + nk + nv, nu))),
            pl.BlockSpec((tm, tn), lambda i, j: (i, tile(j, nq + nk + nv + nu, ng))),
        ],
        out_shape=[
            jax.ShapeDtypeStruct((n, q_dim), _BF16),
            jax.ShapeDtypeStruct((kv_dim // hd, n, hd), _BF16),
            jax.ShapeDtypeStruct((kv_dim // hd, n // tk, hd + _SUM_ROWS, tk), _BF16),
            jax.ShapeDtypeStruct((n, p_dim), _F32),
            jax.ShapeDtypeStruct((n, 2 * d), _BF16),
        ],
        scratch_shapes=[pltpu.VMEM((tm, d), _BF16), pltpu.VMEM((2, tm, tn), _F32)],
        compiler_params=_params("parallel", "arbitrary"),
        name="in_proj",
    )(h, g1, w, cos, sin_signed, qg, kg)


def _gcd_all(*vals):
    import math
    g = 0
    for v in vals:
        g = math.gcd(g, v)
    return g


def _attn_kernel(q_ref, k_ref, vt_ref, o_ref, qs_sc, s_sc, cm_sc, acc_sc,
                 *, groups, hd, tq, tk, seq):
    n_chunks = seq // tk
    cols = groups * tq

    for g in range(groups):
        qs_sc[:, g * tq:(g + 1) * tq] = q_ref[:, g * hd:(g + 1) * hd].astype(_F32).T.astype(_BF16)
    acc_sc[...] = jnp.zeros(acc_sc.shape, _F32)

    def scores(c, slot):
        k_c = k_ref[pl.ds(pl.multiple_of(c * tk, tk), tk), :]
        s = jnp.dot(k_c, qs_sc[...], preferred_element_type=_F32)
        s_sc[slot] = s
        cm_sc[slot] = jnp.max(s, axis=0, keepdims=True)

    def softmax_values(c, slot, m_prev):
        m_new = jnp.maximum(m_prev, cm_sc[slot])
        alpha = jnp.exp2(m_prev - m_new)
        p = jnp.exp2(s_sc[slot] - m_new)
        acc_sc[...] = acc_sc[...] * alpha + jnp.dot(vt_ref[c], p.astype(_BF16),
                                                    preferred_element_type=_F32)
        return m_new

    def pair(j, m):
        c = 2 * j
        scores(c + 1, 1)
        m = softmax_values(c, 0, m)
        scores(c + 2, 0)
        return softmax_values(c + 1, 1, m)

    m = jnp.full((1, cols), -jnp.inf, _F32)
    scores(0, 0)
    m = lax.fori_loop(0, n_chunks // 2 - 1, pair, m)
    scores(n_chunks - 1, 1)
    m = softmax_values(n_chunks - 2, 0, m)
    m = softmax_values(n_chunks - 1, 1, m)

    out_t = acc_sc[0:hd, :] / acc_sc[hd:hd + 1, :]
    for g in range(groups):
        o_ref[:, g * hd:(g + 1) * hd] = out_t[:, g * tq:(g + 1) * tq].T.astype(o_ref.dtype)


def _attention(q, k, vt, dims, batch):
    q_dim, kv_dim, _, hd, seq = dims
    n_kv = kv_dim // hd
    groups = q_dim // kv_dim
    tq, tk = _attn_tiles(seq)
    q3 = q.reshape(batch, seq, q_dim)
    cols = groups * tq
    kern = functools.partial(_attn_kernel, groups=groups, hd=hd, tq=tq, tk=tk, seq=seq)
    o = pl.pallas_call(
        kern,
        grid=(batch, n_kv, seq // tq),
        in_specs=[
            pl.BlockSpec((None, tq, groups * hd), lambda b, kh, i: (b, i, kh)),
            pl.BlockSpec((None, seq, hd), lambda b, kh, i: (kh, b, 0)),
            pl.BlockSpec((None, seq // tk, hd + _SUM_ROWS, tk), lambda b, kh, i: (kh, b, 0, 0)),
        ],
        out_specs=pl.BlockSpec((None, tq, groups * hd), lambda b, kh, i: (b, i, kh)),
        out_shape=jax.ShapeDtypeStruct((batch, seq, q_dim), _BF16),
        scratch_shapes=[
            pltpu.VMEM((hd, cols), _BF16),
            pltpu.VMEM((2, tk, cols), _F32),
            pltpu.VMEM((2, 1, cols), _F32),
            pltpu.VMEM((hd + _SUM_ROWS, cols), _F32),
        ],
        compiler_params=_params("parallel", "parallel", "parallel"),
        name="attention",
    )(q3, k, vt)
    return o.reshape(batch * seq, q_dim)


_POOL_HALO = 8


def _pool_kernel(prev_ref, u_ref, next_ref, wmix_ref, scale_ref, o_ref, ext_sc, run_sc,
                 *, tm, seq, grp):
    halo = _POOL_HALO
    tile = pl.program_id(0) % (seq // tm)
    ext_sc[0:halo, :] = jnp.where(tile == 0, 0.0, prev_ref[...])
    ext_sc[halo:halo + tm, :] = u_ref[...]
    ext_sc[halo + tm:halo + tm + halo, :] = jnp.where(tile == seq // tm - 1, 0.0, next_ref[...])
    t = tile * tm + lax.broadcasted_iota(jnp.int32, (tm, 1), 0)
    for g, w in enumerate(_POOL_WINDOWS):
        sl = slice(g * grp, (g + 1) * grp)
        run = ext_sc[:, sl]
        span = 1
        while span < w:
            run = run + pltpu.roll(run, span, 0)
            span *= 2
        run_sc[...] = run
        end = halo + w // 2 - 1
        wsum = run_sc[end:end + tm, :]
        lo = jnp.maximum(t - w // 2, 0)
        hi = jnp.minimum(t + w // 2 - 1, seq - 1)
        cnt = (hi - lo + 1).astype(_F32)
        dev = wsum / cnt - u_ref[:, sl]
        y = jnp.dot(dev.astype(_BF16), wmix_ref[g], preferred_element_type=_F32)
        o_ref[:, sl] = (y * scale_ref[:, sl]).astype(o_ref.dtype)


def _pool(u, wmix, scale, seq):
    n, p = u.shape
    grp = p // len(_POOL_WINDOWS)
    tm = _largest_tile(seq, 512, _POOL_HALO)
    halo = _POOL_HALO
    hb = tm // halo
    kern = functools.partial(_pool_kernel, tm=tm, seq=seq, grp=grp)
    return pl.pallas_call(
        kern,
        grid=(n // tm,),
        in_specs=[
            pl.BlockSpec((halo, p), lambda i: (jnp.maximum(i * hb - 1, 0), 0)),
            pl.BlockSpec((tm, p), lambda i: (i, 0)),
            pl.BlockSpec((halo, p), lambda i: (jnp.minimum((i + 1) * hb, n // halo - 1), 0)),
            pl.BlockSpec(wmix.shape, lambda i: (0, 0, 0)),
            pl.BlockSpec((1, p), lambda i: (0, 0)),
        ],
        out_specs=pl.BlockSpec((tm, p), lambda i: (i, 0)),
        out_shape=jax.ShapeDtypeStruct((n, p), _BF16),
        scratch_shapes=[pltpu.VMEM((tm + 2 * halo, p), _F32),
                        pltpu.VMEM((tm + 2 * halo, grp), _F32)],
        compiler_params=_params("parallel"),
        name="pool",
    )(u, u, u, wmix, scale)


def _merge_kernel(o_ref, p_ref, wa_ref, wp_ref, ga_ref, gp_ref, out_ref):
    a = jnp.dot(o_ref[...], wa_ref[...], preferred_element_type=_F32)
    p = jnp.dot(p_ref[...], wp_ref[...], preferred_element_type=_F32)
    merged = ga_ref[...].astype(_F32) * a + gp_ref[...].astype(_F32) * p
    out_ref[...] = merged.astype(out_ref.dtype)


def _merge(o, pooled, wa, wp, layer, gates):
    n, q_dim = o.shape
    p_dim = pooled.shape[1]
    d = wa.shape[2]
    tm = _largest_tile(n, 1024, _SUBLANES)
    tn = _largest_tile(d, 1024, _LANES)
    nj = d // tn
    return pl.pallas_call(
        _merge_kernel,
        grid=(n // tm, nj),
        in_specs=[
            pl.BlockSpec((tm, q_dim), lambda i, j: (i, 0)),
            pl.BlockSpec((tm, p_dim), lambda i, j: (i, 0)),
            pl.BlockSpec((None, q_dim, tn), lambda i, j: (layer, 0, j)),
            pl.BlockSpec((None, p_dim, tn), lambda i, j: (layer, 0, j)),
            pl.BlockSpec((tm, tn), lambda i, j: (i, j)),
            pl.BlockSpec((tm, tn), lambda i, j: (i, nj + j)),
        ],
        out_specs=pl.BlockSpec((tm, tn), lambda i, j: (i, j)),
        out_shape=jax.ShapeDtypeStruct((n, d), _BF16),
        compiler_params=_params("parallel", "arbitrary"),
        name="merge",
    )(o, pooled, wa, wp, gates, gates)


def _out_proj_kernel(a_ref, w_ref, res_ref, out_ref):
    out_ref[...] = res_ref[...] + jnp.dot(a_ref[...], w_ref[...], preferred_element_type=_F32)


def _out_proj(merged, w, layer, h):
    n, d = h.shape
    tm = _largest_tile(n, 1024, _SUBLANES)
    tn = _largest_tile(d, 1024, _LANES)
    return pl.pallas_call(
        _out_proj_kernel,
        grid=(n // tm, d // tn),
        in_specs=[
            pl.BlockSpec((tm, merged.shape[1]), lambda i, j: (i, 0)),
            pl.BlockSpec((None, merged.shape[1], tn), lambda i, j: (layer, 0, j)),
            pl.BlockSpec((tm, tn), lambda i, j: (i, j)),
        ],
        out_specs=pl.BlockSpec((tm, tn), lambda i, j: (i, j)),
        out_shape=jax.ShapeDtypeStruct((n, d), _F32),
        compiler_params=_params("parallel", "arbitrary"),
        name="out_proj",
    )(merged, w, h)


def _swiglu_step(xn, wg_ref, wu_ref, wd_ref):
    a = jnp.dot(xn, wg_ref[...], preferred_element_type=_F32)
    b = jnp.dot(xn, wu_ref[...], preferred_element_type=_F32)
    act = (a * jax.nn.sigmoid(a) * b).astype(_BF16)
    return jnp.dot(act, wd_ref[...], preferred_element_type=_F32)


def _ffn_kernel(h_ref, g_ref, wg_ref, wu_ref, wd_ref, out_ref, xn_sc):
    @pl.when(pl.program_id(1) == 0)
    def _():
        x = h_ref[...]
        xn_sc[...] = (x * _rms_scale(x) * g_ref[...]).astype(_BF16)
        out_ref[...] = x

    out_ref[...] += _swiglu_step(xn_sc[...], wg_ref, wu_ref, wd_ref)


def _ffn(h, g2, wg, wu, wd, layer):
    n, d = h.shape
    ff = wg.shape[2]
    tm = _largest_tile(n, 512, _SUBLANES)
    tf = _largest_tile(ff, 512, _LANES)
    return pl.pallas_call(
        _ffn_kernel,
        grid=(n // tm, ff // tf),
        in_specs=[
            pl.BlockSpec((tm, d), lambda i, f: (i, 0)),
            pl.BlockSpec((1, d), lambda i, f: (0, 0)),
            pl.BlockSpec((None, d, tf), lambda i, f: (layer, 0, f)),
            pl.BlockSpec((None, d, tf), lambda i, f: (layer, 0, f)),
            pl.BlockSpec((None, tf, d), lambda i, f: (layer, f, 0)),
        ],
        out_specs=pl.BlockSpec((tm, d), lambda i, f: (i, 0)),
        out_shape=jax.ShapeDtypeStruct((n, d), _F32),
        scratch_shapes=[pltpu.VMEM((tm, d), _BF16)],
        compiler_params=_params("parallel", "arbitrary"),
        name="ffn_dense",
    )(h, g2, wg, wu, wd)


_R_E1, _R_E2, _R_RANK1, _R_RANK2, _R_W1, _R_W2 = range(6)


def _router_kernel(h_ref, g_ref, wr_ref, br_ref, xp_ref, info_ref, cnt_ref, carry_sc, *, n_exp):
    @pl.when(pl.program_id(0) == 0)
    def _():
        carry_sc[...] = jnp.zeros(carry_sc.shape, _F32)

    x = h_ref[...]
    xn = x * _rms_scale(x) * g_ref[...]
    tm, d = xn.shape

    xb = xn.astype(_BF16).astype(_F32)
    lo = lax.bitcast_convert_type(xb[:, :d // 2], jnp.int32)
    hi = lax.bitcast_convert_type(xb[:, d // 2:], jnp.int32)
    xp_ref[...] = lax.shift_right_logical(lo, 16) | hi

    logits = jnp.dot(xn, wr_ref[...], preferred_element_type=_F32,
                     precision=lax.Precision.HIGHEST) + br_ref[...]
    lane = lax.broadcasted_iota(jnp.int32, logits.shape, 1)
    logits = jnp.where(lane < n_exp, logits, -jnp.inf)
    v1 = jnp.max(logits, axis=-1, keepdims=True)
    i1 = jnp.min(jnp.where(logits == v1, lane, _LANES), axis=-1, keepdims=True)
    pick1 = lane == i1
    rest = jnp.where(pick1, -jnp.inf, logits)
    v2 = jnp.max(rest, axis=-1, keepdims=True)
    i2 = jnp.min(jnp.where(rest == v2, lane, _LANES), axis=-1, keepdims=True)
    pick2 = lane == i2
    e = jnp.exp(v2 - v1)
    w1 = 1.0 / (1.0 + e)
    w2 = e / (1.0 + e)

    cnt = (pick1 | pick2).astype(_F32)
    row = lax.broadcasted_iota(jnp.int32, (tm, tm), 0)
    col = lax.broadcasted_iota(jnp.int32, (tm, tm), 1)
    below = (row > col).astype(_BF16)
    before = jnp.dot(below, cnt.astype(_BF16), preferred_element_type=_F32) + carry_sc[...]
    rank1 = jnp.sum(jnp.where(pick1, before, 0.0), axis=-1, keepdims=True)
    rank2 = jnp.sum(jnp.where(pick2, before, 0.0), axis=-1, keepdims=True)
    carry_sc[...] += jnp.sum(cnt, axis=0, keepdims=True)

    info = jnp.zeros(logits.shape, _F32)
    for slot, val in ((_R_E1, i1.astype(_F32)), (_R_E2, i2.astype(_F32)), (_R_RANK1, rank1),
                      (_R_RANK2, rank2), (_R_W1, w1), (_R_W2, w2)):
        info = jnp.where(lane == slot, val, info)
    info_ref[...] = info
    cnt_ref[...] = jnp.broadcast_to(carry_sc[...], cnt_ref.shape)


def _router(h, g2, wr_pad, br_pad, n_exp):
    n, d = h.shape
    tm = _largest_tile(n, 512, _SUBLANES)
    kern = functools.partial(_router_kernel, n_exp=n_exp)
    return pl.pallas_call(
        kern,
        grid=(n // tm,),
        in_specs=[
            pl.BlockSpec((tm, d), lambda i: (i, 0)),
            pl.BlockSpec((1, d), lambda i: (0, 0)),
            pl.BlockSpec((d, _LANES), lambda i: (0, 0)),
            pl.BlockSpec((1, _LANES), lambda i: (0, 0)),
        ],
        out_specs=[
            pl.BlockSpec((tm, d // 2), lambda i: (i, 0)),
            pl.BlockSpec((tm, _LANES), lambda i: (i, 0)),
            pl.BlockSpec((_SUBLANES, _LANES), lambda i: (0, 0)),
        ],
        out_shape=[
            jax.ShapeDtypeStruct((n, d // 2), jnp.int32),
            jax.ShapeDtypeStruct((n, _LANES), _F32),
            jax.ShapeDtypeStruct((_SUBLANES, _LANES), _F32),
        ],
        scratch_shapes=[pltpu.VMEM((1, _LANES), _F32)],
        compiler_params=_params("arbitrary"),
        name="router",
    )(h, g2, wr_pad, br_pad)


_DMA_ISSUE_UNROLL = 8


def _row_copy(src_ref, src_row, dst_ref, dst_row, sem):
    return pltpu.make_async_copy(src_ref.at[pl.ds(src_row, 1)], dst_ref.at[pl.ds(dst_row, 1)], sem)


def _dispatch_kernel(pos_ref, xp_ref, zeros_ref, xs_ref, sem, *, tc):
    del zeros_ref

    def start(t, carry):
        for k in range(_TOP_K):
            _row_copy(xp_ref, t, xs_ref, pos_ref[_TOP_K * t + k], sem).start()
        return carry

    lax.fori_loop(0, tc, start, 0, unroll=_DMA_ISSUE_UNROLL)
    for _ in range(_TOP_K):
        pltpu.make_async_copy(xp_ref, xs_ref.at[pl.ds(0, tc)], sem).wait()


def _dispatch(pos, xp, rows):
    n, half = xp.shape
    tc = _largest_tile(n, 512, _SUBLANES)
    kern = functools.partial(_dispatch_kernel, tc=tc)
    return pl.pallas_call(
        kern,
        grid=(n // tc,),
        in_specs=[
            pl.BlockSpec((_TOP_K * tc,), lambda i: (i,), memory_space=pltpu.SMEM),
            pl.BlockSpec((tc, half), lambda i: (i, 0)),
            pl.BlockSpec(memory_space=pl.ANY),
        ],
        out_specs=pl.BlockSpec(memory_space=pl.ANY),
        out_shape=jax.ShapeDtypeStruct((rows, half), jnp.int32),
        scratch_shapes=[pltpu.SemaphoreType.DMA(())],
        input_output_aliases={2: 0},
        compiler_params=_params("arbitrary"),
        name="moe_dispatch",
    )(pos, xp, jnp.zeros((rows, half), jnp.int32))


def _expert_changed(te_ref, i):
    return (i == 0) | (te_ref[i] != te_ref[jnp.maximum(i - 1, 0)])


def _moe_up_kernel(te_ref, nact_ref, xs_ref, wg_ref, wu_ref, act_ref, wg_sc, wu_sc):
    i = pl.program_id(1)
    active = i < nact_ref[0]

    @pl.when(active & _expert_changed(te_ref, i))
    def _():
        wg_sc[...] = wg_ref[...].astype(_BF16)
        wu_sc[...] = wu_ref[...].astype(_BF16)

    @pl.when(active)
    def _():
        packed = xs_ref[...]
        lo = lax.bitcast_convert_type(lax.shift_left(packed, 16), _F32)
        hi = lax.bitcast_convert_type(packed & jnp.int32(-65536), _F32)
        x = jnp.concatenate([lo, hi], axis=1).astype(_BF16)
        a = jnp.dot(x, wg_sc[...], preferred_element_type=_F32)
        b = jnp.dot(x, wu_sc[...], preferred_element_type=_F32)
        act_ref[...] = (a * jax.nn.sigmoid(a) * b).astype(act_ref.dtype)

    @pl.when(jnp.logical_not(active))
    def _():
        act_ref[...] = jnp.zeros(act_ref.shape, act_ref.dtype)


def _moe_down_kernel(te_ref, nact_ref, act_ref, wd_ref, y_ref, wd_sc):
    i = pl.program_id(1)
    active = i < nact_ref[0]

    @pl.when(active & _expert_changed(te_ref, i))
    def _():
        wd_sc[...] = wd_ref[...].astype(_BF16)

    @pl.when(active)
    def _():
        y_ref[...] = jnp.dot(act_ref[...], wd_sc[...], preferred_element_type=_F32)

    @pl.when(jnp.logical_not(active))
    def _():
        y_ref[...] = jnp.zeros(y_ref.shape, y_ref.dtype)


def _moe_ffn(tile_expert, n_active, xs, wg, wu, wd, tm):
    rows, half = xs.shape
    d = 2 * half
    ff = wg.shape[2]
    tf = _largest_tile(ff, 512, _LANES)
    tn = _largest_tile(d, 512, _LANES)
    n_tiles = rows // tm

    def row_idx(i, nact):
        return jnp.maximum(jnp.minimum(i, nact[0] - 1), 0)

    up_spec = pltpu.PrefetchScalarGridSpec(
        num_scalar_prefetch=2,
        grid=(ff // tf, n_tiles),
        in_specs=[
            pl.BlockSpec((tm, half), lambda f, i, te, na: (row_idx(i, na), 0)),
            pl.BlockSpec((None, d, tf), lambda f, i, te, na: (te[i], 0, f)),
            pl.BlockSpec((None, d, tf), lambda f, i, te, na: (te[i], 0, f)),
        ],
        out_specs=pl.BlockSpec((tm, tf), lambda f, i, te, na: (i, f)),
        scratch_shapes=[pltpu.VMEM((d, tf), _BF16), pltpu.VMEM((d, tf), _BF16)],
    )
    act = pl.pallas_call(
        _moe_up_kernel,
        grid_spec=up_spec,
        out_shape=jax.ShapeDtypeStruct((rows, ff), _BF16),
        compiler_params=_params("arbitrary", "arbitrary"),
        name="moe_up",
    )(tile_expert, n_active, xs, wg, wu)

    down_spec = pltpu.PrefetchScalarGridSpec(
        num_scalar_prefetch=2,
        grid=(d // tn, n_tiles),
        in_specs=[
            pl.BlockSpec((tm, ff), lambda n, i, te, na: (row_idx(i, na), 0)),
            pl.BlockSpec((None, ff, tn), lambda n, i, te, na: (te[i], 0, n)),
        ],
        out_specs=pl.BlockSpec((tm, tn), lambda n, i, te, na: (i, n)),
        scratch_shapes=[pltpu.VMEM((ff, tn), _BF16)],
    )
    return pl.pallas_call(
        _moe_down_kernel,
        grid_spec=down_spec,
        out_shape=jax.ShapeDtypeStruct((rows, d), _F32),
        compiler_params=_params("arbitrary", "arbitrary"),
        name="moe_down",
    )(tile_expert, n_active, act, wd)


def _combine_kernel(pos_ref, info_ref, h_ref, y_ref, g_ref, out_ref, buf, sem, *, tc, final_norm):
    def start(t, carry):
        for k in range(_TOP_K):
            pltpu.make_async_copy(y_ref.at[pl.ds(pos_ref[_TOP_K * t + k], 1)],
                                  buf.at[k, pl.ds(t, 1)], sem).start()
        return carry

    lax.fori_loop(0, tc, start, 0, unroll=_DMA_ISSUE_UNROLL)
    for k in range(_TOP_K):
        pltpu.make_async_copy(y_ref.at[pl.ds(0, tc)], buf.at[k], sem).wait()

    info = info_ref[...]
    w1 = info[:, _R_W1:_R_W1 + 1]
    w2 = info[:, _R_W2:_R_W2 + 1]
    out = h_ref[...] + (w1 * buf[0] + w2 * buf[1])
    if final_norm:
        out = out * _rms_scale(out) * g_ref[...]
    out_ref[...] = out


def _combine(pos, info, h, y, final_g, final_norm):
    n, d = h.shape
    tc = _largest_tile(n, 256, _SUBLANES)
    kern = functools.partial(_combine_kernel, tc=tc, final_norm=final_norm)
    return pl.pallas_call(
        kern,
        grid=(n // tc,),
        in_specs=[
            pl.BlockSpec((_TOP_K * tc,), lambda i: (i,), memory_space=pltpu.SMEM),
            pl.BlockSpec((tc, _LANES), lambda i: (i, 0)),
            pl.BlockSpec((tc, d), lambda i: (i, 0)),
            pl.BlockSpec(memory_space=pl.ANY),
            pl.BlockSpec((1, d), lambda i: (0, 0)),
        ],
        out_specs=pl.BlockSpec((tc, d), lambda i: (i, 0)),
        out_shape=jax.ShapeDtypeStruct((n, d), _F32),
        scratch_shapes=[pltpu.VMEM((_TOP_K, tc, d), _F32), pltpu.SemaphoreType.DMA(())],
        compiler_params=_params("arbitrary"),
        name="moe_combine",
    )(pos, info, h, y, final_g)


def _final_norm_kernel(h_ref, g_ref, out_ref):
    x = h_ref[...]
    out_ref[...] = x * _rms_scale(x) * g_ref[...]


def _final_norm(h, g):
    n, d = h.shape
    tm = _largest_tile(n, 512, _SUBLANES)
    return pl.pallas_call(
        _final_norm_kernel,
        grid=(n // tm,),
        in_specs=[pl.BlockSpec((tm, d), lambda i: (i, 0)), pl.BlockSpec((1, d), lambda i: (0, 0))],
        out_specs=pl.BlockSpec((tm, d), lambda i: (i, 0)),
        out_shape=jax.ShapeDtypeStruct((n, d), _F32),
        compiler_params=_params("parallel"),
        name="final_norm",
    )(h, g)


_MOE_ROW_TILE = 512


def _moe(h, g2, wr, br, wg, wu, wd, layer, final_g, final_norm):
    n, d = h.shape
    n_exp = wr.shape[1]
    tm = _largest_tile(n, _MOE_ROW_TILE, _SUBLANES)
    wr_pad = jnp.zeros((d, _LANES), _F32).at[:, :n_exp].set(wr)
    br_pad = jnp.zeros((1, _LANES), _F32).at[0, :n_exp].set(br)
    xp, info, cnt = _router(h, g2, wr_pad, br_pad, n_exp)

    counts = cnt[0, :n_exp].astype(jnp.int32)
    tiles_per = (counts + tm - 1) // tm
    tile_end = jnp.cumsum(tiles_per)
    starts = (tile_end - tiles_per) * tm
    n_active = tile_end[-1:]
    n_tiles = (_TOP_K * n) // tm + n_exp
    tile_ids = jnp.minimum(jnp.arange(n_tiles, dtype=jnp.int32), n_active - 1)
    tile_expert = jnp.sum((tile_ids[:, None] >= tile_end[None, :]).astype(jnp.int32), axis=1)
    tile_expert = tile_expert + layer * n_exp
    experts = info[:, _R_E1:_R_E2 + 1].astype(jnp.int32)
    ranks = info[:, _R_RANK1:_R_RANK2 + 1].astype(jnp.int32)
    onehot = experts[:, :, None] == jnp.arange(n_exp, dtype=jnp.int32)[None, None, :]
    pos = (jnp.sum(jnp.where(onehot, starts[None, None, :], 0), axis=-1) + ranks).reshape(-1)

    xs = _dispatch(pos, xp, n_tiles * tm)
    y = _moe_ffn(tile_expert, n_active, xs, wg, wu, wd, tm)
    return _combine(pos, info, h, y, final_g, final_norm)


def _rope_tables(seq, hd):
    axis_dim = hd // 2
    nfreq = axis_dim // 2
    rows = seq // _GRID_W
    row = jnp.broadcast_to(jnp.arange(rows)[:, None], (rows, _GRID_W)).reshape(seq).astype(_F32)
    col = jnp.broadcast_to(jnp.arange(_GRID_W)[None, :], (rows, _GRID_W)).reshape(seq).astype(_F32)
    inv = _ROPE_THETA ** (-jnp.arange(nfreq, dtype=_F32) * 2.0 / axis_dim)
    ar = row[:, None] * inv
    ac = col[:, None] * inv
    ang = jnp.concatenate([ar, ar, ac, ac], axis=-1)
    lane = jnp.arange(hd)
    sign = jnp.where((lane % axis_dim) < nfreq, -1.0, 1.0).astype(_F32)
    return jnp.cos(ang), jnp.sin(ang) * sign


def kernel(x, norm1_g, w_in, q_norm_g, k_norm_g, w_attn_o, w_pool_mix, pool_scale, w_pool_o,
           w_out, norm2_g, ffn_w_gate, ffn_w_up, ffn_w_down, router_w, router_b,
           moe_w_gate, moe_w_up, moe_w_down, final_g):
    batch, seq, d = x.shape
    depth = w_in.shape[0]
    hd = q_norm_g.shape[1]
    q_dim = w_attn_o.shape[1]
    p_dim = pool_scale.shape[1]
    kv_dim = (w_in.shape[2] - q_dim - p_dim - 2 * d) // 2
    dims = (q_dim, kv_dim, p_dim, hd, seq)
    assert seq % _GRID_W == 0 and q_dim % kv_dim == 0 and kv_dim % hd == 0

    cos, sin_signed = _rope_tables(seq, hd)
    bf = lambda w: w.astype(_BF16)
    row = lambda v: v.reshape(1, -1)
    experts = lambda w: w.reshape((-1,) + w.shape[2:])

    w_in, w_attn_o, w_pool_o, w_out = bf(w_in), bf(w_attn_o), bf(w_pool_o), bf(w_out)
    ffn_w_gate, ffn_w_up, ffn_w_down = bf(ffn_w_gate), bf(ffn_w_up), bf(ffn_w_down)
    moe_w_gate, moe_w_up, moe_w_down = experts(moe_w_gate), experts(moe_w_up), experts(moe_w_down)

    h = x.reshape(batch * seq, d)
    fg = row(final_g)
    for l in range(depth):
        q, k, vt, u, gates = _in_proj(h, row(norm1_g[l]), w_in, l, cos, sin_signed,
                                      row(q_norm_g[l]), row(k_norm_g[l]), dims)
        o = _attention(q, k, vt, dims, batch)
        pooled = _pool(u, bf(w_pool_mix[l]), row(pool_scale[l]), seq)
        merged = _merge(o, pooled, w_attn_o, w_pool_o, l, gates)
        h = _out_proj(merged, w_out, l, h)
        i = l // 2
        if l % 2 == 0:
            h = _ffn(h, row(norm2_g[l]), ffn_w_gate, ffn_w_up, ffn_w_down, i)
        else:
            last = l == depth - 1
            h = _moe(h, row(norm2_g[l]), router_w[i], router_b[i], moe_w_gate, moe_w_up,
                     moe_w_down, i, fg, last)
    if depth % 2 == 1 or depth == 0:
        h = _final_norm(h, fg)
    return h.reshape(batch, seq, d)
```

```python
import functools
import math

import jax
import jax.numpy as jnp
from jax import lax
from jax.experimental import pallas as pl
from jax.experimental.pallas import tpu as pltpu

_GRID_W = 64
_ROPE_THETA = 10000.0
_POOL_WINDOWS = (2, 4, 8, 16)
_TOP_K = 2
_NORM_EPS = 1e-6
_LOG2_E = 1.4426950408889634

_LANES = 128
_SUBLANES = 8
_SUM_ROWS = 16
_V7X_VMEM_LIMIT_BYTES = 56 * 1024 * 1024

_F32 = jnp.float32
_BF16 = jnp.bfloat16


def _params(*semantics):
    return pltpu.CompilerParams(dimension_semantics=semantics,
                                vmem_limit_bytes=_V7X_VMEM_LIMIT_BYTES)


def _largest_tile(n, cap, quantum):
    t = min(cap, n)
    t -= t % quantum
    while t > quantum and n % t:
        t -= quantum
    assert t >= quantum and n % t == 0, (n, cap, quantum)
    return t


def _rms_scale(x):
    return lax.rsqrt(jnp.mean(x * x, axis=-1, keepdims=True) + _NORM_EPS)


_ROW_PARTS = 4


def _in_proj_kernel(h_ref, g_ref, w_ref, cos_ref, sin_ref, qg_ref, kg_ref,
                    q_ref, k_ref, vt_ref, u_ref, gate_ref, xn_sc, acc_sc,
                    *, nq, nk, nv, nu, ng, hd, tk, qscale):
    j = pl.program_id(1)
    ncol = nq + nk + nv + nu + ng
    tn = w_ref.shape[1]
    heads = tn // hd

    tm = xn_sc.shape[0]

    def matmul_rows(slot, rows):
        acc_sc[slot, rows, :] = jnp.dot(xn_sc[rows, :], w_ref[...], preferred_element_type=_F32)

    def norm_rope(acc, rows, gain, scale):
        cos = cos_ref[rows, :]
        sin = sin_ref[rows, :]
        lane = lax.broadcasted_iota(jnp.int32, (1, hd), 1)
        first_half = (lane % (hd // 2)) < (hd // 4)
        outs = []
        for hh in range(heads):
            xh = acc[:, hh * hd:(hh + 1) * hd]
            xh = xh * _rms_scale(xh) * gain
            partner = jnp.where(first_half,
                                pltpu.roll(xh, hd - hd // 4, 1),
                                pltpu.roll(xh, hd // 4, 1))
            outs.append(((xh * cos + partner * sin) * scale).astype(_BF16))
        return outs

    def finish_q(acc, rows):
        q_ref[rows, :] = jnp.concatenate(norm_rope(acc, rows, qg_ref[...], qscale), axis=1)

    def finish_k(acc, rows):
        for hh, kh in enumerate(norm_rope(acc, rows, kg_ref[...], 1.0)):
            k_ref[hh, rows, :] = kh

    def finish_v(acc, rows):
        sum_rows = (lax.broadcasted_iota(jnp.int32, (_SUM_ROWS, tk), 0) == 0).astype(_BF16)
        c = rows.start // tk
        for hh in range(heads):
            vt_ref[hh, c, 0:hd, :] = acc[:, hh * hd:(hh + 1) * hd].T.astype(_BF16)
            vt_ref[hh, c, hd:hd + _SUM_ROWS, :] = sum_rows

    def finish_u(acc, rows):
        u_ref[rows, :] = acc

    def finish_gate(acc, rows):
        gate_ref[rows, :] = (0.5 * jnp.tanh(0.5 * acc) + 0.5).astype(_BF16)

    def row_parts(parts):
        return [slice(r * (tm // parts), (r + 1) * (tm // parts)) for r in range(parts)]

    @pl.when(j == 0)
    def _():
        for rows in row_parts(_ROW_PARTS):
            x = h_ref[rows, :]
            xn_sc[rows, :] = (x * _rms_scale(x) * g_ref[...]).astype(_BF16)
            matmul_rows(0, rows)

    t = j - 1
    lo = 0
    for cnt, finish, parts in ((nq, finish_q, 1), (nk, finish_k, 1),
                               (nv, finish_v, tm // tk), (nu, finish_u, _ROW_PARTS),
                               (ng, finish_gate, _ROW_PARTS)):
        @pl.when((t >= lo) & (t < lo + cnt) & (j < ncol))
        def _(finish=finish, parts=parts):
            for rows in row_parts(parts):
                finish(acc_sc[t % 2, rows, :], rows)
                matmul_rows(j % 2, rows)
        lo += cnt

    @pl.when(j == ncol)
    def _():
        finish_gate(acc_sc[t % 2], slice(0, tm))


def _attn_tiles(seq):
    tq = _largest_tile(seq, 1024, _LANES)
    tk = _largest_tile(seq // 2, 512, _LANES)
    assert (seq // tk) % 2 == 0
    return tq, tk


def _in_proj(h, g1, w, layer, cos, sin_signed, qg, kg, dims):
    n, d = h.shape
    q_dim, kv_dim, p_dim, hd, seq = dims
    _, tk = _attn_tiles(seq)
    cols = w.shape[2]
    tm = _largest_tile(seq, 1024, _SUBLANES)
    tn = _largest_tile(_gcd_all(q_dim, kv_dim, p_dim, d), 512, hd)
    nq, nk, nv, nu, ng = q_dim // tn, kv_dim // tn, kv_dim // tn, p_dim // tn, 2 * d // tn
    ncol = nq + nk + nv + nu + ng
    assert ncol * tn == cols and tm % tk == 0
    s_tiles = seq // tm
    heads = tn // hd

    def tile(j, lo, cnt):
        return jnp.clip(j - 1 - lo, 0, cnt - 1)

    kern = functools.partial(_in_proj_kernel, nq=nq, nk=nk, nv=nv, nu=nu, ng=ng, hd=hd, tk=tk,
                             qscale=_LOG2_E * float(hd) ** -0.5)
    return pl.pallas_call(
        kern,
        grid=(n // tm, ncol + 1),
        in_specs=[
            pl.BlockSpec((tm, d), lambda i, j: (i, 0)),
            pl.BlockSpec((1, d), lambda i, j: (0, 0)),
            pl.BlockSpec((None, d, tn), lambda i, j: (layer, 0, jnp.minimum(j, ncol - 1))),
            pl.BlockSpec((tm, hd), lambda i, j: (i % s_tiles, 0)),
            pl.BlockSpec((tm, hd), lambda i, j: (i % s_tiles, 0)),
            pl.BlockSpec((1, hd), lambda i, j: (0, 0)),
            pl.BlockSpec((1, hd), lambda i, j: (0, 0)),
        ],
        out_specs=[
            pl.BlockSpec((tm, tn), lambda i, j: (i, tile(j, 0, nq))),
            pl.BlockSpec((heads, tm, hd), lambda i, j: (tile(j, nq, nk), i, 0)),
            pl.BlockSpec((heads, tm // tk, hd + _SUM_ROWS, tk),
                         lambda i, j: (tile(j, nq + nk, nv), i, 0, 0)),
            pl.BlockSpec((tm, tn), lambda i, j: (i, tile(j, nq + nk + nv, nu))),
            pl.BlockSpec((tm, tn), lambda i, j: (i, tile(j, nq + nk + nv + nu, ng))),
        ],
        out_shape=[
            jax.ShapeDtypeStruct((n, q_dim), _BF16),
            jax.ShapeDtypeStruct((kv_dim // hd, n, hd), _BF16),
            jax.ShapeDtypeStruct((kv_dim // hd, n // tk, hd + _SUM_ROWS, tk), _BF16),
            jax.ShapeDtypeStruct((n, p_dim), _F32),
            jax.ShapeDtypeStruct((n, 2 * d), _BF16),
        ],
        scratch_shapes=[pltpu.VMEM((tm, d), _BF16), pltpu.VMEM((2, tm, tn), _F32)],
        compiler_params=_params("parallel", "arbitrary"),
        name="in_proj",
    )(h, g1, w, cos, sin_signed, qg, kg)


def _gcd_all(*vals):
    g = 0
    for v in vals:
        g = math.gcd(g, v)
    return g


def _attn_kernel(q_ref, k_ref, vt_ref, o_ref, qs_sc, s_sc, cm_sc, acc_sc,
                 *, groups, hd, tq, tk, seq):
    n_chunks = seq // tk
    cols = groups * tq

    for g in range(groups):
        qs_sc[:, g * tq:(g + 1) * tq] = q_ref[:, g * hd:(g + 1) * hd].astype(_F32).T.astype(_BF16)
    acc_sc[...] = jnp.zeros(acc_sc.shape, _F32)

    def scores(c, slot):
        k_c = k_ref[pl.ds(pl.multiple_of(c * tk, tk), tk), :]
        s = jnp.dot(k_c, qs_sc[...], preferred_element_type=_F32)
        s_sc[slot] = s
        cm_sc[slot] = jnp.max(s, axis=0, keepdims=True)

    def softmax_values(c, slot, m_prev):
        m_new = jnp.maximum(m_prev, cm_sc[slot])
        alpha = jnp.exp2(m_prev - m_new)
        p = jnp.exp2(s_sc[slot] - m_new)
        acc_sc[...] = acc_sc[...] * alpha + jnp.dot(vt_ref[c], p.astype(_BF16),
                                                    preferred_element_type=_F32)
        return m_new

    def pair(j, m):
        c = 2 * j
        scores(c + 1, 1)
        m = softmax_values(c, 0, m)
        scores(c + 2, 0)
        return softmax_values(c + 1, 1, m)

    m = jnp.full((1, cols), -jnp.inf, _F32)
    scores(0, 0)
    m = lax.fori_loop(0, n_chunks // 2 - 1, pair, m)
    scores(n_chunks - 1, 1)
    m = softmax_values(n_chunks - 2, 0, m)
    m = softmax_values(n_chunks - 1, 1, m)

    out_t = acc_sc[0:hd, :] / acc_sc[hd:hd + 1, :]
    for g in range(groups):
        o_ref[:, g * hd:(g + 1) * hd] = out_t[:, g * tq:(g + 1) * tq].T.astype(o_ref.dtype)


def _attention(q, k, vt, dims, batch):
    q_dim, kv_dim, _, hd, seq = dims
    n_kv = kv_dim // hd
    groups = q_dim // kv_dim
    tq, tk = _attn_tiles(seq)
    q3 = q.reshape(batch, seq, q_dim)
    cols = groups * tq
    kern = functools.partial(_attn_kernel, groups=groups, hd=hd, tq=tq, tk=tk, seq=seq)
    o = pl.pallas_call(
        kern,
        grid=(batch, n_kv, seq // tq),
        in_specs=[
            pl.BlockSpec((None, tq, groups * hd), lambda b, kh, i: (b, i, kh)),
            pl.BlockSpec((None, seq, hd), lambda b, kh, i: (kh, b, 0)),
            pl.BlockSpec((None, seq // tk, hd + _SUM_ROWS, tk), lambda b, kh, i: (kh, b, 0, 0)),
        ],
        out_specs=pl.BlockSpec((None, tq, groups * hd), lambda b, kh, i: (b, i, kh)),
        out_shape=jax.ShapeDtypeStruct((batch, seq, q_dim), _BF16),
        scratch_shapes=[
            pltpu.VMEM((hd, cols), _BF16),
            pltpu.VMEM((2, tk, cols), _F32),
            pltpu.VMEM((2, 1, cols), _F32),
            pltpu.VMEM((hd + _SUM_ROWS, cols), _F32),
        ],
        compiler_params=_params("parallel", "parallel", "parallel"),
        name="attention",
    )(q3, k, vt)
    return o.reshape(batch * seq, q_dim)


_POOL_HALO = 8


def _pool_kernel(prev_ref, u_ref, next_ref, wmix_ref, scale_ref, o_ref, ext_sc, run_sc,
                 *, tm, seq, grp):
    halo = _POOL_HALO
    tile = pl.program_id(0) % (seq // tm)
    ext_sc[0:halo, :] = jnp.where(tile == 0, 0.0, prev_ref[...])
    ext_sc[halo:halo + tm, :] = u_ref[...]
    ext_sc[halo + tm:halo + tm + halo, :] = jnp.where(tile == seq // tm - 1, 0.0, next_ref[...])
    t = tile * tm + lax.broadcasted_iota(jnp.int32, (tm, 1), 0)
    for g, w in enumerate(_POOL_WINDOWS):
        sl = slice(g * grp, (g + 1) * grp)
        run = ext_sc[:, sl]
        span = 1
        while span < w:
            run = run + pltpu.roll(run, span, 0)
            span *= 2
        run_sc[...] = run
        end = halo + w // 2 - 1
        wsum = run_sc[end:end + tm, :]
        lo = jnp.maximum(t - w // 2, 0)
        hi = jnp.minimum(t + w // 2 - 1, seq - 1)
        cnt = (hi - lo + 1).astype(_F32)
        dev = wsum / cnt - u_ref[:, sl]
        y = jnp.dot(dev.astype(_BF16), wmix_ref[g], preferred_element_type=_F32)
        o_ref[:, sl] = (y * scale_ref[:, sl]).astype(o_ref.dtype)


def _pool(u, wmix, scale, seq):
    n, p = u.shape
    grp = p // len(_POOL_WINDOWS)
    tm = _largest_tile(seq, 512, _POOL_HALO)
    halo = _POOL_HALO
    hb = tm // halo
    kern = functools.partial(_pool_kernel, tm=tm, seq=seq, grp=grp)
    return pl.pallas_call(
        kern,
        grid=(n // tm,),
        in_specs=[
            pl.BlockSpec((halo, p), lambda i: (jnp.maximum(i * hb - 1, 0), 0)),
            pl.BlockSpec((tm, p), lambda i: (i, 0)),
            pl.BlockSpec((halo, p), lambda i: (jnp.minimum((i + 1) * hb, n // halo - 1), 0)),
            pl.BlockSpec(wmix.shape, lambda i: (0, 0, 0)),
            pl.BlockSpec((1, p), lambda i: (0, 0)),
        ],
        out_specs=pl.BlockSpec((tm, p), lambda i: (i, 0)),
        out_shape=jax.ShapeDtypeStruct((n, p), _BF16),
        scratch_shapes=[pltpu.VMEM((tm + 2 * halo, p), _F32),
                        pltpu.VMEM((tm + 2 * halo, grp), _F32)],
        compiler_params=_params("parallel"),
        name="pool",
    )(u, u, u, wmix, scale)


def _merge_kernel(o_ref, p_ref, wa_ref, wp_ref, ga_ref, gp_ref, out_ref):
    a = jnp.dot(o_ref[...], wa_ref[...], preferred_element_type=_F32)
    p = jnp.dot(p_ref[...], wp_ref[...], preferred_element_type=_F32)
    merged = ga_ref[...].astype(_F32) * a + gp_ref[...].astype(_F32) * p
    out_ref[...] = merged.astype(out_ref.dtype)


def _merge(o, pooled, wa, wp, layer, gates):
    n, q_dim = o.shape
    p_dim = pooled.shape[1]
    d = wa.shape[2]
    tm = _largest_tile(n, 1024, _SUBLANES)
    tn = _largest_tile(d, 1024, _LANES)
    nj = d // tn
    return pl.pallas_call(
        _merge_kernel,
        grid=(n // tm, nj),
        in_specs=[
            pl.BlockSpec((tm, q_dim), lambda i, j: (i, 0)),
            pl.BlockSpec((tm, p_dim), lambda i, j: (i, 0)),
            pl.BlockSpec((None, q_dim, tn), lambda i, j: (layer, 0, j)),
            pl.BlockSpec((None, p_dim, tn), lambda i, j: (layer, 0, j)),
            pl.BlockSpec((tm, tn), lambda i, j: (i, j)),
            pl.BlockSpec((tm, tn), lambda i, j: (i, nj + j)),
        ],
        out_specs=pl.BlockSpec((tm, tn), lambda i, j: (i, j)),
        out_shape=jax.ShapeDtypeStruct((n, d), _BF16),
        compiler_params=_params("parallel", "arbitrary"),
        name="merge",
    )(o, pooled, wa, wp, gates, gates)


def _out_proj_kernel(a_ref, w_ref, res_ref, out_ref):
    out_ref[...] = res_ref[...] + jnp.dot(a_ref[...], w_ref[...], preferred_element_type=_F32)


def _out_proj(merged, w, layer, h):
    n, d = h.shape
    tm = _largest_tile(n, 1024, _SUBLANES)
    tn = _largest_tile(d, 1024, _LANES)
    return pl.pallas_call(
        _out_proj_kernel,
        grid=(n // tm, d // tn),
        in_specs=[
            pl.BlockSpec((tm, merged.shape[1]), lambda i, j: (i, 0)),
            pl.BlockSpec((None, merged.shape[1], tn), lambda i, j: (layer, 0, j)),
            pl.BlockSpec((tm, tn), lambda i, j: (i, j)),
        ],
        out_specs=pl.BlockSpec((tm, tn), lambda i, j: (i, j)),
        out_shape=jax.ShapeDtypeStruct((n, d), _F32),
        compiler_params=_params("parallel", "arbitrary"),
        name="out_proj",
    )(merged, w, h)


def _swiglu_step(xn, wg_ref, wu_ref, wd_ref):
    a = jnp.dot(xn, wg_ref[...], preferred_element_type=_F32)
    b = jnp.dot(xn, wu_ref[...], preferred_element_type=_F32)
    act = (a * jax.nn.sigmoid(a) * b).astype(_BF16)
    return jnp.dot(act, wd_ref[...], preferred_element_type=_F32)


def _ffn_kernel(h_ref, g_ref, wg_ref, wu_ref, wd_ref, out_ref, xn_sc):
    @pl.when(pl.program_id(1) == 0)
    def _():
        x = h_ref[...]
        xn_sc[...] = (x * _rms_scale(x) * g_ref[...]).astype(_BF16)
        out_ref[...] = x

    out_ref[...] += _swiglu_step(xn_sc[...], wg_ref, wu_ref, wd_ref)


def _ffn(h, g2, wg, wu, wd, layer):
    n, d = h.shape
    ff = wg.shape[2]
    tm = _largest_tile(n, 512, _SUBLANES)
    tf = _largest_tile(ff, 512, _LANES)
    return pl.pallas_call(
        _ffn_kernel,
        grid=(n // tm, ff // tf),
        in_specs=[
            pl.BlockSpec((tm, d), lambda i, f: (i, 0)),
            pl.BlockSpec((1, d), lambda i, f: (0, 0)),
            pl.BlockSpec((None, d, tf), lambda i, f: (layer, 0, f)),
            pl.BlockSpec((None, d, tf), lambda i, f: (layer, 0, f)),
            pl.BlockSpec((None, tf, d), lambda i, f: (layer, f, 0)),
        ],
        out_specs=pl.BlockSpec((tm, d), lambda i, f: (i, 0)),
        out_shape=jax.ShapeDtypeStruct((n, d), _F32),
        scratch_shapes=[pltpu.VMEM((tm, d), _BF16)],
        compiler_params=_params("parallel", "arbitrary"),
        name="ffn_dense",
    )(h, g2, wg, wu, wd)


_R_E1, _R_E2, _R_RANK1, _R_RANK2, _R_W1, _R_W2 = range(6)


def _router_kernel(h_ref, g_ref, wr_ref, br_ref, xp_ref, info_ref, cnt_ref, carry_sc, *, n_exp):
    @pl.when(pl.program_id(0) == 0)
    def _():
        carry_sc[...] = jnp.zeros(carry_sc.shape, _F32)

    x = h_ref[...]
    xn = x * _rms_scale(x) * g_ref[...]
    tm, d = xn.shape

    xb = xn.astype(_BF16).astype(_F32)
    lo = lax.bitcast_convert_type(xb[:, :d // 2], jnp.int32)
    hi = lax.bitcast_convert_type(xb[:, d // 2:], jnp.int32)
    xp_ref[...] = lax.shift_right_logical(lo, 16) | hi

    logits = jnp.dot(xn, wr_ref[...], preferred_element_type=_F32,
                     precision=lax.Precision.HIGHEST) + br_ref[...]
    lane = lax.broadcasted_iota(jnp.int32, logits.shape, 1)
    logits = jnp.where(lane < n_exp, logits, -jnp.inf)
    v1 = jnp.max(logits, axis=-1, keepdims=True)
    i1 = jnp.min(jnp.where(logits == v1, lane, _LANES), axis=-1, keepdims=True)
    pick1 = lane == i1
    rest = jnp.where(pick1, -jnp.inf, logits)
    v2 = jnp.max(rest, axis=-1, keepdims=True)
    i2 = jnp.min(jnp.where(rest == v2, lane, _LANES), axis=-1, keepdims=True)
    pick2 = lane == i2
    e = jnp.exp(v2 - v1)
    w1 = 1.0 / (1.0 + e)
    w2 = e / (1.0 + e)

    cnt = (pick1 | pick2).astype(_F32)
    row = lax.broadcasted_iota(jnp.int32, (tm, tm), 0)
    col = lax.broadcasted_iota(jnp.int32, (tm, tm), 1)
    below = (row > col).astype(_BF16)
    before = jnp.dot(below, cnt.astype(_BF16), preferred_element_type=_F32) + carry_sc[...]
    rank1 = jnp.sum(jnp.where(pick1, before, 0.0), axis=-1, keepdims=True)
    rank2 = jnp.sum(jnp.where(pick2, before, 0.0), axis=-1, keepdims=True)
    carry_sc[...] += jnp.sum(cnt, axis=0, keepdims=True)

    info = jnp.zeros(logits.shape, _F32)
    for slot, val in ((_R_E1, i1.astype(_F32)), (_R_E2, i2.astype(_F32)), (_R_RANK1, rank1),
                      (_R_RANK2, rank2), (_R_W1, w1), (_R_W2, w2)):
        info = jnp.where(lane == slot, val, info)
    info_ref[...] = info
    cnt_ref[...] = jnp.broadcast_to(carry_sc[...], cnt_ref.shape)


def _router(h, g2, wr_pad, br_pad, n_exp):
    n, d = h.shape
    tm = _largest_tile(n, 512, _SUBLANES)
    kern = functools.partial(_router_kernel, n_exp=n_exp)
    return pl.pallas_call(
        kern,
        grid=(n // tm,),
        in_specs=[
            pl.BlockSpec((tm, d), lambda i: (i, 0)),
            pl.BlockSpec((1, d), lambda i: (0, 0)),
            pl.BlockSpec((d, _LANES), lambda i: (0, 0)),
            pl.BlockSpec((1, _LANES), lambda i: (0, 0)),
        ],
        out_specs=[
            pl.BlockSpec((tm, d // 2), lambda i: (i, 0)),
            pl.BlockSpec((tm, _LANES), lambda i: (i, 0)),
            pl.BlockSpec((_SUBLANES, _LANES), lambda i: (0, 0)),
        ],
        out_shape=[
            jax.ShapeDtypeStruct((n, d // 2), jnp.int32),
            jax.ShapeDtypeStruct((n, _LANES), _F32),
            jax.ShapeDtypeStruct((_SUBLANES, _LANES), _F32),
        ],
        scratch_shapes=[pltpu.VMEM((1, _LANES), _F32)],
        compiler_params=_params("arbitrary"),
        name="router",
    )(h, g2, wr_pad, br_pad)


_DMA_ISSUE_UNROLL = 8


def _row_copy(src_ref, src_row, dst_ref, dst_row, sem):
    return pltpu.make_async_copy(src_ref.at[pl.ds(src_row, 1)], dst_ref.at[pl.ds(dst_row, 1)], sem)


def _dispatch_kernel(pos_ref, xp_ref, zeros_ref, xs_ref, sem, *, tc):
    del zeros_ref

    def start(t, carry):
        for k in range(_TOP_K):
            _row_copy(xp_ref, t, xs_ref, pos_ref[_TOP_K * t + k], sem).start()
        return carry

    lax.fori_loop(0, tc, start, 0, unroll=_DMA_ISSUE_UNROLL)
    for _ in range(_TOP_K):
        pltpu.make_async_copy(xp_ref, xs_ref.at[pl.ds(0, tc)], sem).wait()


def _dispatch(pos, xp, rows):
    n, half = xp.shape
    tc = _largest_tile(n, 512, _SUBLANES)
    kern = functools.partial(_dispatch_kernel, tc=tc)
    return pl.pallas_call(
        kern,
        grid=(n // tc,),
        in_specs=[
            pl.BlockSpec((_TOP_K * tc,), lambda i: (i,), memory_space=pltpu.SMEM),
            pl.BlockSpec((tc, half), lambda i: (i, 0)),
            pl.BlockSpec(memory_space=pl.ANY),
        ],
        out_specs=pl.BlockSpec(memory_space=pl.ANY),
        out_shape=jax.ShapeDtypeStruct((rows, half), jnp.int32),
        scratch_shapes=[pltpu.SemaphoreType.DMA(())],
        input_output_aliases={2: 0},
        compiler_params=_params("arbitrary"),
        name="moe_dispatch",
    )(pos, xp, jnp.zeros((rows, half), jnp.int32))


def _expert_changed(te_ref, i):
    return (i == 0) | (te_ref[i] != te_ref[jnp.maximum(i - 1, 0)])


def _moe_up_kernel(te_ref, nact_ref, xs_ref, wg_ref, wu_ref, act_ref, wg_sc, wu_sc):
    i = pl.program_id(1)
    active = i < nact_ref[0]

    @pl.when(active & _expert_changed(te_ref, i))
    def _():
        wg_sc[...] = wg_ref[...].astype(_BF16)
        wu_sc[...] = wu_ref[...].astype(_BF16)

    @pl.when(active)
    def _():
        packed = xs_ref[...]
        lo = lax.bitcast_convert_type(lax.shift_left(packed, 16), _F32)
        hi = lax.bitcast_convert_type(packed & jnp.int32(-65536), _F32)
        x = jnp.concatenate([lo, hi], axis=1).astype(_BF16)
        a = jnp.dot(x, wg_sc[...], preferred_element_type=_F32)
        b = jnp.dot(x, wu_sc[...], preferred_element_type=_F32)
        act_ref[...] = (a * jax.nn.sigmoid(a) * b).astype(act_ref.dtype)

    @pl.when(jnp.logical_not(active))
    def _():
        act_ref[...] = jnp.zeros(act_ref.shape, act_ref.dtype)


def _moe_down_kernel(te_ref, nact_ref, act_ref, wd_ref, y_ref, wd_sc):
    i = pl.program_id(1)
    active = i < nact_ref[0]

    @pl.when(active & _expert_changed(te_ref, i))
    def _():
        wd_sc[...] = wd_ref[...].astype(_BF16)

    @pl.when(active)
    def _():
        y_ref[...] = jnp.dot(act_ref[...], wd_sc[...], preferred_element_type=_F32)

    @pl.when(jnp.logical_not(active))
    def _():
        y_ref[...] = jnp.zeros(y_ref.shape, y_ref.dtype)


def _moe_ffn(tile_expert, n_active, xs, wg, wu, wd, tm):
    rows, half = xs.shape
    d = 2 * half
    ff = wg.shape[2]
    tf = _largest_tile(ff, 512, _LANES)
    tn = _largest_tile(d, 512, _LANES)
    n_tiles = rows // tm

    def row_idx(i, nact):
        return jnp.maximum(jnp.minimum(i, nact[0] - 1), 0)

    up_spec = pltpu.PrefetchScalarGridSpec(
        num_scalar_prefetch=2,
        grid=(ff // tf, n_tiles),
        in_specs=[
            pl.BlockSpec((tm, half), lambda f, i, te, na: (row_idx(i, na), 0)),
            pl.BlockSpec((None, d, tf), lambda f, i, te, na: (te[i], 0, f)),
            pl.BlockSpec((None, d, tf), lambda f, i, te, na: (te[i], 0, f)),
        ],
        out_specs=pl.BlockSpec((tm, tf), lambda f, i, te, na: (i, f)),
        scratch_shapes=[pltpu.VMEM((d, tf), _BF16), pltpu.VMEM((d, tf), _BF16)],
    )
    act = pl.pallas_call(
        _moe_up_kernel,
        grid_spec=up_spec,
        out_shape=jax.ShapeDtypeStruct((rows, ff), _BF16),
        compiler_params=_params("arbitrary", "arbitrary"),
        name="moe_up",
    )(tile_expert, n_active, xs, wg, wu)

    down_spec = pltpu.PrefetchScalarGridSpec(
        num_scalar_prefetch=2,
        grid=(d // tn, n_tiles),
        in_specs=[
            pl.BlockSpec((tm, ff), lambda n, i, te, na: (row_idx(i, na), 0)),
            pl.BlockSpec((None, ff, tn), lambda n, i, te, na: (te[i], 0, n)),
        ],
        out_specs=pl.BlockSpec((tm, tn), lambda n, i, te, na: (i, n)),
        scratch_shapes=[pltpu.VMEM((ff, tn), _BF16)],
    )
    return pl.pallas_call(
        _moe_down_kernel,
        grid_spec=down_spec,
        out_shape=jax.ShapeDtypeStruct((rows, d), _F32),
        compiler_params=_params("arbitrary", "arbitrary"),
        name="moe_down",
    )(tile_expert, n_active, act, wd)


def _combine_kernel(pos_ref, info_ref, h_ref, y_ref, g_ref, out_ref, buf, sem, *, tc, final_norm):
    def start(t, carry):
        for k in range(_TOP_K):
            pltpu.make_async_copy(y_ref.at[pl.ds(pos_ref[_TOP_K * t + k], 1)],
                                  buf.at[k, pl.ds(t, 1)], sem).start()
        return carry

    lax.fori_loop(0, tc, start, 0, unroll=_DMA_ISSUE_UNROLL)
    for k in range(_TOP_K):
        pltpu.make_async_copy(y_ref.at[pl.ds(0, tc)], buf.at[k], sem).wait()

    info = info_ref[...]
    w1 = info[:, _R_W1:_R_W1 + 1]
    w2 = info[:, _R_W2:_R_W2 + 1]
    out = h_ref[...] + (w1 * buf[0] + w2 * buf[1])
    if final_norm:
        out = out * _rms_scale(out) * g_ref[...]
    out_ref[...] = out


def _combine(pos, info, h, y, final_g, final_norm):
    n, d = h.shape
    tc = _largest_tile(n, 512, _SUBLANES)
    kern = functools.partial(_combine_kernel, tc=tc, final_norm=final_norm)
    return pl.pallas_call(
        kern,
        grid=(n // tc,),
        in_specs=[
            pl.BlockSpec((_TOP_K * tc,), lambda i: (i,), memory_space=pltpu.SMEM),
            pl.BlockSpec((tc, _LANES), lambda i: (i, 0)),
            pl.BlockSpec((tc, d), lambda i: (i, 0)),
            pl.BlockSpec(memory_space=pl.ANY),
            pl.BlockSpec((1, d), lambda i: (0, 0)),
        ],
        out_specs=pl.BlockSpec((tc, d), lambda i: (i, 0)),
        out_shape=jax.ShapeDtypeStruct((n, d), _F32),
        scratch_shapes=[pltpu.VMEM((_TOP_K, tc, d), _F32), pltpu.SemaphoreType.DMA(())],
        compiler_params=_params("arbitrary"),
        name="moe_combine",
    )(pos, info, h, y, final_g)


def _final_norm_kernel(h_ref, g_ref, out_ref):
    x = h_ref[...]
    out_ref[...] = x * _rms_scale(x) * g_ref[...]


def _final_norm(h, g):
    n, d = h.shape
    tm = _largest_tile(n, 512, _SUBLANES)
    return pl.pallas_call(
        _final_norm_kernel,
        grid=(n // tm,),
        in_specs=[pl.BlockSpec((tm, d), lambda i: (i, 0)), pl.BlockSpec((1, d), lambda i: (0, 0))],
        out_specs=pl.BlockSpec((tm, d), lambda i: (i, 0)),
        out_shape=jax.ShapeDtypeStruct((n, d), _F32),
        compiler_params=_params("parallel"),
        name="final_norm",
    )(h, g)


_MOE_ROW_TILE = 512


def _moe(h, g2, wr, br, wg, wu, wd, layer, final_g, final_norm):
    n, d = h.shape
    n_exp = wr.shape[1]
    tm = _largest_tile(n, _MOE_ROW_TILE, _SUBLANES)
    wr_pad = jnp.zeros((d, _LANES), _F32).at[:, :n_exp].set(wr)
    br_pad = jnp.zeros((1, _LANES), _F32).at[0, :n_exp].set(br)
    xp, info, cnt = _router(h, g2, wr_pad, br_pad, n_exp)

    counts = cnt[0, :n_exp].astype(jnp.int32)
    tiles_per = (counts + tm - 1) // tm
    tile_end = jnp.cumsum(tiles_per)
    starts = (tile_end - tiles_per) * tm
    n_active = tile_end[-1:]
    n_tiles = (_TOP_K * n) // tm + n_exp
    tile_ids = jnp.minimum(jnp.arange(n_tiles, dtype=jnp.int32), n_active - 1)
    tile_expert = jnp.sum((tile_ids[:, None] >= tile_end[None, :]).astype(jnp.int32), axis=1)
    tile_expert = tile_expert + layer * n_exp
    experts = info[:, _R_E1:_R_E2 + 1].astype(jnp.int32)
    ranks = info[:, _R_RANK1:_R_RANK2 + 1].astype(jnp.int32)
    onehot = experts[:, :, None] == jnp.arange(n_exp, dtype=jnp.int32)[None, None, :]
    pos = (jnp.sum(jnp.where(onehot, starts[None, None, :], 0), axis=-1) + ranks).reshape(-1)

    xs = _dispatch(pos, xp, n_tiles * tm)
    y = _moe_ffn(tile_expert, n_active, xs, wg, wu, wd, tm)
    return _combine(pos, info, h, y, final_g, final_norm)


def _rope_tables(seq, hd):
    axis_dim = hd // 2
    nfreq = axis_dim // 2
    rows = seq // _GRID_W
    row = jnp.broadcast_to(jnp.arange(rows)[:, None], (rows, _GRID_W)).reshape(seq).astype(_F32)
    col = jnp.broadcast_to(jnp.arange(_GRID_W)[None, :], (rows, _GRID_W)).reshape(seq).astype(_F32)
    inv = _ROPE_THETA ** (-jnp.arange(nfreq, dtype=_F32) * 2.0 / axis_dim)
    ar = row[:, None] * inv
    ac = col[:, None] * inv
    ang = jnp.concatenate([ar, ar, ac, ac], axis=-1)
    lane = jnp.arange(hd)
    sign = jnp.where((lane % axis_dim) < nfreq, -1.0, 1.0).astype(_F32)
    return jnp.cos(ang), jnp.sin(ang) * sign


def kernel(x, norm1_g, w_in, q_norm_g, k_norm_g, w_attn_o, w_pool_mix, pool_scale, w_pool_o,
           w_out, norm2_g, ffn_w_gate, ffn_w_up, ffn_w_down, router_w, router_b,
           moe_w_gate, moe_w_up, moe_w_down, final_g):
    batch, seq, d = x.shape
    depth = w_in.shape[0]
    hd = q_norm_g.shape[1]
    q_dim = w_attn_o.shape[1]
    p_dim = pool_scale.shape[1]
    kv_dim = (w_in.shape[2] - q_dim - p_dim - 2 * d) // 2
    dims = (q_dim, kv_dim, p_dim, hd, seq)
    assert seq % _GRID_W == 0 and q_dim % kv_dim == 0 and kv_dim % hd == 0

    cos, sin_signed = _rope_tables(seq, hd)
    bf = lambda w: w.astype(_BF16)
    row = lambda v: v.reshape(1, -1)
    experts = lambda w: w.reshape((-1,) + w.shape[2:])

    w_in, w_attn_o, w_pool_o, w_out = bf(w_in), bf(w_attn_o), bf(w_pool_o), bf(w_out)
    ffn_w_gate, ffn_w_up, ffn_w_down = bf(ffn_w_gate), bf(ffn_w_up), bf(ffn_w_down)
    moe_w_gate, moe_w_up, moe_w_down = experts(moe_w_gate), experts(moe_w_up), experts(moe_w_down)

    h = x.reshape(batch * seq, d)
    fg = row(final_g)
    for l in range(depth):
        q, k, vt, u, gates = _in_proj(h, row(norm1_g[l]), w_in, l, cos, sin_signed,
                                      row(q_norm_g[l]), row(k_norm_g[l]), dims)
        o = _attention(q, k, vt, dims, batch)
        pooled = _pool(u, bf(w_pool_mix[l]), row(pool_scale[l]), seq)
        merged = _merge(o, pooled, w_attn_o, w_pool_o, l, gates)
        h = _out_proj(merged, w_out, l, h)
        i = l // 2
        if l % 2 == 0:
            h = _ffn(h, row(norm2_g[l]), ffn_w_gate, ffn_w_up, ffn_w_down, i)
        else:
            last = l == depth - 1
            h = _moe(h, row(norm2_g[l]), router_w[i], router_b[i], moe_w_gate, moe_w_up,
                     moe_w_down, i, fg, last)
    if depth % 2 == 1 or depth == 0:
        h = _final_norm(h, fg)
    return h.reshape(batch, seq, d)
```

```python
import functools
import math

import jax
import jax.numpy as jnp
from jax import lax
from jax.experimental import pallas as pl
from jax.experimental.pallas import tpu as pltpu

_GRID_W = 64
_ROPE_THETA = 10000.0
_POOL_WINDOWS = (2, 4, 8, 16)
_TOP_K = 2
_NORM_EPS = 1e-6
_LOG2_E = 1.4426950408889634

_LANES = 128
_SUBLANES = 8
_SUM_ROWS = 16
_V7X_VMEM_LIMIT_BYTES = 56 * 1024 * 1024

_F32 = jnp.float32
_BF16 = jnp.bfloat16


def _params(*semantics):
    return pltpu.CompilerParams(dimension_semantics=semantics,
                                vmem_limit_bytes=_V7X_VMEM_LIMIT_BYTES)


def _largest_tile(n, cap, quantum):
    t = min(cap, n)
    t -= t % quantum
    while t > quantum and n % t:
        t -= quantum
    assert t >= quantum and n % t == 0, (n, cap, quantum)
    return t


def _rms_scale(x):
    return lax.rsqrt(jnp.mean(x * x, axis=-1, keepdims=True) + _NORM_EPS)


_ROW_PARTS = 4


def _in_proj_kernel(h_ref, g_ref, w_ref, cos_ref, sin_ref, qg_ref, kg_ref,
                    q_ref, k_ref, vt_ref, u_ref, gate_ref, xn_sc, acc_sc,
                    *, nq, nk, nv, nu, ng, hd, tk, qscale):
    j = pl.program_id(1)
    ncol = nq + nk + nv + nu + ng
    tn = w_ref.shape[1]
    heads = tn // hd

    tm = xn_sc.shape[0]

    def matmul_rows(slot, rows):
        acc_sc[slot, rows, :] = jnp.dot(xn_sc[rows, :], w_ref[...], preferred_element_type=_F32)

    def norm_rope(acc, rows, gain, scale):
        cos = cos_ref[rows, :]
        sin = sin_ref[rows, :]
        lane = lax.broadcasted_iota(jnp.int32, (1, hd), 1)
        first_half = (lane % (hd // 2)) < (hd // 4)
        outs = []
        for hh in range(heads):
            xh = acc[:, hh * hd:(hh + 1) * hd]
            xh = xh * _rms_scale(xh) * gain
            partner = jnp.where(first_half,
                                pltpu.roll(xh, hd - hd // 4, 1),
                                pltpu.roll(xh, hd // 4, 1))
            outs.append(((xh * cos + partner * sin) * scale).astype(_BF16))
        return outs

    def finish_q(acc, rows):
        q_ref[rows, :] = jnp.concatenate(norm_rope(acc, rows, qg_ref[...], qscale), axis=1)

    def finish_k(acc, rows):
        for hh, kh in enumerate(norm_rope(acc, rows, kg_ref[...], 1.0)):
            k_ref[hh, rows, :] = kh

    def finish_v(acc, rows):
        sum_rows = (lax.broadcasted_iota(jnp.int32, (_SUM_ROWS, tk), 0) == 0).astype(_BF16)
        c = rows.start // tk
        for hh in range(heads):
            vt_ref[hh, c, 0:hd, :] = acc[:, hh * hd:(hh + 1) * hd].T.astype(_BF16)
            vt_ref[hh, c, hd:hd + _SUM_ROWS, :] = sum_rows

    def finish_u(acc, rows):
        u_ref[rows, :] = acc

    def finish_gate(acc, rows):
        gate_ref[rows, :] = (0.5 * jnp.tanh(0.5 * acc) + 0.5).astype(_BF16)

    def row_parts(parts):
        return [slice(r * (tm // parts), (r + 1) * (tm // parts)) for r in range(parts)]

    @pl.when(j == 0)
    def _():
        for rows in row_parts(_ROW_PARTS):
            x = h_ref[rows, :]
            xn_sc[rows, :] = (x * _rms_scale(x) * g_ref[...]).astype(_BF16)
            matmul_rows(0, rows)

    t = j - 1
    lo = 0
    for cnt, finish, parts in ((nq, finish_q, 1), (nk, finish_k, 1),
                               (nv, finish_v, tm // tk), (nu, finish_u, _ROW_PARTS),
                               (ng, finish_gate, _ROW_PARTS)):
        @pl.when((t >= lo) & (t < lo + cnt) & (j < ncol))
        def _(finish=finish, parts=parts):
            for rows in row_parts(parts):
                finish(acc_sc[t % 2, rows, :], rows)
                matmul_rows(j % 2, rows)
        lo += cnt

    @pl.when(j == ncol)
    def _():
        finish_gate(acc_sc[t % 2], slice(0, tm))


def _attn_tiles(seq):
    tq = _largest_tile(seq, 1024, _LANES)
    tk = _largest_tile(seq // 2, 512, _LANES)
    assert (seq // tk) % 2 == 0
    return tq, tk


def _in_proj(h, g1, w, layer, cos, sin_signed, qg, kg, dims):
    n, d = h.shape
    q_dim, kv_dim, p_dim, hd, seq = dims
    _, tk = _attn_tiles(seq)
    cols = w.shape[2]
    tm = _largest_tile(seq, 1024, _SUBLANES)
    tn = _largest_tile(_gcd_all(q_dim, kv_dim, p_dim, d), 512, hd)
    nq, nk, nv, nu, ng = q_dim // tn, kv_dim // tn, kv_dim // tn, p_dim // tn, 2 * d // tn
    ncol = nq + nk + nv + nu + ng
    assert ncol * tn == cols and tm % tk == 0
    s_tiles = seq // tm
    heads = tn // hd

    def tile(j, lo, cnt):
        return jnp.clip(j - 1 - lo, 0, cnt - 1)

    kern = functools.partial(_in_proj_kernel, nq=nq, nk=nk, nv=nv, nu=nu, ng=ng, hd=hd, tk=tk,
                             qscale=_LOG2_E * float(hd) ** -0.5)
    return pl.pallas_call(
        kern,
        grid=(n // tm, ncol + 1),
        in_specs=[
            pl.BlockSpec((tm, d), lambda i, j: (i, 0)),
            pl.BlockSpec((1, d), lambda i, j: (0, 0)),
            pl.BlockSpec((None, d, tn), lambda i, j: (layer, 0, jnp.minimum(j, ncol - 1))),
            pl.BlockSpec((tm, hd), lambda i, j: (i % s_tiles, 0)),
            pl.BlockSpec((tm, hd), lambda i, j: (i % s_tiles, 0)),
            pl.BlockSpec((1, hd), lambda i, j: (0, 0)),
            pl.BlockSpec((1, hd), lambda i, j: (0, 0)),
        ],
        out_specs=[
            pl.BlockSpec((tm, tn), lambda i, j: (i, tile(j, 0, nq))),
            pl.BlockSpec((heads, tm, hd), lambda i, j: (tile(j, nq, nk), i, 0)),
            pl.BlockSpec((heads, tm // tk, hd + _SUM_ROWS, tk),
                         lambda i, j: (tile(j, nq + nk, nv), i, 0, 0)),
            pl.BlockSpec((tm, tn), lambda i, j: (i, tile(j, nq + nk + nv, nu))),
            pl.BlockSpec((tm, tn), lambda i, j: (i, tile(j, nq + nk + nv + nu, ng))),
        ],
        out_shape=[
            jax.ShapeDtypeStruct((n, q_dim), _BF16),
            jax.ShapeDtypeStruct((kv_dim // hd, n, hd), _BF16),
            jax.ShapeDtypeStruct((kv_dim // hd, n // tk, hd + _SUM_ROWS, tk), _BF16),
            jax.ShapeDtypeStruct((n, p_dim), _F32),
            jax.ShapeDtypeStruct((n, 2 * d), _BF16),
        ],
        scratch_shapes=[pltpu.VMEM((tm, d), _BF16), pltpu.VMEM((2, tm, tn), _F32)],
        compiler_params=_params("parallel", "arbitrary"),
        name="in_proj",
    )(h, g1, w, cos, sin_signed, qg, kg)


def _gcd_all(*vals):
    g = 0
    for v in vals:
        g = math.gcd(g, v)
    return g


def _attn_kernel(q_ref, k_ref, vt_ref, o_ref, qs_sc, s_sc, cm_sc, acc_sc,
                 *, groups, hd, tq, tk, seq):
    n_chunks = seq // tk
    cols = groups * tq

    for g in range(groups):
        qs_sc[:, g * tq:(g + 1) * tq] = q_ref[:, g * hd:(g + 1) * hd].astype(_F32).T.astype(_BF16)
    acc_sc[...] = jnp.zeros(acc_sc.shape, _F32)

    def scores(c, slot):
        k_c = k_ref[pl.ds(pl.multiple_of(c * tk, tk), tk), :]
        s = jnp.dot(k_c, qs_sc[...], preferred_element_type=_F32)
        s_sc[slot] = s
        cm_sc[slot] = jnp.max(s, axis=0, keepdims=True)

    def softmax_values(c, slot, m_prev):
        m_new = jnp.maximum(m_prev, cm_sc[slot])
        alpha = jnp.exp2(m_prev - m_new)
        p = jnp.exp2(s_sc[slot] - m_new)
        acc_sc[...] = acc_sc[...] * alpha + jnp.dot(vt_ref[c], p.astype(_BF16),
                                                    preferred_element_type=_F32)
        return m_new

    def pair(j, m):
        c = 2 * j
        scores(c + 1, 1)
        m = softmax_values(c, 0, m)
        scores(c + 2, 0)
        return softmax_values(c + 1, 1, m)

    m = jnp.full((1, cols), -jnp.inf, _F32)
    scores(0, 0)
    m = lax.fori_loop(0, n_chunks // 2 - 1, pair, m)
    scores(n_chunks - 1, 1)
    m = softmax_values(n_chunks - 2, 0, m)
    m = softmax_values(n_chunks - 1, 1, m)

    out_t = acc_sc[0:hd, :] / acc_sc[hd:hd + 1, :]
    for g in range(groups):
        o_ref[:, g * hd:(g + 1) * hd] = out_t[:, g * tq:(g + 1) * tq].T.astype(o_ref.dtype)


def _attention(q, k, vt, dims, batch):
    q_dim, kv_dim, _, hd, seq = dims
    n_kv = kv_dim // hd
    groups = q_dim // kv_dim
    tq, tk = _attn_tiles(seq)
    q3 = q.reshape(batch, seq, q_dim)
    cols = groups * tq
    kern = functools.partial(_attn_kernel, groups=groups, hd=hd, tq=tq, tk=tk, seq=seq)
    o = pl.pallas_call(
        kern,
        grid=(batch, n_kv, seq // tq),
        in_specs=[
            pl.BlockSpec((None, tq, groups * hd), lambda b, kh, i: (b, i, kh)),
            pl.BlockSpec((None, seq, hd), lambda b, kh, i: (kh, b, 0)),
            pl.BlockSpec((None, seq // tk, hd + _SUM_ROWS, tk), lambda b, kh, i: (kh, b, 0, 0)),
        ],
        out_specs=pl.BlockSpec((None, tq, groups * hd), lambda b, kh, i: (b, i, kh)),
        out_shape=jax.ShapeDtypeStruct((batch, seq, q_dim), _BF16),
        scratch_shapes=[
            pltpu.VMEM((hd, cols), _BF16),
            pltpu.VMEM((2, tk, cols), _F32),
            pltpu.VMEM((2, 1, cols), _F32),
            pltpu.VMEM((hd + _SUM_ROWS, cols), _F32),
        ],
        compiler_params=_params("parallel", "parallel", "parallel"),
        name="attention",
    )(q3, k, vt)
    return o.reshape(batch * seq, q_dim)


_POOL_HALO = 8


def _pool_kernel(prev_ref, u_ref, next_ref, wmix_ref, scale_ref, o_ref, ext_sc, run_sc,
                 *, tm, seq, grp):
    halo = _POOL_HALO
    tile = pl.program_id(0) % (seq // tm)
    ext_sc[0:halo, :] = jnp.where(tile == 0, 0.0, prev_ref[...])
    ext_sc[halo:halo + tm, :] = u_ref[...]
    ext_sc[halo + tm:halo + tm + halo, :] = jnp.where(tile == seq // tm - 1, 0.0, next_ref[...])
    t = tile * tm + lax.broadcasted_iota(jnp.int32, (tm, 1), 0)
    for g, w in enumerate(_POOL_WINDOWS):
        sl = slice(g * grp, (g + 1) * grp)
        run = ext_sc[:, sl]
        span = 1
        while span < w:
            run = run + pltpu.roll(run, span, 0)
            span *= 2
        run_sc[...] = run
        end = halo + w // 2 - 1
        wsum = run_sc[end:end + tm, :]
        lo = jnp.maximum(t - w // 2, 0)
        hi = jnp.minimum(t + w // 2 - 1, seq - 1)
        cnt = (hi - lo + 1).astype(_F32)
        dev = wsum / cnt - u_ref[:, sl]
        y = jnp.dot(dev.astype(_BF16), wmix_ref[g], preferred_element_type=_F32)
        o_ref[:, sl] = (y * scale_ref[:, sl]).astype(o_ref.dtype)


def _pool(u, wmix, scale, seq):
    n, p = u.shape
    grp = p // len(_POOL_WINDOWS)
    tm = _largest_tile(seq, 512, _POOL_HALO)
    halo = _POOL_HALO
    hb = tm // halo
    kern = functools.partial(_pool_kernel, tm=tm, seq=seq, grp=grp)
    return pl.pallas_call(
        kern,
        grid=(n // tm,),
        in_specs=[
            pl.BlockSpec((halo, p), lambda i: (jnp.maximum(i * hb - 1, 0), 0)),
            pl.BlockSpec((tm, p), lambda i: (i, 0)),
            pl.BlockSpec((halo, p), lambda i: (jnp.minimum((i + 1) * hb, n // halo - 1), 0)),
            pl.BlockSpec(wmix.shape, lambda i: (0, 0, 0)),
            pl.BlockSpec((1, p), lambda i: (0, 0)),
        ],
        out_specs=pl.BlockSpec((tm, p), lambda i: (i, 0)),
        out_shape=jax.ShapeDtypeStruct((n, p), _BF16),
        scratch_shapes=[pltpu.VMEM((tm + 2 * halo, p), _F32),
                        pltpu.VMEM((tm + 2 * halo, grp), _F32)],
        compiler_params=_params("parallel"),
        name="pool",
    )(u, u, u, wmix, scale)


def _merge_kernel(o_ref, p_ref, wa_ref, wp_ref, ga_ref, gp_ref, out_ref):
    a = jnp.dot(o_ref[...], wa_ref[...], preferred_element_type=_F32)
    p = jnp.dot(p_ref[...], wp_ref[...], preferred_element_type=_F32)
    merged = ga_ref[...].astype(_F32) * a + gp_ref[...].astype(_F32) * p
    out_ref[...] = merged.astype(out_ref.dtype)


def _merge(o, pooled, wa, wp, layer, gates):
    n, q_dim = o.shape
    p_dim = pooled.shape[1]
    d = wa.shape[2]
    tm = _largest_tile(n, 1024, _SUBLANES)
    tn = _largest_tile(d, 1024, _LANES)
    nj = d // tn
    return pl.pallas_call(
        _merge_kernel,
        grid=(n // tm, nj),
        in_specs=[
            pl.BlockSpec((tm, q_dim), lambda i, j: (i, 0)),
            pl.BlockSpec((tm, p_dim), lambda i, j: (i, 0)),
            pl.BlockSpec((None, q_dim, tn), lambda i, j: (layer, 0, j)),
            pl.BlockSpec((None, p_dim, tn), lambda i, j: (layer, 0, j)),
            pl.BlockSpec((tm, tn), lambda i, j: (i, j)),
            pl.BlockSpec((tm, tn), lambda i, j: (i, nj + j)),
        ],
        out_specs=pl.BlockSpec((tm, tn), lambda i, j: (i, j)),
        out_shape=jax.ShapeDtypeStruct((n, d), _BF16),
        compiler_params=_params("parallel", "arbitrary"),
        name="merge",
    )(o, pooled, wa, wp, gates, gates)


def _out_proj_kernel(a_ref, w_ref, res_ref, out_ref):
    out_ref[...] = res_ref[...] + jnp.dot(a_ref[...], w_ref[...], preferred_element_type=_F32)


def _out_proj(merged, w, layer, h):
    n, d = h.shape
    tm = _largest_tile(n, 1024, _SUBLANES)
    tn = _largest_tile(d, 1024, _LANES)
    return pl.pallas_call(
        _out_proj_kernel,
        grid=(n // tm, d // tn),
        in_specs=[
            pl.BlockSpec((tm, merged.shape[1]), lambda i, j: (i, 0)),
            pl.BlockSpec((None, merged.shape[1], tn), lambda i, j: (layer, 0, j)),
            pl.BlockSpec((tm, tn), lambda i, j: (i, j)),
        ],
        out_specs=pl.BlockSpec((tm, tn), lambda i, j: (i, j)),
        out_shape=jax.ShapeDtypeStruct((n, d), _F32),
        compiler_params=_params("parallel", "arbitrary"),
        name="out_proj",
    )(merged, w, h)


def _swiglu_step(xn, wg_ref, wu_ref, wd_ref):
    a = jnp.dot(xn, wg_ref[...], preferred_element_type=_F32)
    b = jnp.dot(xn, wu_ref[...], preferred_element_type=_F32)
    act = (a * jax.nn.sigmoid(a) * b).astype(_BF16)
    return jnp.dot(act, wd_ref[...], preferred_element_type=_F32)


def _ffn_kernel(h_ref, g_ref, wg_ref, wu_ref, wd_ref, out_ref, xn_sc):
    @pl.when(pl.program_id(1) == 0)
    def _():
        x = h_ref[...]
        xn_sc[...] = (x * _rms_scale(x) * g_ref[...]).astype(_BF16)
        out_ref[...] = x

    out_ref[...] += _swiglu_step(xn_sc[...], wg_ref, wu_ref, wd_ref)


def _ffn(h, g2, wg, wu, wd, layer):
    n, d = h.shape
    ff = wg.shape[2]
    tm = _largest_tile(n, 512, _SUBLANES)
    tf = _largest_tile(ff, 512, _LANES)
    return pl.pallas_call(
        _ffn_kernel,
        grid=(n // tm, ff // tf),
        in_specs=[
            pl.BlockSpec((tm, d), lambda i, f: (i, 0)),
            pl.BlockSpec((1, d), lambda i, f: (0, 0)),
            pl.BlockSpec((None, d, tf), lambda i, f: (layer, 0, f)),
            pl.BlockSpec((None, d, tf), lambda i, f: (layer, 0, f)),
            pl.BlockSpec((None, tf, d), lambda i, f: (layer, f, 0)),
        ],
        out_specs=pl.BlockSpec((tm, d), lambda i, f: (i, 0)),
        out_shape=jax.ShapeDtypeStruct((n, d), _F32),
        scratch_shapes=[pltpu.VMEM((tm, d), _BF16)],
        compiler_params=_params("parallel", "arbitrary"),
        name="ffn_dense",
    )(h, g2, wg, wu, wd)


_R_E1, _R_E2, _R_RANK1, _R_RANK2, _R_W1, _R_W2 = range(6)


def _router_kernel(h_ref, g_ref, wr_ref, br_ref, xp_ref, info_ref, cnt_ref, carry_sc, *, n_exp):
    @pl.when(pl.program_id(0) == 0)
    def _():
        carry_sc[...] = jnp.zeros(carry_sc.shape, _F32)

    x = h_ref[...]
    xn = x * _rms_scale(x) * g_ref[...]
    tm, d = xn.shape

    xb = xn.astype(_BF16).astype(_F32)
    lo = lax.bitcast_convert_type(xb[:, :d // 2], jnp.int32)
    hi = lax.bitcast_convert_type(xb[:, d // 2:], jnp.int32)
    xp_ref[...] = lax.shift_right_logical(lo, 16) | hi

    wr = wr_ref[...]
    x_hi = xn.astype(_BF16)
    x_lo = (xn - x_hi.astype(_F32)).astype(_BF16)
    w_hi = wr.astype(_BF16)
    w_lo = (wr - w_hi.astype(_F32)).astype(_BF16)
    logits = (jnp.dot(x_hi, w_hi, preferred_element_type=_F32)
              + jnp.dot(x_lo, w_hi, preferred_element_type=_F32)
              + jnp.dot(x_hi, w_lo, preferred_element_type=_F32)) + br_ref[...]
    lane = lax.broadcasted_iota(jnp.int32, logits.shape, 1)
    logits = jnp.where(lane < n_exp, logits, -jnp.inf)
    v1 = jnp.max(logits, axis=-1, keepdims=True)
    i1 = jnp.min(jnp.where(logits == v1, lane, _LANES), axis=-1, keepdims=True)
    pick1 = lane == i1
    rest = jnp.where(pick1, -jnp.inf, logits)
    v2 = jnp.max(rest, axis=-1, keepdims=True)
    i2 = jnp.min(jnp.where(rest == v2, lane, _LANES), axis=-1, keepdims=True)
    pick2 = lane == i2
    e = jnp.exp(v2 - v1)
    w1 = 1.0 / (1.0 + e)
    w2 = e / (1.0 + e)

    cnt = (pick1 | pick2).astype(_F32)
    row = lax.broadcasted_iota(jnp.int32, (tm, tm), 0)
    col = lax.broadcasted_iota(jnp.int32, (tm, tm), 1)
    below = (row > col).astype(_BF16)
    before = jnp.dot(below, cnt.astype(_BF16), preferred_element_type=_F32) + carry_sc[...]
    rank1 = jnp.sum(jnp.where(pick1, before, 0.0), axis=-1, keepdims=True)
    rank2 = jnp.sum(jnp.where(pick2, before, 0.0), axis=-1, keepdims=True)
    carry_sc[...] += jnp.sum(cnt, axis=0, keepdims=True)

    info = jnp.zeros(logits.shape, _F32)
    for slot, val in ((_R_E1, i1.astype(_F32)), (_R_E2, i2.astype(_F32)), (_R_RANK1, rank1),
                      (_R_RANK2, rank2), (_R_W1, w1), (_R_W2, w2)):
        info = jnp.where(lane == slot, val, info)
    info_ref[...] = info
    cnt_ref[...] = jnp.broadcast_to(carry_sc[...], cnt_ref.shape)


def _router(h, g2, wr_pad, br_pad, n_exp):
    n, d = h.shape
    tm = _largest_tile(n, 512, _SUBLANES)
    kern = functools.partial(_router_kernel, n_exp=n_exp)
    return pl.pallas_call(
        kern,
        grid=(n // tm,),
        in_specs=[
            pl.BlockSpec((tm, d), lambda i: (i, 0)),
            pl.BlockSpec((1, d), lambda i: (0, 0)),
            pl.BlockSpec((d, _LANES), lambda i: (0, 0)),
            pl.BlockSpec((1, _LANES), lambda i: (0, 0)),
        ],
        out_specs=[
            pl.BlockSpec((tm, d // 2), lambda i: (i, 0)),
            pl.BlockSpec((tm, _LANES), lambda i: (i, 0)),
            pl.BlockSpec((_SUBLANES, _LANES), lambda i: (0, 0)),
        ],
        out_shape=[
            jax.ShapeDtypeStruct((n, d // 2), jnp.int32),
            jax.ShapeDtypeStruct((n, _LANES), _F32),
            jax.ShapeDtypeStruct((_SUBLANES, _LANES), _F32),
        ],
        scratch_shapes=[pltpu.VMEM((1, _LANES), _F32)],
        compiler_params=_params("arbitrary"),
        name="router",
    )(h, g2, wr_pad, br_pad)


_DMA_ISSUE_UNROLL = 8


def _row_copy(src_ref, src_row, dst_ref, dst_row, sem):
    return pltpu.make_async_copy(src_ref.at[pl.ds(src_row, 1)], dst_ref.at[pl.ds(dst_row, 1)], sem)


def _dispatch_kernel(pos_ref, xp_ref, zeros_ref, xs_ref, sem, *, tc):
    del zeros_ref

    def start(t, carry):
        for k in range(_TOP_K):
            _row_copy(xp_ref, t, xs_ref, pos_ref[_TOP_K * t + k], sem).start()
        return carry

    lax.fori_loop(0, tc, start, 0, unroll=_DMA_ISSUE_UNROLL)
    for _ in range(_TOP_K):
        pltpu.make_async_copy(xp_ref, xs_ref.at[pl.ds(0, tc)], sem).wait()


def _dispatch(pos, xp, rows):
    n, half = xp.shape
    tc = _largest_tile(n, 512, _SUBLANES)
    kern = functools.partial(_dispatch_kernel, tc=tc)
    return pl.pallas_call(
        kern,
        grid=(n // tc,),
        in_specs=[
            pl.BlockSpec((_TOP_K * tc,), lambda i: (i,), memory_space=pltpu.SMEM),
            pl.BlockSpec((tc, half), lambda i: (i, 0)),
            pl.BlockSpec(memory_space=pl.ANY),
        ],
        out_specs=pl.BlockSpec(memory_space=pl.ANY),
        out_shape=jax.ShapeDtypeStruct((rows, half), jnp.int32),
        scratch_shapes=[pltpu.SemaphoreType.DMA(())],
        input_output_aliases={2: 0},
        compiler_params=_params("arbitrary"),
        name="moe_dispatch",
    )(pos, xp, jnp.zeros((rows, half), jnp.int32))


def _expert_changed(te_ref, i):
    return (i == 0) | (te_ref[i] != te_ref[jnp.maximum(i - 1, 0)])


def _moe_up_kernel(te_ref, nact_ref, xs_ref, wg_ref, wu_ref, act_ref, wg_sc, wu_sc):
    i = pl.program_id(1)
    active = i < nact_ref[0]

    @pl.when(active & _expert_changed(te_ref, i))
    def _():
        wg_sc[...] = wg_ref[...].astype(_BF16)
        wu_sc[...] = wu_ref[...].astype(_BF16)

    @pl.when(active)
    def _():
        packed = xs_ref[...]
        lo = lax.bitcast_convert_type(lax.shift_left(packed, 16), _F32)
        hi = lax.bitcast_convert_type(packed & jnp.int32(-65536), _F32)
        x = jnp.concatenate([lo, hi], axis=1).astype(_BF16)
        a = jnp.dot(x, wg_sc[...], preferred_element_type=_F32)
        b = jnp.dot(x, wu_sc[...], preferred_element_type=_F32)
        act_ref[...] = (a * jax.nn.sigmoid(a) * b).astype(act_ref.dtype)

    @pl.when(jnp.logical_not(active))
    def _():
        act_ref[...] = jnp.zeros(act_ref.shape, act_ref.dtype)


def _moe_down_kernel(te_ref, nact_ref, act_ref, wd_ref, y_ref, wd_sc):
    i = pl.program_id(1)
    active = i < nact_ref[0]

    @pl.when(active & _expert_changed(te_ref, i))
    def _():
        wd_sc[...] = wd_ref[...].astype(_BF16)

    @pl.when(active)
    def _():
        y_ref[...] = jnp.dot(act_ref[...], wd_sc[...], preferred_element_type=_F32)

    @pl.when(jnp.logical_not(active))
    def _():
        y_ref[...] = jnp.zeros(y_ref.shape, y_ref.dtype)


def _moe_ffn(tile_expert, n_active, xs, wg, wu, wd, tm):
    rows, half = xs.shape
    d = 2 * half
    ff = wg.shape[2]
    tf = _largest_tile(ff, 512, _LANES)
    tn = _largest_tile(d, 512, _LANES)
    n_tiles = rows // tm

    def row_idx(i, nact):
        return jnp.maximum(jnp.minimum(i, nact[0] - 1), 0)

    up_spec = pltpu.PrefetchScalarGridSpec(
        num_scalar_prefetch=2,
        grid=(ff // tf, n_tiles),
        in_specs=[
            pl.BlockSpec((tm, half), lambda f, i, te, na: (row_idx(i, na), 0)),
            pl.BlockSpec((None, d, tf), lambda f, i, te, na: (te[i], 0, f)),
            pl.BlockSpec((None, d, tf), lambda f, i, te, na: (te[i], 0, f)),
        ],
        out_specs=pl.BlockSpec((tm, tf), lambda f, i, te, na: (i, f)),
        scratch_shapes=[pltpu.VMEM((d, tf), _BF16), pltpu.VMEM((d, tf), _BF16)],
    )
    act = pl.pallas_call(
        _moe_up_kernel,
        grid_spec=up_spec,
        out_shape=jax.ShapeDtypeStruct((rows, ff), _BF16),
        compiler_params=_params("arbitrary", "arbitrary"),
        name="moe_up",
    )(tile_expert, n_active, xs, wg, wu)

    down_spec = pltpu.PrefetchScalarGridSpec(
        num_scalar_prefetch=2,
        grid=(d // tn, n_tiles),
        in_specs=[
            pl.BlockSpec((tm, ff), lambda n, i, te, na: (row_idx(i, na), 0)),
            pl.BlockSpec((None, ff, tn), lambda n, i, te, na: (te[i], 0, n)),
        ],
        out_specs=pl.BlockSpec((tm, tn), lambda n, i, te, na: (i, n)),
        scratch_shapes=[pltpu.VMEM((ff, tn), _BF16)],
    )
    return pl.pallas_call(
        _moe_down_kernel,
        grid_spec=down_spec,
        out_shape=jax.ShapeDtypeStruct((rows, d), _F32),
        compiler_params=_params("arbitrary", "arbitrary"),
        name="moe_down",
    )(tile_expert, n_active, act, wd)


def _combine_kernel(pos_ref, info_ref, h_ref, y_ref, g_ref, out_ref, buf, sem, *, tc, final_norm):
    def start(t, carry):
        for k in range(_TOP_K):
            pltpu.make_async_copy(y_ref.at[pl.ds(pos_ref[_TOP_K * t + k], 1)],
                                  buf.at[k, pl.ds(t, 1)], sem).start()
        return carry

    lax.fori_loop(0, tc, start, 0, unroll=_DMA_ISSUE_UNROLL)
    for k in range(_TOP_K):
        pltpu.make_async_copy(y_ref.at[pl.ds(0, tc)], buf.at[k], sem).wait()

    info = info_ref[...]
    w1 = info[:, _R_W1:_R_W1 + 1]
    w2 = info[:, _R_W2:_R_W2 + 1]
    out = h_ref[...] + (w1 * buf[0] + w2 * buf[1])
    if final_norm:
        out = out * _rms_scale(out) * g_ref[...]
    out_ref[...] = out


def _combine(pos, info, h, y, final_g, final_norm):
    n, d = h.shape
    tc = _largest_tile(n, 512, _SUBLANES)
    kern = functools.partial(_combine_kernel, tc=tc, final_norm=final_norm)
    return pl.pallas_call(
        kern,
        grid=(n // tc,),
        in_specs=[
            pl.BlockSpec((_TOP_K * tc,), lambda i: (i,), memory_space=pltpu.SMEM),
            pl.BlockSpec((tc, _LANES), lambda i: (i, 0)),
            pl.BlockSpec((tc, d), lambda i: (i, 0)),
            pl.BlockSpec(memory_space=pl.ANY),
            pl.BlockSpec((1, d), lambda i: (0, 0)),
        ],
        out_specs=pl.BlockSpec((tc, d), lambda i: (i, 0)),
        out_shape=jax.ShapeDtypeStruct((n, d), _F32),
        scratch_shapes=[pltpu.VMEM((_TOP_K, tc, d), _F32), pltpu.SemaphoreType.DMA(())],
        compiler_params=_params("arbitrary"),
        name="moe_combine",
    )(pos, info, h, y, final_g)


def _final_norm_kernel(h_ref, g_ref, out_ref):
    x = h_ref[...]
    out_ref[...] = x * _rms_scale(x) * g_ref[...]


def _final_norm(h, g):
    n, d = h.shape
    tm = _largest_tile(n, 512, _SUBLANES)
    return pl.pallas_call(
        _final_norm_kernel,
        grid=(n // tm,),
        in_specs=[pl.BlockSpec((tm, d), lambda i: (i, 0)), pl.BlockSpec((1, d), lambda i: (0, 0))],
        out_specs=pl.BlockSpec((tm, d), lambda i: (i, 0)),
        out_shape=jax.ShapeDtypeStruct((n, d), _F32),
        compiler_params=_params("parallel"),
        name="final_norm",
    )(h, g)


_MOE_ROW_TILE = 512


def _moe(h, g2, wr, br, wg, wu, wd, layer, final_g, final_norm):
    n, d = h.shape
    n_exp = wr.shape[1]
    tm = _largest_tile(n, _MOE_ROW_TILE, _SUBLANES)
    wr_pad = jnp.zeros((d, _LANES), _F32).at[:, :n_exp].set(wr)
    br_pad = jnp.zeros((1, _LANES), _F32).at[0, :n_exp].set(br)
    xp, info, cnt = _router(h, g2, wr_pad, br_pad, n_exp)

    counts = cnt[0, :n_exp].astype(jnp.int32)
    tiles_per = (counts + tm - 1) // tm
    tile_end = jnp.cumsum(tiles_per)
    starts = (tile_end - tiles_per) * tm
    n_active = tile_end[-1:]
    n_tiles = (_TOP_K * n) // tm + n_exp
    tile_ids = jnp.minimum(jnp.arange(n_tiles, dtype=jnp.int32), n_active - 1)
    tile_expert = jnp.sum((tile_ids[:, None] >= tile_end[None, :]).astype(jnp.int32), axis=1)
    tile_expert = tile_expert + layer * n_exp
    experts = info[:, _R_E1:_R_E2 + 1].astype(jnp.int32)
    ranks = info[:, _R_RANK1:_R_RANK2 + 1].astype(jnp.int32)
    onehot = experts[:, :, None] == jnp.arange(n_exp, dtype=jnp.int32)[None, None, :]
    pos = (jnp.sum(jnp.where(onehot, starts[None, None, :], 0), axis=-1) + ranks).reshape(-1)

    xs = _dispatch(pos, xp, n_tiles * tm)
    y = _moe_ffn(tile_expert, n_active, xs, wg, wu, wd, tm)
    return _combine(pos, info, h, y, final_g, final_norm)


def _rope_tables(seq, hd):
    axis_dim = hd // 2
    nfreq = axis_dim // 2
    rows = seq // _GRID_W
    row = jnp.broadcast_to(jnp.arange(rows)[:, None], (rows, _GRID_W)).reshape(seq).astype(_F32)
    col = jnp.broadcast_to(jnp.arange(_GRID_W)[None, :], (rows, _GRID_W)).reshape(seq).astype(_F32)
    inv = _ROPE_THETA ** (-jnp.arange(nfreq, dtype=_F32) * 2.0 / axis_dim)
    ar = row[:, None] * inv
    ac = col[:, None] * inv
    ang = jnp.concatenate([ar, ar, ac, ac], axis=-1)
    lane = jnp.arange(hd)
    sign = jnp.where((lane % axis_dim) < nfreq, -1.0, 1.0).astype(_F32)
    return jnp.cos(ang), jnp.sin(ang) * sign


def kernel(x, norm1_g, w_in, q_norm_g, k_norm_g, w_attn_o, w_pool_mix, pool_scale, w_pool_o,
           w_out, norm2_g, ffn_w_gate, ffn_w_up, ffn_w_down, router_w, router_b,
           moe_w_gate, moe_w_up, moe_w_down, final_g):
    batch, seq, d = x.shape
    depth = w_in.shape[0]
    hd = q_norm_g.shape[1]
    q_dim = w_attn_o.shape[1]
    p_dim = pool_scale.shape[1]
    kv_dim = (w_in.shape[2] - q_dim - p_dim - 2 * d) // 2
    dims = (q_dim, kv_dim, p_dim, hd, seq)
    assert seq % _GRID_W == 0 and q_dim % kv_dim == 0 and kv_dim % hd == 0

    cos, sin_signed = _rope_tables(seq, hd)
    bf = lambda w: w.astype(_BF16)
    row = lambda v: v.reshape(1, -1)
    experts = lambda w: w.reshape((-1,) + w.shape[2:])

    w_in, w_attn_o, w_pool_o, w_out = bf(w_in), bf(w_attn_o), bf(w_pool_o), bf(w_out)
    ffn_w_gate, ffn_w_up, ffn_w_down = bf(ffn_w_gate), bf(ffn_w_up), bf(ffn_w_down)
    moe_w_gate, moe_w_up, moe_w_down = experts(moe_w_gate), experts(moe_w_up), experts(moe_w_down)

    h = x.reshape(batch * seq, d)
    fg = row(final_g)
    for l in range(depth):
        q, k, vt, u, gates = _in_proj(h, row(norm1_g[l]), w_in, l, cos, sin_signed,
                                      row(q_norm_g[l]), row(k_norm_g[l]), dims)
        o = _attention(q, k, vt, dims, batch)
        pooled = _pool(u, bf(w_pool_mix[l]), row(pool_scale[l]), seq)
        merged = _merge(o, pooled, w_attn_o, w_pool_o, l, gates)
        h = _out_proj(merged, w_out, l, h)
        i = l // 2
        if l % 2 == 0:
            h = _ffn(h, row(norm2_g[l]), ffn_w_gate, ffn_w_up, ffn_w_down, i)
        else:
            last = l == depth - 1
            h = _moe(h, row(norm2_g[l]), router_w[i], router_b[i], moe_w_gate, moe_w_up,
                     moe_w_down, i, fg, last)
    if depth % 2 == 1 or depth == 0:
        h = _final_norm(h, fg)
    return h.reshape(batch, seq, d)
```

```python
import functools
import math

import jax
import jax.numpy as jnp
from jax import lax
from jax.experimental import pallas as pl
from jax.experimental.pallas import tpu as pltpu

_GRID_W = 64
_ROPE_THETA = 10000.0
_POOL_WINDOWS = (2, 4, 8, 16)
_TOP_K = 2
_NORM_EPS = 1e-6
_LOG2_E = 1.4426950408889634

_LANES = 128
_SUBLANES = 8
_SUM_ROWS = 16
_V7X_VMEM_LIMIT_BYTES = 56 * 1024 * 1024

_F32 = jnp.float32
_BF16 = jnp.bfloat16


def _params(*semantics):
    return pltpu.CompilerParams(dimension_semantics=semantics,
                                vmem_limit_bytes=_V7X_VMEM_LIMIT_BYTES)


def _largest_tile(n, cap, quantum):
    t = min(cap, n)
    t -= t % quantum
    while t > quantum and n % t:
        t -= quantum
    assert t >= quantum and n % t == 0, (n, cap, quantum)
    return t


def _rms_scale(x):
    return lax.rsqrt(jnp.mean(x * x, axis=-1, keepdims=True) + _NORM_EPS)


_ROW_PARTS = 4


def _in_proj_kernel(h_ref, g_ref, w_ref, cos_ref, sin_ref, qg_ref, kg_ref,
                    q_ref, k_ref, vt_ref, u_ref, gate_ref, xn_sc, acc_sc,
                    *, nq, nk, nv, nu, ng, hd, tk, qscale):
    j = pl.program_id(1)
    ncol = nq + nk + nv + nu + ng
    tn = w_ref.shape[1]
    heads = tn // hd

    tm = xn_sc.shape[0]

    def matmul_rows(slot, rows):
        acc_sc[slot, rows, :] = jnp.dot(xn_sc[rows, :], w_ref[...], preferred_element_type=_F32)

    def norm_rope(acc, rows, gain, scale):
        cos = cos_ref[rows, :]
        sin = sin_ref[rows, :]
        lane = lax.broadcasted_iota(jnp.int32, (1, hd), 1)
        first_half = (lane % (hd // 2)) < (hd // 4)
        outs = []
        for hh in range(heads):
            xh = acc[:, hh * hd:(hh + 1) * hd]
            xh = xh * _rms_scale(xh) * gain
            partner = jnp.where(first_half,
                                pltpu.roll(xh, hd - hd // 4, 1),
                                pltpu.roll(xh, hd // 4, 1))
            outs.append(((xh * cos + partner * sin) * scale).astype(_BF16))
        return outs

    def finish_q(acc, rows):
        q_ref[rows, :] = jnp.concatenate(norm_rope(acc, rows, qg_ref[...], qscale), axis=1)

    def finish_k(acc, rows):
        for hh, kh in enumerate(norm_rope(acc, rows, kg_ref[...], 1.0)):
            k_ref[hh, rows, :] = kh

    def finish_v(acc, rows):
        sum_rows = (lax.broadcasted_iota(jnp.int32, (_SUM_ROWS, tk), 0) == 0).astype(_BF16)
        c = rows.start // tk
        for hh in range(heads):
            vt_ref[hh, c, 0:hd, :] = acc[:, hh * hd:(hh + 1) * hd].T.astype(_BF16)
            vt_ref[hh, c, hd:hd + _SUM_ROWS, :] = sum_rows

    def finish_u(acc, rows):
        u_ref[rows, :] = acc

    def finish_gate(acc, rows):
        gate_ref[rows, :] = (0.5 * jnp.tanh(0.5 * acc) + 0.5).astype(_BF16)

    def row_parts(parts):
        return [slice(r * (tm // parts), (r + 1) * (tm // parts)) for r in range(parts)]

    @pl.when(j == 0)
    def _():
        for rows in row_parts(_ROW_PARTS):
            x = h_ref[rows, :]
            xn_sc[rows, :] = (x * _rms_scale(x) * g_ref[...]).astype(_BF16)
            matmul_rows(0, rows)

    t = j - 1
    lo = 0
    for cnt, finish, parts in ((nq, finish_q, 1), (nk, finish_k, 1),
                               (nv, finish_v, tm // tk), (nu, finish_u, _ROW_PARTS),
                               (ng, finish_gate, _ROW_PARTS)):
        @pl.when((t >= lo) & (t < lo + cnt) & (j < ncol))
        def _(finish=finish, parts=parts):
            for rows in row_parts(parts):
                finish(acc_sc[t % 2, rows, :], rows)
                matmul_rows(j % 2, rows)
        lo += cnt

    @pl.when(j == ncol)
    def _():
        finish_gate(acc_sc[t % 2], slice(0, tm))


def _attn_tiles(seq):
    tq = _largest_tile(seq, 1024, _LANES)
    tk = _largest_tile(seq // 2, 512, _LANES)
    assert (seq // tk) % 2 == 0
    return tq, tk


def _in_proj(h, g1, w, layer, cos, sin_signed, qg, kg, dims):
    n, d = h.shape
    q_dim, kv_dim, p_dim, hd, seq = dims
    _, tk = _attn_tiles(seq)
    cols = w.shape[2]
    tm = _largest_tile(seq, 1024, _SUBLANES)
    tn = _largest_tile(_gcd_all(q_dim, kv_dim, p_dim, d), 512, hd)
    nq, nk, nv, nu, ng = q_dim // tn, kv_dim // tn, kv_dim // tn, p_dim // tn, 2 * d // tn
    ncol = nq + nk + nv + nu + ng
    assert ncol * tn == cols and tm % tk == 0
    s_tiles = seq // tm
    heads = tn // hd

    def tile(j, lo, cnt):
        return jnp.clip(j - 1 - lo, 0, cnt - 1)

    kern = functools.partial(_in_proj_kernel, nq=nq, nk=nk, nv=nv, nu=nu, ng=ng, hd=hd, tk=tk,
                             qscale=_LOG2_E * float(hd) ** -0.5)
    return pl.pallas_call(
        kern,
        grid=(n // tm, ncol + 1),
        in_specs=[
            pl.BlockSpec((tm, d), lambda i, j: (i, 0)),
            pl.BlockSpec((1, d), lambda i, j: (0, 0)),
            pl.BlockSpec((None, d, tn), lambda i, j: (layer, 0, jnp.minimum(j, ncol - 1))),
            pl.BlockSpec((tm, hd), lambda i, j: (i % s_tiles, 0)),
            pl.BlockSpec((tm, hd), lambda i, j: (i % s_tiles, 0)),
            pl.BlockSpec((1, hd), lambda i, j: (0, 0)),
            pl.BlockSpec((1, hd), lambda i, j: (0, 0)),
        ],
        out_specs=[
            pl.BlockSpec((tm, tn), lambda i, j: (i, tile(j, 0, nq))),
            pl.BlockSpec((heads, tm, hd), lambda i, j: (tile(j, nq, nk), i, 0)),
            pl.BlockSpec((heads, tm // tk, hd + _SUM_ROWS, tk),
                         lambda i, j: (tile(j, nq + nk, nv), i, 0, 0)),
            pl.BlockSpec((tm, tn), lambda i, j: (i, tile(j, nq + nk + nv, nu))),
            pl.BlockSpec((tm, tn), lambda i, j: (i, tile(j, nq + nk + nv + nu, ng))),
        ],
        out_shape=[
            jax.ShapeDtypeStruct((n, q_dim), _BF16),
            jax.ShapeDtypeStruct((kv_dim // hd, n, hd), _BF16),
            jax.ShapeDtypeStruct((kv_dim // hd, n // tk, hd + _SUM_ROWS, tk), _BF16),
            jax.ShapeDtypeStruct((n, p_dim), _F32),
            jax.ShapeDtypeStruct((n, 2 * d), _BF16),
        ],
        scratch_shapes=[pltpu.VMEM((tm, d), _BF16), pltpu.VMEM((2, tm, tn), _F32)],
        compiler_params=_params("parallel", "arbitrary"),
        name="in_proj",
    )(h, g1, w, cos, sin_signed, qg, kg)


def _gcd_all(*vals):
    g = 0
    for v in vals:
        g = math.gcd(g, v)
    return g


def _attn_kernel(q_ref, k_ref, vt_ref, o_ref, qs_sc, s_sc, cm_sc, acc_sc,
                 *, groups, hd, tq, tk, seq):
    n_chunks = seq // tk
    cols = groups * tq

    for g in range(groups):
        qs_sc[:, g * tq:(g + 1) * tq] = q_ref[:, g * hd:(g + 1) * hd].astype(_F32).T.astype(_BF16)
    acc_sc[...] = jnp.zeros(acc_sc.shape, _F32)

    def scores(c, slot):
        k_c = k_ref[pl.ds(pl.multiple_of(c * tk, tk), tk), :]
        s = jnp.dot(k_c, qs_sc[...], preferred_element_type=_F32)
        s_sc[slot] = s
        cm_sc[slot] = jnp.max(s, axis=0, keepdims=True)

    def softmax_values(c, slot, m_prev):
        m_new = jnp.maximum(m_prev, cm_sc[slot])
        alpha = jnp.exp2(m_prev - m_new)
        p = jnp.exp2(s_sc[slot] - m_new)
        acc_sc[...] = acc_sc[...] * alpha + jnp.dot(vt_ref[c], p.astype(_BF16),
                                                    preferred_element_type=_F32)
        return m_new

    def pair(j, m):
        c = 2 * j
        scores(c + 1, 1)
        m = softmax_values(c, 0, m)
        scores(c + 2, 0)
        return softmax_values(c + 1, 1, m)

    m = jnp.full((1, cols), -jnp.inf, _F32)
    scores(0, 0)
    m = lax.fori_loop(0, n_chunks // 2 - 1, pair, m)
    scores(n_chunks - 1, 1)
    m = softmax_values(n_chunks - 2, 0, m)
    m = softmax_values(n_chunks - 1, 1, m)

    out_t = acc_sc[0:hd, :] / acc_sc[hd:hd + 1, :]
    for g in range(groups):
        o_ref[:, g * hd:(g + 1) * hd] = out_t[:, g * tq:(g + 1) * tq].T.astype(o_ref.dtype)


def _attention(q, k, vt, dims, batch):
    q_dim, kv_dim, _, hd, seq = dims
    n_kv = kv_dim // hd
    groups = q_dim // kv_dim
    tq, tk = _attn_tiles(seq)
    q3 = q.reshape(batch, seq, q_dim)
    cols = groups * tq
    kern = functools.partial(_attn_kernel, groups=groups, hd=hd, tq=tq, tk=tk, seq=seq)
    o = pl.pallas_call(
        kern,
        grid=(batch, n_kv, seq // tq),
        in_specs=[
            pl.BlockSpec((None, tq, groups * hd), lambda b, kh, i: (b, i, kh)),
            pl.BlockSpec((None, seq, hd), lambda b, kh, i: (kh, b, 0)),
            pl.BlockSpec((None, seq // tk, hd + _SUM_ROWS, tk), lambda b, kh, i: (kh, b, 0, 0)),
        ],
        out_specs=pl.BlockSpec((None, tq, groups * hd), lambda b, kh, i: (b, i, kh)),
        out_shape=jax.ShapeDtypeStruct((batch, seq, q_dim), _BF16),
        scratch_shapes=[
            pltpu.VMEM((hd, cols), _BF16),
            pltpu.VMEM((2, tk, cols), _F32),
            pltpu.VMEM((2, 1, cols), _F32),
            pltpu.VMEM((hd + _SUM_ROWS, cols), _F32),
        ],
        compiler_params=_params("parallel", "parallel", "parallel"),
        name="attention",
    )(q3, k, vt)
    return o.reshape(batch * seq, q_dim)


_POOL_HALO = 8


def _pool_kernel(prev_ref, u_ref, next_ref, wmix_ref, scale_ref, o_ref, ext_sc, run_sc,
                 *, tm, seq, grp):
    halo = _POOL_HALO
    tile = pl.program_id(0) % (seq // tm)
    ext_sc[0:halo, :] = jnp.where(tile == 0, 0.0, prev_ref[...])
    ext_sc[halo:halo + tm, :] = u_ref[...]
    ext_sc[halo + tm:halo + tm + halo, :] = jnp.where(tile == seq // tm - 1, 0.0, next_ref[...])
    t = tile * tm + lax.broadcasted_iota(jnp.int32, (tm, 1), 0)
    for g, w in enumerate(_POOL_WINDOWS):
        sl = slice(g * grp, (g + 1) * grp)
        run = ext_sc[:, sl]
        span = 1
        while span < w:
            run = run + pltpu.roll(run, span, 0)
            span *= 2
        run_sc[...] = run
        end = halo + w // 2 - 1
        wsum = run_sc[end:end + tm, :]
        lo = jnp.maximum(t - w // 2, 0)
        hi = jnp.minimum(t + w // 2 - 1, seq - 1)
        cnt = (hi - lo + 1).astype(_F32)
        dev = wsum / cnt - u_ref[:, sl]
        y = jnp.dot(dev.astype(_BF16), wmix_ref[g], preferred_element_type=_F32)
        o_ref[:, sl] = (y * scale_ref[:, sl]).astype(o_ref.dtype)


def _pool(u, wmix, scale, seq):
    n, p = u.shape
    grp = p // len(_POOL_WINDOWS)
    tm = _largest_tile(seq, 512, _POOL_HALO)
    halo = _POOL_HALO
    hb = tm // halo
    kern = functools.partial(_pool_kernel, tm=tm, seq=seq, grp=grp)
    return pl.pallas_call(
        kern,
        grid=(n // tm,),
        in_specs=[
            pl.BlockSpec((halo, p), lambda i: (jnp.maximum(i * hb - 1, 0), 0)),
            pl.BlockSpec((tm, p), lambda i: (i, 0)),
            pl.BlockSpec((halo, p), lambda i: (jnp.minimum((i + 1) * hb, n // halo - 1), 0)),
            pl.BlockSpec(wmix.shape, lambda i: (0, 0, 0)),
            pl.BlockSpec((1, p), lambda i: (0, 0)),
        ],
        out_specs=pl.BlockSpec((tm, p), lambda i: (i, 0)),
        out_shape=jax.ShapeDtypeStruct((n, p), _BF16),
        scratch_shapes=[pltpu.VMEM((tm + 2 * halo, p), _F32),
                        pltpu.VMEM((tm + 2 * halo, grp), _F32)],
        compiler_params=_params("parallel"),
        name="pool",
    )(u, u, u, wmix, scale)


def _merge_kernel(o_ref, p_ref, wa_ref, wp_ref, ga_ref, gp_ref, out_ref):
    a = jnp.dot(o_ref[...], wa_ref[...], preferred_element_type=_F32)
    p = jnp.dot(p_ref[...], wp_ref[...], preferred_element_type=_F32)
    merged = ga_ref[...].astype(_F32) * a + gp_ref[...].astype(_F32) * p
    out_ref[...] = merged.astype(out_ref.dtype)


def _merge(o, pooled, wa, wp, layer, gates):
    n, q_dim = o.shape
    p_dim = pooled.shape[1]
    d = wa.shape[2]
    tm = _largest_tile(n, 1024, _SUBLANES)
    tn = _largest_tile(d, 1024, _LANES)
    nj = d // tn
    return pl.pallas_call(
        _merge_kernel,
        grid=(n // tm, nj),
        in_specs=[
            pl.BlockSpec((tm, q_dim), lambda i, j: (i, 0)),
            pl.BlockSpec((tm, p_dim), lambda i, j: (i, 0)),
            pl.BlockSpec((None, q_dim, tn), lambda i, j: (layer, 0, j)),
            pl.BlockSpec((None, p_dim, tn), lambda i, j: (layer, 0, j)),
            pl.BlockSpec((tm, tn), lambda i, j: (i, j)),
            pl.BlockSpec((tm, tn), lambda i, j: (i, nj + j)),
        ],
        out_specs=pl.BlockSpec((tm, tn), lambda i, j: (i, j)),
        out_shape=jax.ShapeDtypeStruct((n, d), _BF16),
        compiler_params=_params("parallel", "arbitrary"),
        name="merge",
    )(o, pooled, wa, wp, gates, gates)


def _out_proj_kernel(a_ref, w_ref, res_ref, out_ref):
    out_ref[...] = res_ref[...] + jnp.dot(a_ref[...], w_ref[...], preferred_element_type=_F32)


def _out_proj(merged, w, layer, h):
    n, d = h.shape
    tm = _largest_tile(n, 1024, _SUBLANES)
    tn = _largest_tile(d, 1024, _LANES)
    return pl.pallas_call(
        _out_proj_kernel,
        grid=(n // tm, d // tn),
        in_specs=[
            pl.BlockSpec((tm, merged.shape[1]), lambda i, j: (i, 0)),
            pl.BlockSpec((None, merged.shape[1], tn), lambda i, j: (layer, 0, j)),
            pl.BlockSpec((tm, tn), lambda i, j: (i, j)),
        ],
        out_specs=pl.BlockSpec((tm, tn), lambda i, j: (i, j)),
        out_shape=jax.ShapeDtypeStruct((n, d), _F32),
        compiler_params=_params("parallel", "arbitrary"),
        name="out_proj",
    )(merged, w, h)


def _swiglu_step(xn, wg_ref, wu_ref, wd_ref):
    a = jnp.dot(xn, wg_ref[...], preferred_element_type=_F32)
    b = jnp.dot(xn, wu_ref[...], preferred_element_type=_F32)
    act = (a * jax.nn.sigmoid(a) * b).astype(_BF16)
    return jnp.dot(act, wd_ref[...], preferred_element_type=_F32)


def _ffn_kernel(h_ref, g_ref, wg_ref, wu_ref, wd_ref, out_ref, xn_sc):
    @pl.when(pl.program_id(1) == 0)
    def _():
        x = h_ref[...]
        xn_sc[...] = (x * _rms_scale(x) * g_ref[...]).astype(_BF16)
        out_ref[...] = x

    out_ref[...] += _swiglu_step(xn_sc[...], wg_ref, wu_ref, wd_ref)


def _ffn(h, g2, wg, wu, wd, layer):
    n, d = h.shape
    ff = wg.shape[2]
    tm = _largest_tile(n, 512, _SUBLANES)
    tf = _largest_tile(ff, 512, _LANES)
    return pl.pallas_call(
        _ffn_kernel,
        grid=(n // tm, ff // tf),
        in_specs=[
            pl.BlockSpec((tm, d), lambda i, f: (i, 0)),
            pl.BlockSpec((1, d), lambda i, f: (0, 0)),
            pl.BlockSpec((None, d, tf), lambda i, f: (layer, 0, f)),
            pl.BlockSpec((None, d, tf), lambda i, f: (layer, 0, f)),
            pl.BlockSpec((None, tf, d), lambda i, f: (layer, f, 0)),
        ],
        out_specs=pl.BlockSpec((tm, d), lambda i, f: (i, 0)),
        out_shape=jax.ShapeDtypeStruct((n, d), _F32),
        scratch_shapes=[pltpu.VMEM((tm, d), _BF16)],
        compiler_params=_params("parallel", "arbitrary"),
        name="ffn_dense",
    )(h, g2, wg, wu, wd)


_R_E1, _R_E2, _R_RANK1, _R_RANK2, _R_W1, _R_W2 = range(6)


def _router_kernel(h_ref, g_ref, wr_ref, br_ref, xp_ref, info_ref, cnt_ref, carry_sc, *, n_exp):
    @pl.when(pl.program_id(0) == 0)
    def _():
        carry_sc[...] = jnp.zeros(carry_sc.shape, _F32)

    x = h_ref[...]
    xn = x * _rms_scale(x) * g_ref[...]
    tm, d = xn.shape

    xb = xn.astype(_BF16).astype(_F32)
    lo = lax.bitcast_convert_type(xb[:, :d // 2], jnp.int32)
    hi = lax.bitcast_convert_type(xb[:, d // 2:], jnp.int32)
    xp_ref[...] = lax.shift_right_logical(lo, 16) | hi

    wr = wr_ref[...]
    x_hi = xn.astype(_BF16)
    x_lo = (xn - x_hi.astype(_F32)).astype(_BF16)
    w_hi = wr.astype(_BF16)
    w_lo = (wr - w_hi.astype(_F32)).astype(_BF16)
    logits = (jnp.dot(x_hi, w_hi, preferred_element_type=_F32)
              + jnp.dot(x_lo, w_hi, preferred_element_type=_F32)
              + jnp.dot(x_hi, w_lo, preferred_element_type=_F32)) + br_ref[...]
    lane = lax.broadcasted_iota(jnp.int32, logits.shape, 1)
    logits = jnp.where(lane < n_exp, logits, -jnp.inf)
    v1 = jnp.max(logits, axis=-1, keepdims=True)
    i1 = jnp.min(jnp.where(logits == v1, lane, _LANES), axis=-1, keepdims=True)
    pick1 = lane == i1
    rest = jnp.where(pick1, -jnp.inf, logits)
    v2 = jnp.max(rest, axis=-1, keepdims=True)
    i2 = jnp.min(jnp.where(rest == v2, lane, _LANES), axis=-1, keepdims=True)
    pick2 = lane == i2
    e = jnp.exp(v2 - v1)
    w1 = 1.0 / (1.0 + e)
    w2 = e / (1.0 + e)

    cnt = (pick1 | pick2).astype(_F32)
    row = lax.broadcasted_iota(jnp.int32, (tm, tm), 0)
    col = lax.broadcasted_iota(jnp.int32, (tm, tm), 1)
    below = (row > col).astype(_BF16)
    before = jnp.dot(below, cnt.astype(_BF16), preferred_element_type=_F32) + carry_sc[...]
    rank1 = jnp.sum(jnp.where(pick1, before, 0.0), axis=-1, keepdims=True)
    rank2 = jnp.sum(jnp.where(pick2, before, 0.0), axis=-1, keepdims=True)
    carry_sc[...] += jnp.sum(cnt, axis=0, keepdims=True)

    info = jnp.zeros(logits.shape, _F32)
    for slot, val in ((_R_E1, i1.astype(_F32)), (_R_E2, i2.astype(_F32)), (_R_RANK1, rank1),
                      (_R_RANK2, rank2), (_R_W1, w1), (_R_W2, w2)):
        info = jnp.where(lane == slot, val, info)
    info_ref[...] = info
    cnt_ref[...] = jnp.broadcast_to(carry_sc[...], cnt_ref.shape)


def _router(h, g2, wr_pad, br_pad, n_exp):
    n, d = h.shape
    tm = _largest_tile(n, 512, _SUBLANES)
    kern = functools.partial(_router_kernel, n_exp=n_exp)
    return pl.pallas_call(
        kern,
        grid=(n // tm,),
        in_specs=[
            pl.BlockSpec((tm, d), lambda i: (i, 0)),
            pl.BlockSpec((1, d), lambda i: (0, 0)),
            pl.BlockSpec((d, _LANES), lambda i: (0, 0)),
            pl.BlockSpec((1, _LANES), lambda i: (0, 0)),
        ],
        out_specs=[
            pl.BlockSpec((tm, d // 2), lambda i: (i, 0)),
            pl.BlockSpec((tm, _LANES), lambda i: (i, 0)),
            pl.BlockSpec((_SUBLANES, _LANES), lambda i: (0, 0)),
        ],
        out_shape=[
            jax.ShapeDtypeStruct((n, d // 2), jnp.int32),
            jax.ShapeDtypeStruct((n, _LANES), _F32),
            jax.ShapeDtypeStruct((_SUBLANES, _LANES), _F32),
        ],
        scratch_shapes=[pltpu.VMEM((1, _LANES), _F32)],
        compiler_params=_params("arbitrary"),
        name="router",
    )(h, g2, wr_pad, br_pad)


_DMA_ISSUE_UNROLL = 8


def _row_copy(src_ref, src_row, dst_ref, dst_row, sem):
    return pltpu.make_async_copy(src_ref.at[pl.ds(src_row, 1)], dst_ref.at[pl.ds(dst_row, 1)], sem)


def _dispatch_kernel(pos_ref, xp_ref, zeros_ref, xs_ref, sem, *, tc):
    del zeros_ref

    def start(t, carry):
        for k in range(_TOP_K):
            _row_copy(xp_ref, t, xs_ref, pos_ref[_TOP_K * t + k], sem).start(priority=k % 2)
        return carry

    lax.fori_loop(0, tc, start, 0, unroll=_DMA_ISSUE_UNROLL)
    for _ in range(_TOP_K):
        pltpu.make_async_copy(xp_ref, xs_ref.at[pl.ds(0, tc)], sem).wait()


def _dispatch(pos, xp, rows):
    n, half = xp.shape
    tc = _largest_tile(n, 512, _SUBLANES)
    kern = functools.partial(_dispatch_kernel, tc=tc)
    return pl.pallas_call(
        kern,
        grid=(n // tc,),
        in_specs=[
            pl.BlockSpec((_TOP_K * tc,), lambda i: (i,), memory_space=pltpu.SMEM),
            pl.BlockSpec((tc, half), lambda i: (i, 0)),
            pl.BlockSpec(memory_space=pl.ANY),
        ],
        out_specs=pl.BlockSpec(memory_space=pl.ANY),
        out_shape=jax.ShapeDtypeStruct((rows, half), jnp.int32),
        scratch_shapes=[pltpu.SemaphoreType.DMA(())],
        input_output_aliases={2: 0},
        compiler_params=_params("arbitrary"),
        name="moe_dispatch",
    )(pos, xp, jnp.zeros((rows, half), jnp.int32))


def _expert_changed(te_ref, i):
    return (i == 0) | (te_ref[i] != te_ref[jnp.maximum(i - 1, 0)])


def _moe_up_kernel(te_ref, nact_ref, xs_ref, wg_ref, wu_ref, act_ref, wg_sc, wu_sc):
    i = pl.program_id(1)
    active = i < nact_ref[0]

    @pl.when(active & _expert_changed(te_ref, i))
    def _():
        wg_sc[...] = wg_ref[...].astype(_BF16)
        wu_sc[...] = wu_ref[...].astype(_BF16)

    @pl.when(active)
    def _():
        packed = xs_ref[...]
        lo = lax.bitcast_convert_type(lax.shift_left(packed, 16), _F32)
        hi = lax.bitcast_convert_type(packed & jnp.int32(-65536), _F32)
        x = jnp.concatenate([lo, hi], axis=1).astype(_BF16)
        a = jnp.dot(x, wg_sc[...], preferred_element_type=_F32)
        b = jnp.dot(x, wu_sc[...], preferred_element_type=_F32)
        act_ref[...] = (a * jax.nn.sigmoid(a) * b).astype(act_ref.dtype)

    @pl.when(jnp.logical_not(active))
    def _():
        act_ref[...] = jnp.zeros(act_ref.shape, act_ref.dtype)


def _moe_down_kernel(te_ref, nact_ref, act_ref, wd_ref, y_ref, wd_sc):
    i = pl.program_id(1)
    active = i < nact_ref[0]

    @pl.when(active & _expert_changed(te_ref, i))
    def _():
        wd_sc[...] = wd_ref[...].astype(_BF16)

    @pl.when(active)
    def _():
        y_ref[...] = jnp.dot(act_ref[...], wd_sc[...], preferred_element_type=_F32)

    @pl.when(jnp.logical_not(active))
    def _():
        y_ref[...] = jnp.zeros(y_ref.shape, y_ref.dtype)


def _moe_ffn(tile_expert, n_active, xs, wg, wu, wd, tm):
    rows, half = xs.shape
    d = 2 * half
    ff = wg.shape[2]
    tf = _largest_tile(ff, 512, _LANES)
    tn = _largest_tile(d, 512, _LANES)
    n_tiles = rows // tm

    def row_idx(i, nact):
        return jnp.maximum(jnp.minimum(i, nact[0] - 1), 0)

    up_spec = pltpu.PrefetchScalarGridSpec(
        num_scalar_prefetch=2,
        grid=(ff // tf, n_tiles),
        in_specs=[
            pl.BlockSpec((tm, half), lambda f, i, te, na: (row_idx(i, na), 0)),
            pl.BlockSpec((None, d, tf), lambda f, i, te, na: (te[i], 0, f)),
            pl.BlockSpec((None, d, tf), lambda f, i, te, na: (te[i], 0, f)),
        ],
        out_specs=pl.BlockSpec((tm, tf), lambda f, i, te, na: (i, f)),
        scratch_shapes=[pltpu.VMEM((d, tf), _BF16), pltpu.VMEM((d, tf), _BF16)],
    )
    act = pl.pallas_call(
        _moe_up_kernel,
        grid_spec=up_spec,
        out_shape=jax.ShapeDtypeStruct((rows, ff), _BF16),
        compiler_params=_params("arbitrary", "arbitrary"),
        name="moe_up",
    )(tile_expert, n_active, xs, wg, wu)

    down_spec = pltpu.PrefetchScalarGridSpec(
        num_scalar_prefetch=2,
        grid=(d // tn, n_tiles),
        in_specs=[
            pl.BlockSpec((tm, ff), lambda n, i, te, na: (row_idx(i, na), 0)),
            pl.BlockSpec((None, ff, tn), lambda n, i, te, na: (te[i], 0, n)),
        ],
        out_specs=pl.BlockSpec((tm, tn), lambda n, i, te, na: (i, n)),
        scratch_shapes=[pltpu.VMEM((ff, tn), _BF16)],
    )
    return pl.pallas_call(
        _moe_down_kernel,
        grid_spec=down_spec,
        out_shape=jax.ShapeDtypeStruct((rows, d), _F32),
        compiler_params=_params("arbitrary", "arbitrary"),
        name="moe_down",
    )(tile_expert, n_active, act, wd)


def _combine_kernel(pos_ref, info_ref, h_ref, y_ref, g_ref, out_ref, buf, sem, *, tc, final_norm):
    def start(t, carry):
        for k in range(_TOP_K):
            pltpu.make_async_copy(y_ref.at[pl.ds(pos_ref[_TOP_K * t + k], 1)],
                                  buf.at[k, pl.ds(t, 1)], sem).start(priority=k % 2)
        return carry

    lax.fori_loop(0, tc, start, 0, unroll=_DMA_ISSUE_UNROLL)
    for k in range(_TOP_K):
        pltpu.make_async_copy(y_ref.at[pl.ds(0, tc)], buf.at[k], sem).wait()

    info = info_ref[...]
    w1 = info[:, _R_W1:_R_W1 + 1]
    w2 = info[:, _R_W2:_R_W2 + 1]
    out = h_ref[...] + (w1 * buf[0] + w2 * buf[1])
    if final_norm:
        out = out * _rms_scale(out) * g_ref[...]
    out_ref[...] = out


def _combine(pos, info, h, y, final_g, final_norm):
    n, d = h.shape
    tc = _largest_tile(n, 512, _SUBLANES)
    kern = functools.partial(_combine_kernel, tc=tc, final_norm=final_norm)
    return pl.pallas_call(
        kern,
        grid=(n // tc,),
        in_specs=[
            pl.BlockSpec((_TOP_K * tc,), lambda i: (i,), memory_space=pltpu.SMEM),
            pl.BlockSpec((tc, _LANES), lambda i: (i, 0)),
            pl.BlockSpec((tc, d), lambda i: (i, 0)),
            pl.BlockSpec(memory_space=pl.ANY),
            pl.BlockSpec((1, d), lambda i: (0, 0)),
        ],
        out_specs=pl.BlockSpec((tc, d), lambda i: (i, 0)),
        out_shape=jax.ShapeDtypeStruct((n, d), _F32),
        scratch_shapes=[pltpu.VMEM((_TOP_K, tc, d), _F32), pltpu.SemaphoreType.DMA(())],
        compiler_params=_params("arbitrary"),
        name="moe_combine",
    )(pos, info, h, y, final_g)


def _final_norm_kernel(h_ref, g_ref, out_ref):
    x = h_ref[...]
    out_ref[...] = x * _rms_scale(x) * g_ref[...]


def _final_norm(h, g):
    n, d = h.shape
    tm = _largest_tile(n, 512, _SUBLANES)
    return pl.pallas_call(
        _final_norm_kernel,
        grid=(n // tm,),
        in_specs=[pl.BlockSpec((tm, d), lambda i: (i, 0)), pl.BlockSpec((1, d), lambda i: (0, 0))],
        out_specs=pl.BlockSpec((tm, d), lambda i: (i, 0)),
        out_shape=jax.ShapeDtypeStruct((n, d), _F32),
        compiler_params=_params("parallel"),
        name="final_norm",
    )(h, g)


_MOE_ROW_TILE = 512


def _moe(h, g2, wr, br, wg, wu, wd, layer, final_g, final_norm):
    n, d = h.shape
    n_exp = wr.shape[1]
    tm = _largest_tile(n, _MOE_ROW_TILE, _SUBLANES)
    wr_pad = jnp.zeros((d, _LANES), _F32).at[:, :n_exp].set(wr)
    br_pad = jnp.zeros((1, _LANES), _F32).at[0, :n_exp].set(br)
    xp, info, cnt = _router(h, g2, wr_pad, br_pad, n_exp)

    counts = cnt[0, :n_exp].astype(jnp.int32)
    tiles_per = (counts + tm - 1) // tm
    tile_end = jnp.cumsum(tiles_per)
    starts = (tile_end - tiles_per) * tm
    n_active = tile_end[-1:]
    n_tiles = (_TOP_K * n) // tm + n_exp
    tile_ids = jnp.minimum(jnp.arange(n_tiles, dtype=jnp.int32), n_active - 1)
    tile_expert = jnp.sum((tile_ids[:, None] >= tile_end[None, :]).astype(jnp.int32), axis=1)
    tile_expert = tile_expert + layer * n_exp
    experts = info[:, _R_E1:_R_E2 + 1].astype(jnp.int32)
    ranks = info[:, _R_RANK1:_R_RANK2 + 1].astype(jnp.int32)
    onehot = experts[:, :, None] == jnp.arange(n_exp, dtype=jnp.int32)[None, None, :]
    pos = (jnp.sum(jnp.where(onehot, starts[None, None, :], 0), axis=-1) + ranks).reshape(-1)

    xs = _dispatch(pos, xp, n_tiles * tm)
    y = _moe_ffn(tile_expert, n_active, xs, wg, wu, wd, tm)
    return _combine(pos, info, h, y, final_g, final_norm)


def _rope_tables(seq, hd):
    axis_dim = hd // 2
    nfreq = axis_dim // 2
    rows = seq // _GRID_W
    row = jnp.broadcast_to(jnp.arange(rows)[:, None], (rows, _GRID_W)).reshape(seq).astype(_F32)
    col = jnp.broadcast_to(jnp.arange(_GRID_W)[None, :], (rows, _GRID_W)).reshape(seq).astype(_F32)
    inv = _ROPE_THETA ** (-jnp.arange(nfreq, dtype=_F32) * 2.0 / axis_dim)
    ar = row[:, None] * inv
    ac = col[:, None] * inv
    ang = jnp.concatenate([ar, ar, ac, ac], axis=-1)
    lane = jnp.arange(hd)
    sign = jnp.where((lane % axis_dim) < nfreq, -1.0, 1.0).astype(_F32)
    return jnp.cos(ang), jnp.sin(ang) * sign


def kernel(x, norm1_g, w_in, q_norm_g, k_norm_g, w_attn_o, w_pool_mix, pool_scale, w_pool_o,
           w_out, norm2_g, ffn_w_gate, ffn_w_up, ffn_w_down, router_w, router_b,
           moe_w_gate, moe_w_up, moe_w_down, final_g):
    batch, seq, d = x.shape
    depth = w_in.shape[0]
    hd = q_norm_g.shape[1]
    q_dim = w_attn_o.shape[1]
    p_dim = pool_scale.shape[1]
    kv_dim = (w_in.shape[2] - q_dim - p_dim - 2 * d) // 2
    dims = (q_dim, kv_dim, p_dim, hd, seq)
    assert seq % _GRID_W == 0 and q_dim % kv_dim == 0 and kv_dim % hd == 0

    cos, sin_signed = _rope_tables(seq, hd)
    bf = lambda w: w.astype(_BF16)
    row = lambda v: v.reshape(1, -1)
    experts = lambda w: w.reshape((-1,) + w.shape[2:])

    w_in, w_attn_o, w_pool_o, w_out = bf(w_in), bf(w_attn_o), bf(w_pool_o), bf(w_out)
    ffn_w_gate, ffn_w_up, ffn_w_down = bf(ffn_w_gate), bf(ffn_w_up), bf(ffn_w_down)
    moe_w_gate, moe_w_up, moe_w_down = experts(moe_w_gate), experts(moe_w_up), experts(moe_w_down)

    h = x.reshape(batch * seq, d)
    fg = row(final_g)
    for l in range(depth):
        q, k, vt, u, gates = _in_proj(h, row(norm1_g[l]), w_in, l, cos, sin_signed,
                                      row(q_norm_g[l]), row(k_norm_g[l]), dims)
        o = _attention(q, k, vt, dims, batch)
        pooled = _pool(u, bf(w_pool_mix[l]), row(pool_scale[l]), seq)
        merged = _merge(o, pooled, w_attn_o, w_pool_o, l, gates)
        h = _out_proj(merged, w_out, l, h)
        i = l // 2
        if l % 2 == 0:
            h = _ffn(h, row(norm2_g[l]), ffn_w_gate, ffn_w_up, ffn_w_down, i)
        else:
            last = l == depth - 1
            h = _moe(h, row(norm2_g[l]), router_w[i], router_b[i], moe_w_gate, moe_w_up,
                     moe_w_down, i, fg, last)
    if depth % 2 == 1 or depth == 0:
        h = _final_norm(h, fg)
    return h.reshape(batch, seq, d)
```
